```python
import math
import jax, jax.numpy as jnp
from jax import lax
import numpy as np

D_MODEL = 2048
BATCH = 1
SEQ = 16384
DEPTH = 2

GRID_W = 64
CTX_LEN = 256

HEAD_DIM = 128
ATTN_WIDTH = D_MODEL // 2
N_Q_HEADS = ATTN_WIDTH // HEAD_DIM
N_KV_HEADS = 2
Q_PER_KV = N_Q_HEADS // N_KV_HEADS
KV_WIDTH = N_KV_HEADS * HEAD_DIM
Q_BLOCK = 128
ROPE_THETA = 10000.0
ROPE_PAIRS = HEAD_DIM // 4

SSM_WIDTH = D_MODEL // 4
SSM_GROUP = 16
N_SSM_GROUPS = SSM_WIDTH // SSM_GROUP
SSM_STATE = 64
DT_MIN = 0.001
DT_MAX = 0.1

FOURIER_WIDTH = D_MODEL - ATTN_WIDTH - SSM_WIDTH
FOURIER_HEADS = 4
FOURIER_HEAD_DIM = FOURIER_WIDTH // FOURIER_HEADS

Q_END = ATTN_WIDTH
K_END = Q_END + KV_WIDTH
V_END = K_END + KV_WIDTH
S_END = V_END + SSM_WIDTH
IN_WIDTH = S_END + FOURIER_WIDTH

FFN_HIDDEN = ((8 * D_MODEL // 3 + 255) // 256) * 256
NORM_EPS = 1e-6

kernel_name = 'hybrid_gqa_s5_fnet_diffusion_block'


def rms_norm(x, g):
    x32 = x.astype(jnp.float32)
    r = x32 * lax.rsqrt(jnp.mean(x32 * x32, axis=-1, keepdims=True) + NORM_EPS)
    return (r * g.astype(jnp.float32)).astype(x.dtype)


def modulate(h, shift, scale):
    return h * (1 + scale) + shift


def axial_rope_tables(rows):
    row = jnp.repeat(jnp.arange(rows, dtype=jnp.float32), GRID_W)
    col = jnp.tile(jnp.arange(GRID_W, dtype=jnp.float32), rows)
    freqs = ROPE_THETA ** (-jnp.arange(ROPE_PAIRS, dtype=jnp.float32) / ROPE_PAIRS)
    ang_r = (row[:, None] * freqs)[:, None, :]
    ang_c = (col[:, None] * freqs)[:, None, :]
    return (jnp.cos(ang_r), jnp.sin(ang_r), jnp.cos(ang_c), jnp.sin(ang_c))


def apply_axial_rope(x, tabs):
    cr, sr, cc, sc = [t.astype(x.dtype) for t in tabs]
    x1, x2, x3, x4 = jnp.split(x, 4, axis=-1)
    return jnp.concatenate([x1 * cr - x2 * sr, x2 * cr + x1 * sr,
                            x3 * cc - x4 * sc, x4 * cc + x3 * sc], axis=-1)


def attend(q, k, v):
    s = jnp.einsum('bqkgd,bskd->bkgqs', q, k).astype(jnp.float32) * (HEAD_DIM ** -0.5)
    p = jax.nn.softmax(s, axis=-1).astype(v.dtype)
    return jnp.einsum('bkgqs,bskd->bqkgd', p, v)


def latent_attention(q, k_all, v_all):
    b, l = q.shape[:2]
    nb = l // Q_BLOCK
    qb = jnp.moveaxis(q.reshape(b, nb, Q_BLOCK, N_KV_HEADS, Q_PER_KV, HEAD_DIM), 1, 0)
    o = lax.map(lambda qblk: attend(qblk, k_all, v_all), qb)
    return jnp.moveaxis(o, 0, 1).reshape(b, l, ATTN_WIDTH)


def s5_discretize(lam_re, lam_im, log_dt, b_re, b_im):
    lr = lam_re.astype(jnp.float32)
    li = lam_im.astype(jnp.float32)
    dt = jnp.exp(log_dt.astype(jnp.float32))[:, None]
    mag = jnp.exp(lr * dt)
    ar = mag * jnp.cos(li * dt)
    ai = mag * jnp.sin(li * dt)
    den = lr * lr + li * li
    cr = ((ar - 1.0) * lr + ai * li) / den
    ci = (ai * lr - (ar - 1.0) * li) / den
    br = b_re.astype(jnp.float32)
    bi = b_im.astype(jnp.float32)
    bbr = cr[..., None] * br - ci[..., None] * bi
    bbi = cr[..., None] * bi + ci[..., None] * br
    return ar, ai, bbr, bbi


def s5_combine(e1, e2):
    a1r, a1i, b1r, b1i = e1
    a2r, a2i, b2r, b2i = e2
    return (a2r * a1r - a2i * a1i, a2r * a1i + a2i * a1r,
            a2r * b1r - a2i * b1i + b2r, a2r * b1i + a2i * b1r + b2i)


def s5_scan(u, ar, ai, bbr, bbi, h0, reverse):
    l = u.shape[1]
    bu_r = jnp.einsum('blgh,gph->blgp', u, bbr)
    bu_i = jnp.einsum('blgh,gph->blgp', u, bbi)
    if h0 is not None:
        h0r, h0i = h0
        edge = l - 1 if reverse else 0
        bu_r = bu_r.at[:, edge].add(ar * h0r - ai * h0i)
        bu_i = bu_i.at[:, edge].add(ar * h0i + ai * h0r)
    shape = bu_r.shape
    elems = (jnp.broadcast_to(ar, shape), jnp.broadcast_to(ai, shape), bu_r, bu_i)
    _, _, hr, hi = lax.associative_scan(s5_combine, elems, reverse=reverse, axis=1)
    return hr, hi


def s5_readout(hr, hi, c_re, c_im):
    return (jnp.einsum('blgp,ghp->blgh', hr, c_re.astype(jnp.float32))
            - jnp.einsum('blgp,ghp->blgh', hi, c_im.astype(jnp.float32)))


def s5_bidirectional(u, uc, lam_re, lam_im, log_dt, b_re, b_im, c_re, c_im, d, need_ctx):
    b, l = u.shape[:2]
    u32 = u.astype(jnp.float32).reshape(b, l, N_SSM_GROUPS, SSM_GROUP)
    uc32 = uc.astype(jnp.float32).reshape(b, uc.shape[1], N_SSM_GROUPS, SSM_GROUP)
    dd = d.astype(jnp.float32).reshape(N_SSM_GROUPS, SSM_GROUP)
    y = dd * u32
    yc = dd * uc32 if need_ctx else None
    for direction in range(2):
        rev = direction == 1
        ar, ai, bbr, bbi = s5_discretize(lam_re[direction], lam_im[direction],
                                         log_dt[direction], b_re[direction], b_im[direction])
        hcr, hci = s5_scan(uc32, ar, ai, bbr, bbi, None, rev)
        end = 0 if rev else -1
        hr, hi = s5_scan(u32, ar, ai, bbr, bbi, (hcr[:, end], hci[:, end]), rev)
        y = y + s5_readout(hr, hi, c_re[direction], c_im[direction])
        if need_ctx:
            yc = yc + s5_readout(hcr, hci, c_re[direction], c_im[direction])
    y = y.reshape(b, l, SSM_WIDTH).astype(u.dtype)
    if need_ctx:
        yc = yc.reshape(b, uc.shape[1], SSM_WIDTH).astype(uc.dtype)
    return y, yc


def s5_glu(y, w, bias):
    g = jax.nn.gelu(y)
    return g * jax.nn.sigmoid(g @ w + bias)


def fourier_mix(u):
    b, l = u.shape[:2]
    u32 = u.astype(jnp.float32).reshape(b, l, FOURIER_HEADS, FOURIER_HEAD_DIM)
    f = jnp.fft.fftn(u32, axes=(1, 3), norm='ortho').real
    return f.reshape(b, l, FOURIER_WIDTH).astype(u.dtype)


def swiglu(h, w_gate, w_up, w_down):
    return (jax.nn.silu(h @ w_gate) * (h @ w_up)) @ w_down


def mixer(h, hc, tabs, w_in, q_norm, k_norm, lam_re, lam_im, log_dt, b_re, b_im,
          c_re, c_im, ssm_d, glu_w, glu_b, fourier_w, w_out, need_ctx):
    b, l, _ = h.shape
    n_ctx = hc.shape[1]
    proj = h @ w_in
    q = rms_norm(proj[..., :Q_END].reshape(b, l, N_Q_HEADS, HEAD_DIM), q_norm)
    q = apply_axial_rope(q, tabs).reshape(b, l, N_KV_HEADS, Q_PER_KV, HEAD_DIM)
    k = apply_axial_rope(rms_norm(proj[..., Q_END:K_END].reshape(b, l, N_KV_HEADS, HEAD_DIM), k_norm), tabs)
    v = proj[..., K_END:V_END].reshape(b, l, N_KV_HEADS, HEAD_DIM)
    u_s = proj[..., V_END:S_END]
    u_f = proj[..., S_END:]
    proj_c = hc @ (w_in if need_ctx else w_in[:, Q_END:S_END])
    off = 0 if need_ctx else Q_END
    kc = rms_norm(proj_c[..., Q_END - off:K_END - off].reshape(b, n_ctx, N_KV_HEADS, HEAD_DIM), k_norm)
    vc = proj_c[..., K_END - off:V_END - off].reshape(b, n_ctx, N_KV_HEADS, HEAD_DIM)
    uc_s = proj_c[..., V_END - off:S_END - off]

    attn = latent_attention(q, jnp.concatenate([k, kc], axis=1), jnp.concatenate([v, vc], axis=1))
    ys, ysc = s5_bidirectional(u_s, uc_s, lam_re, lam_im, log_dt, b_re, b_im, c_re, c_im, ssm_d, need_ctx)
    ssm_out = s5_glu(ys, glu_w, glu_b)
    four = fourier_mix(u_f) @ fourier_w
    y = jnp.concatenate([attn, ssm_out, four], axis=-1) @ w_out
    if not need_ctx:
        return y, None
    qc = rms_norm(proj_c[..., :Q_END].reshape(b, n_ctx, N_Q_HEADS, HEAD_DIM), q_norm)
    qc = qc.reshape(b, n_ctx, N_KV_HEADS, Q_PER_KV, HEAD_DIM)
    attn_c = attend(qc, kc, vc).reshape(b, n_ctx, ATTN_WIDTH)
    four_c = fourier_mix(proj_c[..., S_END:]) @ fourier_w
    yc = jnp.concatenate([attn_c, s5_glu(ysc, glu_w, glu_b), four_c], axis=-1) @ w_out
    return y, yc


def setup_inputs(seed: int = 0) -> dict:
    key = jax.random.key(seed)
    ks = jax.random.split(key, 28)
    f32 = jnp.float32

    def nrm(k, shape, scale):
        return jax.random.normal(k, shape, f32) * scale

    g_p = (DEPTH, 2, N_SSM_GROUPS, SSM_STATE)
    n_idx = jnp.arange(SSM_STATE, dtype=f32)
    return {
        'x': nrm(ks[0], (BATCH, SEQ, D_MODEL), 1.0),
        'c': nrm(ks[1], (BATCH, D_MODEL), 1.0),
        'ctx': nrm(ks[2], (BATCH, CTX_LEN, D_MODEL), 1.0),
        'c_ctx': nrm(ks[3], (D_MODEL,), 1.0),
        'ada_w': nrm(ks[4], (DEPTH, D_MODEL, 6 * D_MODEL), 0.5 * D_MODEL ** -0.5),
        'ada_b': nrm(ks[5], (DEPTH, 6 * D_MODEL), 0.01),
        'norm_mix_pre': 1 + nrm(ks[6], (DEPTH, D_MODEL), 0.05),
        'norm_mix_post': 1 + nrm(ks[7], (DEPTH, D_MODEL), 0.05),
        'norm_ffn_pre': 1 + nrm(ks[8], (DEPTH, D_MODEL), 0.05),
        'norm_ffn_post': 1 + nrm(ks[9], (DEPTH, D_MODEL), 0.05),
        'w_in': nrm(ks[10], (DEPTH, D_MODEL, IN_WIDTH), D_MODEL ** -0.5),
        'q_norm': 1 + nrm(ks[11], (DEPTH, HEAD_DIM), 0.05),
        'k_norm': 1 + nrm(ks[12], (DEPTH, HEAD_DIM), 0.05),
        'ssm_lam_re': -0.5 + nrm(ks[13], g_p, 0.02),
        'ssm_lam_im': math.pi * n_idx + nrm(ks[14], g_p, 0.02),
        'ssm_log_dt': jax.random.uniform(ks[15], (DEPTH, 2, N_SSM_GROUPS), f32,
                                         math.log(DT_MIN), math.log(DT_MAX)),
        'ssm_b_re': nrm(ks[16], (DEPTH, 2, N_SSM_GROUPS, SSM_STATE, SSM_GROUP), (2 * SSM_GROUP) ** -0.5),
        'ssm_b_im': nrm(ks[17], (DEPTH, 2, N_SSM_GROUPS, SSM_STATE, SSM_GROUP), (2 * SSM_GROUP) ** -0.5),
        'ssm_c_re': nrm(ks[18], (DEPTH, 2, N_SSM_GROUPS, SSM_GROUP, SSM_STATE), (2 * SSM_STATE) ** -0.5),
        'ssm_c_im': nrm(ks[19], (DEPTH, 2, N_SSM_GROUPS, SSM_GROUP, SSM_STATE), (2 * SSM_STATE) ** -0.5),
        'ssm_d': nrm(ks[20], (DEPTH, SSM_WIDTH), 1.0),
        'ssm_glu_w': nrm(ks[21], (DEPTH, SSM_WIDTH, SSM_WIDTH), SSM_WIDTH ** -0.5),
        'ssm_glu_b': nrm(ks[22], (DEPTH, SSM_WIDTH), 0.01),
        'fourier_w': nrm(ks[23], (DEPTH, FOURIER_WIDTH, FOURIER_WIDTH), FOURIER_WIDTH ** -0.5),
        'w_out': nrm(ks[24], (DEPTH, D_MODEL, D_MODEL), D_MODEL ** -0.5),
        'ffn_w_gate': nrm(ks[25], (DEPTH, D_MODEL, FFN_HIDDEN), D_MODEL ** -0.5),
        'ffn_w_up': nrm(ks[26], (DEPTH, D_MODEL, FFN_HIDDEN), D_MODEL ** -0.5),
        'ffn_w_down': nrm(ks[27], (DEPTH, FFN_HIDDEN, D_MODEL), FFN_HIDDEN ** -0.5),
    }


def reference(x, c, ctx, c_ctx, ada_w, ada_b, norm_mix_pre, norm_mix_post, norm_ffn_pre,
              norm_ffn_post, w_in, q_norm, k_norm, ssm_lam_re, ssm_lam_im, ssm_log_dt,
              ssm_b_re, ssm_b_im, ssm_c_re, ssm_c_im, ssm_d, ssm_glu_w, ssm_glu_b,
              fourier_w, w_out, ffn_w_gate, ffn_w_up, ffn_w_down):
    n_tok = x.shape[1]
    ROWS = n_tok // GRID_W
    tabs = axial_rope_tables(ROWS)
    xc = ctx
    for layer in range(DEPTH):
        need_ctx = layer < DEPTH - 1
        mod = jax.nn.silu(c) @ ada_w[layer] + ada_b[layer]
        sh_m, sc_m, g_m, sh_f, sc_f, g_f = jnp.split(mod[:, None, :], 6, axis=-1)
        mod_c = jax.nn.silu(c_ctx) @ ada_w[layer] + ada_b[layer]
        shc_m, scc_m, gc_m, shc_f, scc_f, gc_f = jnp.split(mod_c, 6, axis=-1)

        h = modulate(rms_norm(x, norm_mix_pre[layer]), sh_m, sc_m)
        hc = modulate(rms_norm(xc, norm_mix_pre[layer]), shc_m, scc_m)
        y, yc = mixer(h, hc, tabs, w_in[layer], q_norm[layer], k_norm[layer],
                      ssm_lam_re[layer], ssm_lam_im[layer], ssm_log_dt[layer],
                      ssm_b_re[layer], ssm_b_im[layer], ssm_c_re[layer], ssm_c_im[layer],
                      ssm_d[layer], ssm_glu_w[layer], ssm_glu_b[layer], fourier_w[layer],
                      w_out[layer], need_ctx)
        x = x + g_m * rms_norm(y, norm_mix_post[layer])
        h = modulate(rms_norm(x, norm_ffn_pre[layer]), sh_f, sc_f)
        x = x + g_f * rms_norm(swiglu(h, ffn_w_gate[layer], ffn_w_up[layer], ffn_w_down[layer]),
                               norm_ffn_post[layer])
        if need_ctx:
            xc = xc + gc_m * rms_norm(yc, norm_mix_post[layer])
            hc = modulate(rms_norm(xc, norm_ffn_pre[layer]), shc_f, scc_f)
            xc = xc + gc_f * rms_norm(swiglu(hc, ffn_w_gate[layer], ffn_w_up[layer], ffn_w_down[layer]),
                                      norm_ffn_post[layer])
    return x
```

```python
import functools
import math

import numpy as np
import jax
import jax.numpy as jnp
from jax import lax
from jax.experimental import pallas as pl
from jax.experimental.pallas import tpu as pltpu

F32 = jnp.float32
BF16 = jnp.bfloat16

HEAD_DIM = 128
N_KV_HEADS = 2
Q_PER_KV = 4
GRID_W = 64
ROPE_THETA = 10000.0
ROPE_PAIRS = HEAD_DIM // 4
SSM_GROUP = 16
SSM_STATE = 64
FOURIER_HEAD_DIM = 128
NORM_EPS = 1e-6
SSM_CHUNK = 16
LOG2E = 1.4426950408889634

VMEM_LIMIT_MB = 56


def _cparams(semantics, vmem_mb=VMEM_LIMIT_MB):
    return pltpu.CompilerParams(dimension_semantics=semantics,
                                vmem_limit_bytes=vmem_mb * 2 ** 20)


def _rms(x, gain):
    return x * lax.rsqrt(jnp.mean(x * x, axis=-1, keepdims=True) + NORM_EPS) * gain


def _ada_kernel(s_ref, w_ref, b_ref, o_ref):
    s = s_ref[...]
    act = (s * jax.nn.sigmoid(s)).astype(BF16)
    o_ref[0] = jnp.dot(act, w_ref[0].astype(BF16), preferred_element_type=F32) + b_ref[0]


def _ada_call(cond, ada_w, ada_b):
    depth, d, n = ada_w.shape
    tn = 1024
    return pl.pallas_call(
        _ada_kernel,
        grid=(depth, n // tn),
        in_specs=[pl.BlockSpec((8, d), lambda l, j: (0, 0)),
                  pl.BlockSpec((1, d, tn), lambda l, j: (l, 0, j)),
                  pl.BlockSpec((1, 1, tn), lambda l, j: (l, 0, j))],
        out_specs=pl.BlockSpec((1, 8, tn), lambda l, j: (l, 0, j)),
        out_shape=jax.ShapeDtypeStruct((depth, 8, n), F32),
        compiler_params=_cparams(("arbitrary", "arbitrary")),
        name="ada_ln",
    )(cond, ada_w, ada_b.reshape(depth, 1, n))


def _inproj_kernel(rope, x_ref, g_ref, sh_ref, sc_ref, w_ref, qn_ref, kn_ref, cos_ref, sin_ref,
                   cs_ref, q_ref, k_ref, vt_ref, us_ref, ab_ref):
    x = x_ref[...]
    h = (_rms(x, g_ref[...]) * (1.0 + sc_ref[...]) + sh_ref[...]).astype(BF16)
    n_q = q_ref.shape[1] // HEAD_DIM
    n_k = k_ref.shape[1] // HEAD_DIM
    q_end = n_q * HEAD_DIM
    k_end = q_end + n_k * HEAD_DIM
    v_end = k_end + n_k * HEAD_DIM
    s_end = v_end + us_ref.shape[1]
    f_end = s_end + ab_ref.shape[2]

    if rope:
        cos = cos_ref[...]
        sin = sin_ref[...]
        lane = lax.broadcasted_iota(jnp.int32, cos.shape, 1)
        low = (lane % 64) < 32

    def head(t, gain, scale):
        t = _rms(t, gain)
        if rope:
            sw = jnp.where(low, pltpu.roll(t, HEAD_DIM - 32, 1), pltpu.roll(t, 32, 1))
            t = t * cos + sw * sin
        if scale != 1.0:
            t = t * scale
        return t.astype(BF16)

    qk = jnp.dot(h, w_ref[:, 0:k_end], preferred_element_type=F32)
    q_scale = HEAD_DIM ** -0.5 * LOG2E
    for i in range(n_q):
        q_ref[:, i * HEAD_DIM:(i + 1) * HEAD_DIM] = head(
            qk[:, i * HEAD_DIM:(i + 1) * HEAD_DIM], qn_ref[...], q_scale)
    for i in range(n_k):
        k_ref[:, i * HEAD_DIM:(i + 1) * HEAD_DIM] = head(
            qk[:, q_end + i * HEAD_DIM:q_end + (i + 1) * HEAD_DIM], kn_ref[...], 1.0)
    rest = jnp.dot(h, w_ref[:, k_end:f_end], preferred_element_type=F32)
    vt_ref[...] = rest[:, 0:v_end - k_end].T.astype(BF16)
    us_ref[...] = rest[:, v_end - k_end:s_end - k_end].astype(BF16)
    uf = rest[:, s_end - k_end:f_end - k_end].astype(BF16)
    ab = jnp.dot(uf, cs_ref[...], preferred_element_type=F32)
    fw = ab_ref.shape[2]
    ab_ref[0] = ab[:, 0:fw].astype(BF16)
    ab_ref[1] = ab[:, fw:2 * fw].astype(BF16)


def _inproj_call(x, gain, shift, scale, w_in, qn, kn, cos, sin, cs, rope):
    l, d = x.shape
    tm = min(512, l)
    qw = Q_PER_KV * N_KV_HEADS * HEAD_DIM
    kw = N_KV_HEADS * HEAD_DIM
    fw = cs.shape[0]
    sw = w_in.shape[1] - qw - 2 * kw - fw
    row = lambda i: (i, 0)
    fix = lambda i: (0, 0)
    return pl.pallas_call(
        functools.partial(_inproj_kernel, rope),
        grid=(l // tm,),
        in_specs=[pl.BlockSpec((tm, d), row),
                  pl.BlockSpec((1, d), fix), pl.BlockSpec((1, d), fix), pl.BlockSpec((1, d), fix),
                  pl.BlockSpec(w_in.shape, fix),
                  pl.BlockSpec((1, HEAD_DIM), fix), pl.BlockSpec((1, HEAD_DIM), fix),
                  pl.BlockSpec((tm, HEAD_DIM), row), pl.BlockSpec((tm, HEAD_DIM), row),
                  pl.BlockSpec(cs.shape, fix)],
        out_specs=[pl.BlockSpec((tm, qw), row),
                   pl.BlockSpec((tm, kw), row),
                   pl.BlockSpec((kw, tm), lambda i: (0, i)),
                   pl.BlockSpec((tm, sw), row),
                   pl.BlockSpec((2, tm, fw), lambda i: (0, i, 0))],
        out_shape=[jax.ShapeDtypeStruct((l, qw), BF16),
                   jax.ShapeDtypeStruct((l, kw), BF16),
                   jax.ShapeDtypeStruct((kw, l), BF16),
                   jax.ShapeDtypeStruct((l, sw), BF16),
                   jax.ShapeDtypeStruct((2, l, fw), BF16)],
        compiler_params=_cparams(("arbitrary",)),
        name="in_proj",
    )(x, gain, shift, scale, w_in, qn, kn, cos, sin, cs)


def _attn_kernel(has_ctx, nk, tq, *refs):
    if has_ctx:
        q_ref, k_ref, vt_ref, kc_ref, vtc_ref, o_ref, qs_ref, m_ref, l_ref, acc_ref = refs
    else:
        q_ref, k_ref, vt_ref, o_ref, qs_ref, m_ref, l_ref, acc_ref = refs
    j = pl.program_id(2)

    @pl.when(j == 0)
    def _init():
        for h in range(Q_PER_KV):
            qs_ref[h * tq:(h + 1) * tq, :] = q_ref[:, h * HEAD_DIM:(h + 1) * HEAD_DIM]
        m_ref[...] = jnp.full(m_ref.shape, -jnp.inf, F32)
        l_ref[...] = jnp.zeros(l_ref.shape, F32)
        acc_ref[...] = jnp.zeros(acc_ref.shape, F32)

    def step(kb, vtb):
        s = lax.dot_general(kb, qs_ref[...], (((1,), (1,)), ((), ())),
                            preferred_element_type=F32)
        m_prev = m_ref[...]
        m_new = jnp.maximum(m_prev, jnp.max(s, axis=0, keepdims=True))
        alpha = jnp.exp2(m_prev - m_new)
        p = jnp.exp2(s - m_new)
        l_ref[...] = alpha * l_ref[...] + jnp.sum(p, axis=0, keepdims=True)
        acc_ref[...] = alpha * acc_ref[...] + jnp.dot(vtb, p.astype(BF16),
                                                      preferred_element_type=F32)
        m_ref[...] = m_new

    step(k_ref[...], vt_ref[...])

    @pl.when(j == nk - 1)
    def _finish():
        if has_ctx:
            step(kc_ref[...], vtc_ref[...])
        o = acc_ref[...] / l_ref[...]
        for h in range(Q_PER_KV):
            o_ref[:, h * HEAD_DIM:(h + 1) * HEAD_DIM] = o[:, h * tq:(h + 1) * tq].T.astype(BF16)


def _attn_call(q, k, vt, kc=None, vtc=None):
    l = q.shape[0]
    tq = min(256, l)
    tk = min(512, l)
    nk = l // tk
    gw = Q_PER_KV * HEAD_DIM
    n = Q_PER_KV * tq
    has_ctx = kc is not None
    in_specs = [pl.BlockSpec((tq, gw), lambda g, i, j: (i, g)),
                pl.BlockSpec((tk, HEAD_DIM), lambda g, i, j: (j, g)),
                pl.BlockSpec((HEAD_DIM, tk), lambda g, i, j: (g, j))]
    args = [q, k, vt]
    if has_ctx:
        nc = kc.shape[0]
        in_specs += [pl.BlockSpec((nc, HEAD_DIM), lambda g, i, j: (0, g)),
                     pl.BlockSpec((HEAD_DIM, nc), lambda g, i, j: (g, 0))]
        args += [kc, vtc]
    return pl.pallas_call(
        functools.partial(_attn_kernel, has_ctx, nk, tq),
        grid=(N_KV_HEADS, l // tq, nk),
        in_specs=in_specs,
        out_specs=pl.BlockSpec((tq, gw), lambda g, i, j: (i, g)),
        out_shape=jax.ShapeDtypeStruct(q.shape, BF16),
        scratch_shapes=[pltpu.VMEM((n, HEAD_DIM), BF16),
                        pltpu.VMEM((1, n), F32),
                        pltpu.VMEM((1, n), F32),
                        pltpu.VMEM((HEAD_DIM, n), F32)],
        compiler_params=_cparams(("arbitrary", "arbitrary", "arbitrary")),
        name="attention",
    )(*args)


def _ssm_tables(lam_re, lam_im, log_dt, b_re, b_im, c_re, c_im, d):
    t = SSM_CHUNK
    g, p = lam_re.shape[1:]
    hdim = b_re.shape[-1]
    lr = lam_re.astype(F32)
    li = lam_im.astype(F32)
    dt = jnp.exp(log_dt.astype(F32))[..., None]
    mag = jnp.exp(lr * dt)
    ar = mag * jnp.cos(li * dt)
    ai = mag * jnp.sin(li * dt)
    den = lr * lr + li * li
    cr = ((ar - 1.0) * lr + ai * li) / den
    ci = (ai * lr - (ar - 1.0) * li) / den
    br = b_re.astype(F32)
    bi = b_im.astype(F32)
    bbr = cr[..., None] * br - ci[..., None] * bi
    bbi = cr[..., None] * bi + ci[..., None] * br
    n = jnp.arange(t + 1, dtype=F32)[:, None, None, None]
    pmag = jnp.exp(n * (lr * dt)[None])
    pr = pmag * jnp.cos(n * (li * dt)[None])
    pi = pmag * jnp.sin(n * (li * dt)[None])
    cre = c_re.astype(F32)
    cim = c_im.astype(F32)

    def summ(direction, powers):
        er = pr[powers, direction][..., None] * bbr[direction][None] \
            - pi[powers, direction][..., None] * bbi[direction][None]
        ei = pr[powers, direction][..., None] * bbi[direction][None] \
            + pi[powers, direction][..., None] * bbr[direction][None]
        tr = lambda e: jnp.transpose(e, (1, 0, 3, 2)).reshape(g, t * hdim, p)
        return tr(er), tr(ei)
    steps = jnp.arange(t)
    pfr, pfi = summ(0, t - 1 - steps)
    pbr, pbi = summ(1, steps)
    p_mat = jnp.concatenate([pfr, pbr, pfi, pbi], axis=-1)

    a_vec = jnp.concatenate([pr[t, 0], pr[t, 1], pi[t, 0], pi[t, 1]], axis=-1)

    def kern(direction):
        er = pr[:t, direction][..., None] * bbr[direction][None] \
            - pi[:t, direction][..., None] * bbi[direction][None]
        ei = pr[:t, direction][..., None] * bbi[direction][None] \
            + pi[:t, direction][..., None] * bbr[direction][None]
        return (jnp.einsum('gop,ngpi->ngoi', cre[direction], er)
                - jnp.einsum('gop,ngpi->ngoi', cim[direction], ei))
    kf = kern(0)
    kb = kern(1)
    s_idx = steps[:, None]
    t_idx = steps[None, :]
    lag_f = jnp.clip(t_idx - s_idx, 0, t - 1)
    lag_b = jnp.clip(s_idx - t_idx, 0, t - 1)
    mf = jnp.where((t_idx >= s_idx)[:, :, None, None, None], kf[lag_f], 0.0)
    mb = jnp.where((s_idx >= t_idx)[:, :, None, None, None], kb[lag_b], 0.0)
    dd = d.astype(F32).reshape(g, hdim)
    eye_t = jnp.eye(t, dtype=F32)[:, :, None, None, None]
    eye_c = jnp.eye(hdim, dtype=F32)[None, None, None]
    m5 = mf + mb + eye_t * eye_c * dd[None, None, :, :, None]
    m_mat = jnp.transpose(m5, (2, 0, 4, 1, 3)).reshape(g, t * hdim, t * hdim)

    def state_out(direction, powers):
        prn = pr[powers, direction]
        pin = pi[powers, direction]
        wr = cre[direction][None] * prn[:, :, None, :] - cim[direction][None] * pin[:, :, None, :]
        wi = -cre[direction][None] * pin[:, :, None, :] - cim[direction][None] * prn[:, :, None, :]
        tr = lambda w: jnp.transpose(w, (1, 3, 0, 2)).reshape(g, p, t * hdim)
        return tr(wr), tr(wi)
    qfr, qfi = state_out(0, steps + 1)
    qbr, qbi = state_out(1, t - steps)
    z = jnp.zeros_like(qfr)
    qf = jnp.concatenate([qfr, z, qfi, z], axis=1)
    qb = jnp.concatenate([z, qbr, z, qbi], axis=1)
    w_out = jnp.concatenate([m_mat, qf, qb], axis=1)
    return p_mat.astype(BF16), a_vec, w_out.astype(BF16)


def _ssm_sum_kernel(x_ref, p_ref, s_ref):
    s_ref[0] = jnp.dot(x_ref[0], p_ref[0], preferred_element_type=F32)


def _ssm_sum_call(xg, p_mat):
    g, n, k = xg.shape
    sw = p_mat.shape[2]
    return pl.pallas_call(
        _ssm_sum_kernel,
        grid=(g,),
        in_specs=[pl.BlockSpec((1, n, k), lambda i: (i, 0, 0)),
                  pl.BlockSpec((1, k, sw), lambda i: (i, 0, 0))],
        out_specs=pl.BlockSpec((1, n, sw), lambda i: (i, 0, 0)),
        out_shape=jax.ShapeDtypeStruct((g, n, sw), F32),
        compiler_params=_cparams(("arbitrary",)),
        name="ssm_summaries",
    )(xg, p_mat)


def _ssm_scan_kernel(kb, sf_ref, sb_ref, a_ref, h0_ref, hf_ref, hb_ref, hend_ref, h_ref):
    i = pl.program_id(0)

    @pl.when(i == 0)
    def _init():
        h_ref[...] = h0_ref[...]

    half = a_ref.shape[1] // 2
    a_r = a_ref[:, 0:half]
    a_i = a_ref[:, half:2 * half]
    lane = lax.broadcasted_iota(jnp.int32, h_ref.shape, 1)
    fwd = (lane % half) < (half // 2)

    def body(kk, h):
        hf_ref[kk] = h.astype(hf_ref.dtype)
        hb_ref[kb - 1 - kk] = h.astype(hb_ref.dtype)
        s = jnp.where(fwd, sf_ref[kk], sb_ref[kb - 1 - kk])
        hr = h[:, 0:half]
        hi = h[:, half:2 * half]
        nr = a_r * hr - a_i * hi + s[:, 0:half]
        ni = a_r * hi + a_i * hr + s[:, half:2 * half]
        return jnp.concatenate([nr, ni], axis=1)

    h = lax.fori_loop(0, kb, body, h_ref[...])
    h_ref[...] = h
    hend_ref[...] = h


def _ssm_scan_call(s_t, a_vec, h0):
    n, g, sw = s_t.shape
    kb = min(128, n)
    nb = n // kb
    return pl.pallas_call(
        functools.partial(_ssm_scan_kernel, kb),
        grid=(nb,),
        in_specs=[pl.BlockSpec((kb, g, sw), lambda i: (i, 0, 0)),
                  pl.BlockSpec((kb, g, sw), lambda i: (nb - 1 - i, 0, 0)),
                  pl.BlockSpec((g, sw), lambda i: (0, 0)),
                  pl.BlockSpec((g, sw), lambda i: (0, 0))],
        out_specs=[pl.BlockSpec((kb, g, sw), lambda i: (i, 0, 0)),
                   pl.BlockSpec((kb, g, sw), lambda i: (nb - 1 - i, 0, 0)),
                   pl.BlockSpec((g, sw), lambda i: (0, 0))],
        out_shape=[jax.ShapeDtypeStruct((n, g, sw), BF16),
                   jax.ShapeDtypeStruct((n, g, sw), BF16),
                   jax.ShapeDtypeStruct((g, sw), F32)],
        scratch_shapes=[pltpu.VMEM((g, sw), F32)],
        compiler_params=_cparams(("arbitrary",)),
        name="ssm_chunk_scan",
    )(s_t, s_t, a_vec, h0)


def _ssm_out_kernel(x_ref, hf_ref, hb_ref, w_ref, y_ref):
    k = x_ref.shape[2]
    sw = hf_ref.shape[2]
    y = jnp.dot(x_ref[0], w_ref[0, 0:k, :], preferred_element_type=F32)
    y = y + jnp.dot(hf_ref[0], w_ref[0, k:k + sw, :], preferred_element_type=F32)
    y = y + jnp.dot(hb_ref[0], w_ref[0, k + sw:k + 2 * sw, :], preferred_element_type=F32)
    y_ref[0] = y.astype(y_ref.dtype)


def _ssm_out_call(xg, hf, hb, w_out):
    g, n, k = xg.shape
    sw = hf.shape[2]
    blk = lambda w: pl.BlockSpec((1, n, w), lambda i: (i, 0, 0))
    return pl.pallas_call(
        _ssm_out_kernel,
        grid=(g,),
        in_specs=[blk(k), blk(sw), blk(sw),
                  pl.BlockSpec((1,) + w_out.shape[1:], lambda i: (i, 0, 0))],
        out_specs=blk(k),
        out_shape=jax.ShapeDtypeStruct((g, n, k), BF16),
        compiler_params=_cparams(("arbitrary",)),
        name="ssm_outputs",
    )(xg, hf, hb, w_out)


def _ssm_apply(us, tables, h0, need_y):
    p_mat, a_vec, w_out = tables
    l, w = us.shape
    g = w // SSM_GROUP
    n = l // SSM_CHUNK
    xg = jnp.transpose(us.reshape(n, SSM_CHUNK, g, SSM_GROUP), (2, 0, 1, 3)).reshape(
        g, n, SSM_CHUNK * SSM_GROUP)
    s = _ssm_sum_call(xg, p_mat)
    hf, hb, hend = _ssm_scan_call(jnp.transpose(s, (1, 0, 2)), a_vec, h0)
    if not need_y:
        return None, hend
    yg = _ssm_out_call(xg, jnp.transpose(hf, (1, 0, 2)), jnp.transpose(hb, (1, 0, 2)), w_out)
    y = jnp.transpose(yg.reshape(g, n, SSM_CHUNK, SSM_GROUP), (1, 2, 0, 3)).reshape(l, w)
    return y, hend


def _dft_tables(l, n1, hw):
    n2 = l // n1
    scale = 1.0 / math.sqrt(l * hw)
    if n1 == 1:
        t1 = None
    else:
        ang = 2.0 * np.pi * ((np.arange(n1)[:, None] * np.arange(n1)[None, :]) % n1) / n1
        c, s = np.cos(ang), np.sin(ang)
        t1 = jnp.asarray(np.block([[c, -s], [-s, -c]]), dtype=BF16)
    k1 = np.arange(n1)[:, None, None]
    k2 = np.arange(n2)[None, :, None]
    m = np.arange(n2)[None, None, :]
    ang = 2.0 * np.pi * ((m * (n1 * k2 + k1)) % l) / l
    sign = -1.0 if n1 == 1 else 1.0
    g = np.concatenate([np.cos(ang), sign * np.sin(ang)], axis=-1) * scale
    return t1, jnp.asarray(g, dtype=BF16)


def _dft1_kernel(t_ref, x_ref, z_ref):
    z_ref[...] = jnp.dot(t_ref[...], x_ref[...], preferred_element_type=F32).astype(z_ref.dtype)


def _dft1_call(t1, ab2d):
    r, n = ab2d.shape
    tn = min(4096, n)
    return pl.pallas_call(
        _dft1_kernel,
        grid=(n // tn,),
        in_specs=[pl.BlockSpec((r, r), lambda j: (0, 0)),
                  pl.BlockSpec((r, tn), lambda j: (0, j))],
        out_specs=pl.BlockSpec((r, tn), lambda j: (0, j)),
        out_shape=jax.ShapeDtypeStruct((r, n), BF16),
        compiler_params=_cparams(("arbitrary",)),
        name="dft_stage1",
    )(t1, ab2d)


def _dft2_kernel(bt, zr_ref, zi_ref, g_ref, w_ref, o_ref):
    fw = w_ref.shape[0]
    for b in range(bt):
        z = jnp.concatenate([zr_ref[b], zi_ref[b]], axis=0)
        x = jnp.dot(g_ref[b], z, preferred_element_type=F32)
        o_ref[:, b * fw:(b + 1) * fw] = jnp.dot(
            x.astype(BF16), w_ref[...], preferred_element_type=F32).astype(o_ref.dtype)


def _dft2_call(z3, g, fourier_w):
    n1x2, n2, fw = z3.shape
    n1 = n1x2 // 2
    bt = min(8, n1)
    return pl.pallas_call(
        functools.partial(_dft2_kernel, bt),
        grid=(n1 // bt,),
        in_specs=[pl.BlockSpec((bt, n2, fw), lambda b: (b, 0, 0)),
                  pl.BlockSpec((bt, n2, fw), lambda b: (b + n1 // bt, 0, 0)),
                  pl.BlockSpec((bt, n2, 2 * n2), lambda b: (b, 0, 0)),
                  pl.BlockSpec(fourier_w.shape, lambda b: (0, 0))],
        out_specs=pl.BlockSpec((n2, bt * fw), lambda b: (0, b)),
        out_shape=jax.ShapeDtypeStruct((n2, n1 * fw), BF16),
        compiler_params=_cparams(("arbitrary",)),
        name="dft_stage2",
    )(z3, z3, g, fourier_w)


def _fourier_apply(ab, tables, fourier_w, n1):
    t1, g = tables
    _, l, fw = ab.shape
    n2 = l // n1
    if n1 == 1:
        z3 = ab
    else:
        z3 = _dft1_call(t1, ab.reshape(2 * n1, n2 * fw)).reshape(2 * n1, n2, fw)
    return _dft2_call(z3, g, fourier_w).reshape(l, fw)


def _outproj_kernel(attn_ref, ys_ref, four_ref, x_ref, wo_ref, gw_ref, gb_ref, npost_ref,
                    gate_ref, npre_ref, sh_ref, sc_ref, xo_ref, h_ref):
    ys = ys_ref[...].astype(F32)
    gl = 0.5 * ys * (1.0 + jnp.tanh(math.sqrt(2.0 / math.pi) * (ys + 0.044715 * (ys * ys * ys))))
    z = jnp.dot(gl.astype(BF16), gw_ref[...], preferred_element_type=F32) + gb_ref[...]
    ssm = (gl * jax.nn.sigmoid(z)).astype(BF16)
    cat = jnp.concatenate([attn_ref[...], ssm, four_ref[...]], axis=-1)
    y = jnp.dot(cat, wo_ref[...], preferred_element_type=F32)
    xn = x_ref[...] + gate_ref[...] * _rms(y, npost_ref[...])
    xo_ref[...] = xn
    h_ref[...] = (_rms(xn, npre_ref[...]) * (1.0 + sc_ref[...]) + sh_ref[...]).astype(BF16)


def _outproj_call(attn, ys, four, x, w_out, glu_w, glu_b, npost, gate, npre, shift, scale):
    l, d = x.shape
    tm = min(512, l)
    row = lambda i: (i, 0)
    fix = lambda i: (0, 0)
    vec = pl.BlockSpec((1, d), fix)
    return pl.pallas_call(
        _outproj_kernel,
        grid=(l // tm,),
        in_specs=[pl.BlockSpec((tm, attn.shape[1]), row),
                  pl.BlockSpec((tm, ys.shape[1]), row),
                  pl.BlockSpec((tm, four.shape[1]), row),
                  pl.BlockSpec((tm, d), row),
                  pl.BlockSpec(w_out.shape, fix),
                  pl.BlockSpec(glu_w.shape, fix),
                  pl.BlockSpec((1, glu_w.shape[1]), fix),
                  vec, vec, vec, vec, vec],
        out_specs=[pl.BlockSpec((tm, d), row), pl.BlockSpec((tm, d), row)],
        out_shape=[jax.ShapeDtypeStruct((l, d), F32), jax.ShapeDtypeStruct((l, d), BF16)],
        compiler_params=_cparams(("arbitrary",)),
        name="out_proj",
    )(attn, ys, four, x, w_out, glu_w, glu_b, npost, gate, npre, shift, scale)


def _ffn_kernel(nj, h_ref, x_ref, wg_ref, wu_ref, wd_ref, npost_ref, gate_ref, o_ref, acc_ref):
    j = pl.program_id(1)
    h = h_ref[...]
    a = jnp.dot(h, wg_ref[...], preferred_element_type=F32)
    u = jnp.dot(h, wu_ref[...], preferred_element_type=F32)
    act = (a * jax.nn.sigmoid(a) * u).astype(BF16)
    part = jnp.dot(act, wd_ref[...], preferred_element_type=F32)

    @pl.when(j == 0)
    def _first():
        acc_ref[...] = part

    @pl.when(j > 0)
    def _rest():
        acc_ref[...] += part

    @pl.when(j == nj - 1)
    def _finish():
        o_ref[...] = x_ref[...] + gate_ref[...] * _rms(acc_ref[...], npost_ref[...])


def _ffn_call(h, x, w_gate, w_up, w_down, npost, gate):
    l, d = x.shape
    fh = w_gate.shape[1]
    tm = min(512, l)
    th = 512
    nj = fh // th
    vec = pl.BlockSpec((1, d), lambda i, j: (0, 0))
    return pl.pallas_call(
        functools.partial(_ffn_kernel, nj),
        grid=(l // tm, nj),
        in_specs=[pl.BlockSpec((tm, d), lambda i, j: (i, 0)),
                  pl.BlockSpec((tm, d), lambda i, j: (i, 0)),
                  pl.BlockSpec((d, th), lambda i, j: (0, j)),
                  pl.BlockSpec((d, th), lambda i, j: (0, j)),
                  pl.BlockSpec((th, d), lambda i, j: (j, 0)),
                  vec, vec],
        out_specs=pl.BlockSpec((tm, d), lambda i, j: (i, 0)),
        out_shape=jax.ShapeDtypeStruct((l, d), F32),
        scratch_shapes=[pltpu.VMEM((tm, d), F32)],
        compiler_params=_cparams(("arbitrary", "arbitrary")),
        name="ffn",
    )(h, x, w_gate, w_up, w_down, npost, gate)


def _rope_tables(l):
    t = jnp.arange(l, dtype=jnp.int32)
    row = (t // GRID_W).astype(F32)
    col = (t % GRID_W).astype(F32)
    freqs = ROPE_THETA ** (-jnp.arange(ROPE_PAIRS, dtype=F32) / ROPE_PAIRS)
    ang_r = row[:, None] * freqs
    ang_c = col[:, None] * freqs
    cos = jnp.concatenate([jnp.cos(ang_r)] * 2 + [jnp.cos(ang_c)] * 2, axis=-1)
    sin = jnp.concatenate([-jnp.sin(ang_r), jnp.sin(ang_r), -jnp.sin(ang_c), jnp.sin(ang_c)], axis=-1)
    return cos, sin


def _channel_dft_table(fw):
    hw = FOURIER_HEAD_DIM
    ang = 2.0 * np.pi * ((np.arange(hw)[:, None] * np.arange(hw)[None, :]) % hw) / hw
    eye = np.eye(fw // hw)
    return jnp.asarray(np.concatenate([np.kron(eye, np.cos(ang)), np.kron(eye, np.sin(ang))], axis=1),
                       dtype=BF16)


def _dft_split(l):
    n1 = 1
    while n1 * n1 < l:
        n1 *= 2
    return n1 if (l >= 1024 and n1 * n1 == l) else 1


def kernel(x, c, ctx, c_ctx, ada_w, ada_b, norm_mix_pre, norm_mix_post, norm_ffn_pre, norm_ffn_post, w_in, q_norm, k_norm, ssm_lam_re, ssm_lam_im, ssm_log_dt, ssm_b_re, ssm_b_im, ssm_c_re, ssm_c_im, ssm_d, ssm_glu_w, ssm_glu_b, fourier_w, w_out, ffn_w_gate, ffn_w_up, ffn_w_down):
    depth = ada_w.shape[0]
    _, l, d = x.shape
    n_ctx = ctx.shape[1]
    fw = fourier_w.shape[1]
    sw = ssm_d.shape[1]
    n_groups = sw // SSM_GROUP

    cond = jnp.zeros((8, d), F32).at[0].set(c[0]).at[1].set(c_ctx)
    mod = _ada_call(cond, ada_w, ada_b)

    cos, sin = _rope_tables(l)
    zero_tab = jnp.zeros((n_ctx, HEAD_DIM), F32)
    cs = _channel_dft_table(fw)
    n1 = _dft_split(l)
    n1c = _dft_split(n_ctx)
    dft_x = _dft_tables(l, n1, FOURIER_HEAD_DIM)
    dft_c = _dft_tables(n_ctx, n1c, FOURIER_HEAD_DIM)
    h0 = jnp.zeros((n_groups, 4 * SSM_STATE), F32)

    xs = x[0]
    xc = ctx[0]
    for layer in range(depth):
        need_ctx = layer < depth - 1
        vecs = lambda r: [mod[layer, r:r + 1, i * d:(i + 1) * d] for i in range(6)]
        sh_m, sc_m, g_m, sh_f, sc_f, g_f = vecs(0)
        shc_m, scc_m, gc_m, shc_f, scc_f, gc_f = vecs(1)
        row = lambda a: a[layer].reshape(1, -1)

        w_in_l = w_in[layer].astype(BF16)
        w_out_l = w_out[layer].astype(BF16)
        glu_w_l = ssm_glu_w[layer].astype(BF16)
        four_w_l = fourier_w[layer].astype(BF16)
        wg_l = ffn_w_gate[layer].astype(BF16)
        wu_l = ffn_w_up[layer].astype(BF16)
        wd_l = ffn_w_down[layer].astype(BF16)
        tables = _ssm_tables(ssm_lam_re[layer], ssm_lam_im[layer], ssm_log_dt[layer],
                             ssm_b_re[layer], ssm_b_im[layer], ssm_c_re[layer], ssm_c_im[layer],
                             ssm_d[layer])

        qc, kc, vtc, usc, abc = _inproj_call(xc, row(norm_mix_pre), shc_m, scc_m, w_in_l,
                                             row(q_norm), row(k_norm), zero_tab, zero_tab, cs, False)
        ysc, hend_c = _ssm_apply(usc, tables, h0, need_ctx)

        q, k, vt, us, ab = _inproj_call(xs, row(norm_mix_pre), sh_m, sc_m, w_in_l,
                                        row(q_norm), row(k_norm), cos, sin, cs, True)
        attn = _attn_call(q, k, vt, kc, vtc)
        ys, _ = _ssm_apply(us, tables, hend_c, True)
        four = _fourier_apply(ab, dft_x, four_w_l, n1)
        xs, hs = _outproj_call(attn, ys, four, xs, w_out_l, glu_w_l, row(ssm_glu_b),
                               row(norm_mix_post), g_m, row(norm_ffn_pre), sh_f, sc_f)
        xs = _ffn_call(hs, xs, wg_l, wu_l, wd_l, row(norm_ffn_post), g_f)

        if need_ctx:
            attn_c = _attn_call(qc, kc, vtc)
            four_c = _fourier_apply(abc, dft_c, four_w_l, n1c)
            xc, hc = _outproj_call(attn_c, ysc, four_c, xc, w_out_l, glu_w_l, row(ssm_glu_b),
                                   row(norm_mix_post), gc_m, row(norm_ffn_pre), shc_f, scc_f)
            xc = _ffn_call(hc, xc, wg_l, wu_l, wd_l, row(norm_ffn_post), gc_f)
    return xs[None]
```

```python
import functools
import math

import numpy as np
import jax
import jax.numpy as jnp
from jax import lax
from jax.experimental import pallas as pl
from jax.experimental.pallas import tpu as pltpu

F32 = jnp.float32
BF16 = jnp.bfloat16

HEAD_DIM = 128
N_KV_HEADS = 2
Q_PER_KV = 4
GRID_W = 64
ROPE_THETA = 10000.0
ROPE_PAIRS = HEAD_DIM // 4
SSM_GROUP = 16
SSM_STATE = 64
FOURIER_HEAD_DIM = 128
NORM_EPS = 1e-6
SSM_CHUNK = 16
LOG2E = 1.4426950408889634
ATTN_TQ = 512
ATTN_TK = 1280

VMEM_LIMIT_MB = 56


def _cparams(semantics, vmem_mb=VMEM_LIMIT_MB):
    return pltpu.CompilerParams(dimension_semantics=semantics,
                                vmem_limit_bytes=vmem_mb * 2 ** 20)


def _rms(x, gain):
    return x * lax.rsqrt(jnp.mean(x * x, axis=-1, keepdims=True) + NORM_EPS) * gain


def _ada_kernel(s_ref, w_ref, b_ref, o_ref):
    s = s_ref[...]
    act = (s * jax.nn.sigmoid(s)).astype(BF16)
    o_ref[0] = jnp.dot(act, w_ref[0].astype(BF16), preferred_element_type=F32) + b_ref[0]


def _ada_call(cond, ada_w, ada_b):
    depth, d, n = ada_w.shape
    tn = 1024
    return pl.pallas_call(
        _ada_kernel,
        grid=(depth, n // tn),
        in_specs=[pl.BlockSpec((8, d), lambda l, j: (0, 0)),
                  pl.BlockSpec((1, d, tn), lambda l, j: (l, 0, j)),
                  pl.BlockSpec((1, 1, tn), lambda l, j: (l, 0, j))],
        out_specs=pl.BlockSpec((1, 8, tn), lambda l, j: (l, 0, j)),
        out_shape=jax.ShapeDtypeStruct((depth, 8, n), F32),
        compiler_params=_cparams(("arbitrary", "arbitrary")),
        name="ada_ln",
    )(cond, ada_w, ada_b.reshape(depth, 1, n))


def _inproj_kernel(rope, x_ref, g_ref, sh_ref, sc_ref, w_ref, qn_ref, kn_ref, cos_ref, sin_ref,
                   cs_ref, q_ref, k_ref, vt_ref, us_ref, ab_ref):
    x = x_ref[...]
    h = (_rms(x, g_ref[...]) * (1.0 + sc_ref[...]) + sh_ref[...]).astype(BF16)
    n_q = q_ref.shape[1] // HEAD_DIM
    n_k = k_ref.shape[1] // HEAD_DIM
    q_end = n_q * HEAD_DIM
    k_end = q_end + n_k * HEAD_DIM
    v_end = k_end + n_k * HEAD_DIM
    s_end = v_end + us_ref.shape[1]
    f_end = s_end + ab_ref.shape[2]

    if rope:
        cos = cos_ref[...]
        sin = sin_ref[...]
        lane = lax.broadcasted_iota(jnp.int32, cos.shape, 1)
        low = (lane % 64) < 32

    def head(t, gain, scale):
        t = _rms(t, gain)
        if rope:
            sw = jnp.where(low, pltpu.roll(t, HEAD_DIM - 32, 1), pltpu.roll(t, 32, 1))
            t = t * cos + sw * sin
        if scale != 1.0:
            t = t * scale
        return t.astype(BF16)

    qk = jnp.dot(h, w_ref[:, 0:k_end], preferred_element_type=F32)
    q_scale = HEAD_DIM ** -0.5 * LOG2E
    for i in range(n_q):
        q_ref[:, i * HEAD_DIM:(i + 1) * HEAD_DIM] = head(
            qk[:, i * HEAD_DIM:(i + 1) * HEAD_DIM], qn_ref[...], q_scale)
    for i in range(n_k):
        k_ref[:, i * HEAD_DIM:(i + 1) * HEAD_DIM] = head(
            qk[:, q_end + i * HEAD_DIM:q_end + (i + 1) * HEAD_DIM], kn_ref[...], 1.0)
    rest = jnp.dot(h, w_ref[:, k_end:f_end], preferred_element_type=F32)
    vt_ref[...] = rest[:, 0:v_end - k_end].T.astype(BF16)
    us_ref[...] = rest[:, v_end - k_end:s_end - k_end].astype(BF16)
    uf = rest[:, s_end - k_end:f_end - k_end].astype(BF16)
    ab = jnp.dot(uf, cs_ref[...], preferred_element_type=F32)
    fw = ab_ref.shape[2]
    ab_ref[0] = ab[:, 0:fw].astype(BF16)
    ab_ref[1] = ab[:, fw:2 * fw].astype(BF16)


def _inproj_call(x, gain, shift, scale, w_in, qn, kn, cos, sin, cs, rope):
    l, d = x.shape
    tm = min(512, l)
    qw = Q_PER_KV * N_KV_HEADS * HEAD_DIM
    kw = N_KV_HEADS * HEAD_DIM
    fw = cs.shape[0]
    sw = w_in.shape[1] - qw - 2 * kw - fw
    row = lambda i: (i, 0)
    fix = lambda i: (0, 0)
    return pl.pallas_call(
        functools.partial(_inproj_kernel, rope),
        grid=(l // tm,),
        in_specs=[pl.BlockSpec((tm, d), row),
                  pl.BlockSpec((1, d), fix), pl.BlockSpec((1, d), fix), pl.BlockSpec((1, d), fix),
                  pl.BlockSpec(w_in.shape, fix),
                  pl.BlockSpec((1, HEAD_DIM), fix), pl.BlockSpec((1, HEAD_DIM), fix),
                  pl.BlockSpec((tm, HEAD_DIM), row), pl.BlockSpec((tm, HEAD_DIM), row),
                  pl.BlockSpec(cs.shape, fix)],
        out_specs=[pl.BlockSpec((tm, qw), row),
                   pl.BlockSpec((tm, kw), row),
                   pl.BlockSpec((kw, tm), lambda i: (0, i)),
                   pl.BlockSpec((tm, sw), row),
                   pl.BlockSpec((2, tm, fw), lambda i: (0, i, 0))],
        out_shape=[jax.ShapeDtypeStruct((l, qw), BF16),
                   jax.ShapeDtypeStruct((l, kw), BF16),
                   jax.ShapeDtypeStruct((kw, l), BF16),
                   jax.ShapeDtypeStruct((l, sw), BF16),
                   jax.ShapeDtypeStruct((2, l, fw), BF16)],
        compiler_params=_cparams(("arbitrary",)),
        name="in_proj",
    )(x, gain, shift, scale, w_in, qn, kn, cos, sin, cs)


def _attn_kernel(nkb, tq, q_ref, k_ref, vt_ref, o_ref, qs_ref, s0_ref, s1_ref, bm0_ref, bm1_ref,
                 m_ref, l_ref, acc_ref):
    j = pl.program_id(2)
    s_refs = (s0_ref, s1_ref)
    bm_refs = (bm0_ref, bm1_ref)

    def scores(slot):
        s = lax.dot_general(k_ref[...], qs_ref[...], (((1,), (1,)), ((), ())),
                            preferred_element_type=F32)
        s_refs[slot][...] = s
        bm_refs[slot][...] = jnp.max(s, axis=0, keepdims=True)

    def absorb(slot):
        m_prev = m_ref[...]
        m_new = jnp.maximum(m_prev, bm_refs[slot][...])
        alpha = jnp.exp2(m_prev - m_new)
        p = jnp.exp2(s_refs[slot][...] - m_new)
        l_ref[...] = alpha * l_ref[...] + jnp.sum(p, axis=0, keepdims=True)
        acc_ref[...] = alpha * acc_ref[...] + jnp.dot(vt_ref[...], p.astype(BF16),
                                                      preferred_element_type=F32)
        m_ref[...] = m_new

    @pl.when(j == 0)
    def _first():
        for h in range(Q_PER_KV):
            qs_ref[h * tq:(h + 1) * tq, :] = q_ref[:, h * HEAD_DIM:(h + 1) * HEAD_DIM]
        m_ref[...] = jnp.full(m_ref.shape, -jnp.inf, F32)
        l_ref[...] = jnp.zeros(l_ref.shape, F32)
        acc_ref[...] = jnp.zeros(acc_ref.shape, F32)
        scores(0)

    for parity in (0, 1):
        @pl.when((j > 0) & (j < nkb) & (j % 2 == parity))
        def _mid(parity=parity):
            scores(parity)
            absorb(1 - parity)

    @pl.when(j == nkb)
    def _last():
        absorb((nkb - 1) % 2)
        o = acc_ref[...] / l_ref[...]
        for h in range(Q_PER_KV):
            o_ref[:, h * HEAD_DIM:(h + 1) * HEAD_DIM] = o[:, h * tq:(h + 1) * tq].T.astype(BF16)


def _pick_block(total, cap):
    best = 128
    for b in range(128, cap + 1, 128):
        if total % b == 0:
            best = b
    return best


def _attn_call(q, k_all, vt_all):
    l = q.shape[0]
    s_len = k_all.shape[0]
    tq = min(ATTN_TQ, l)
    tk = _pick_block(s_len, ATTN_TK)
    nkb = s_len // tk
    gw = Q_PER_KV * HEAD_DIM
    n = Q_PER_KV * tq
    return pl.pallas_call(
        functools.partial(_attn_kernel, nkb, tq),
        grid=(N_KV_HEADS, l // tq, nkb + 1),
        in_specs=[pl.BlockSpec((tq, gw), lambda g, i, j: (i, g)),
                  pl.BlockSpec((tk, HEAD_DIM), lambda g, i, j: (jnp.minimum(j, nkb - 1), g)),
                  pl.BlockSpec((HEAD_DIM, tk), lambda g, i, j: (g, jnp.maximum(j - 1, 0)))],
        out_specs=pl.BlockSpec((tq, gw), lambda g, i, j: (i, g)),
        out_shape=jax.ShapeDtypeStruct(q.shape, BF16),
        scratch_shapes=[pltpu.VMEM((n, HEAD_DIM), BF16),
                        pltpu.VMEM((tk, n), F32),
                        pltpu.VMEM((tk, n), F32),
                        pltpu.VMEM((1, n), F32),
                        pltpu.VMEM((1, n), F32),
                        pltpu.VMEM((1, n), F32),
                        pltpu.VMEM((1, n), F32),
                        pltpu.VMEM((HEAD_DIM, n), F32)],
        compiler_params=_cparams(("arbitrary", "arbitrary", "arbitrary")),
        name="attention",
    )(q, k_all, vt_all)


def _ssm_tables(lam_re, lam_im, log_dt, b_re, b_im, c_re, c_im, d):
    t = SSM_CHUNK
    g, p = lam_re.shape[1:]
    hdim = b_re.shape[-1]
    lr = lam_re.astype(F32)
    li = lam_im.astype(F32)
    dt = jnp.exp(log_dt.astype(F32))[..., None]
    mag = jnp.exp(lr * dt)
    ar = mag * jnp.cos(li * dt)
    ai = mag * jnp.sin(li * dt)
    den = lr * lr + li * li
    cr = ((ar - 1.0) * lr + ai * li) / den
    ci = (ai * lr - (ar - 1.0) * li) / den
    br = b_re.astype(F32)
    bi = b_im.astype(F32)
    bbr = cr[..., None] * br - ci[..., None] * bi
    bbi = cr[..., None] * bi + ci[..., None] * br
    n = jnp.arange(t + 1, dtype=F32)[:, None, None, None]
    pmag = jnp.exp(n * (lr * dt)[None])
    pr = pmag * jnp.cos(n * (li * dt)[None])
    pi = pmag * jnp.sin(n * (li * dt)[None])
    cre = c_re.astype(F32)
    cim = c_im.astype(F32)

    def summ(direction, powers):
        er = pr[powers, direction][..., None] * bbr[direction][None] \
            - pi[powers, direction][..., None] * bbi[direction][None]
        ei = pr[powers, direction][..., None] * bbi[direction][None] \
            + pi[powers, direction][..., None] * bbr[direction][None]
        tr = lambda e: jnp.transpose(e, (1, 0, 3, 2)).reshape(g, t * hdim, p)
        return tr(er), tr(ei)
    steps = jnp.arange(t)
    pfr, pfi = summ(0, t - 1 - steps)
    pbr, pbi = summ(1, steps)
    p_mat = jnp.concatenate([pfr, pbr, pfi, pbi], axis=-1)

    a_vec = jnp.concatenate([pr[t, 0], pr[t, 1], pi[t, 0], pi[t, 1]], axis=-1)

    def kern(direction):
        er = pr[:t, direction][..., None] * bbr[direction][None] \
            - pi[:t, direction][..., None] * bbi[direction][None]
        ei = pr[:t, direction][..., None] * bbi[direction][None] \
            + pi[:t, direction][..., None] * bbr[direction][None]
        return (jnp.einsum('gop,ngpi->ngoi', cre[direction], er)
                - jnp.einsum('gop,ngpi->ngoi', cim[direction], ei))
    kf = kern(0)
    kb = kern(1)
    s_idx = steps[:, None]
    t_idx = steps[None, :]
    lag_f = jnp.clip(t_idx - s_idx, 0, t - 1)
    lag_b = jnp.clip(s_idx - t_idx, 0, t - 1)
    mf = jnp.where((t_idx >= s_idx)[:, :, None, None, None], kf[lag_f], 0.0)
    mb = jnp.where((s_idx >= t_idx)[:, :, None, None, None], kb[lag_b], 0.0)
    dd = d.astype(F32).reshape(g, hdim)
    eye_t = jnp.eye(t, dtype=F32)[:, :, None, None, None]
    eye_c = jnp.eye(hdim, dtype=F32)[None, None, None]
    m5 = mf + mb + eye_t * eye_c * dd[None, None, :, :, None]
    m_mat = jnp.transpose(m5, (2, 0, 4, 1, 3)).reshape(g, t * hdim, t * hdim)

    def state_out(direction, powers):
        prn = pr[powers, direction]
        pin = pi[powers, direction]
        wr = cre[direction][None] * prn[:, :, None, :] - cim[direction][None] * pin[:, :, None, :]
        wi = -cre[direction][None] * pin[:, :, None, :] - cim[direction][None] * prn[:, :, None, :]
        tr = lambda w: jnp.transpose(w, (1, 3, 0, 2)).reshape(g, p, t * hdim)
        return tr(wr), tr(wi)
    qfr, qfi = state_out(0, steps + 1)
    qbr, qbi = state_out(1, t - steps)
    z = jnp.zeros_like(qfr)
    qf = jnp.concatenate([qfr, z, qfi, z], axis=1)
    qb = jnp.concatenate([z, qbr, z, qbi], axis=1)
    w_out = jnp.concatenate([m_mat, qf, qb], axis=1)
    return p_mat.astype(BF16), a_vec, w_out.astype(BF16)


def _ssm_sum_kernel(x_ref, p_ref, s_ref):
    s_ref[0] = jnp.dot(x_ref[0], p_ref[0], preferred_element_type=F32)


def _ssm_sum_call(xg, p_mat):
    g, n, k = xg.shape
    sw = p_mat.shape[2]
    return pl.pallas_call(
        _ssm_sum_kernel,
        grid=(g,),
        in_specs=[pl.BlockSpec((1, n, k), lambda i: (i, 0, 0)),
                  pl.BlockSpec((1, k, sw), lambda i: (i, 0, 0))],
        out_specs=pl.BlockSpec((1, n, sw), lambda i: (i, 0, 0)),
        out_shape=jax.ShapeDtypeStruct((g, n, sw), F32),
        compiler_params=_cparams(("arbitrary",)),
        name="ssm_summaries",
    )(xg, p_mat)


def _ssm_scan_kernel(kb, sf_ref, sb_ref, a_ref, h0_ref, hf_ref, hb_ref, hend_ref, h_ref):
    i = pl.program_id(0)

    @pl.when(i == 0)
    def _init():
        h_ref[...] = h0_ref[...]

    half = a_ref.shape[1] // 2
    a_r = a_ref[:, 0:half]
    a_i = a_ref[:, half:2 * half]
    lane = lax.broadcasted_iota(jnp.int32, h_ref.shape, 1)
    fwd = (lane % half) < (half // 2)

    def body(kk, h):
        hf_ref[kk] = h.astype(hf_ref.dtype)
        hb_ref[kb - 1 - kk] = h.astype(hb_ref.dtype)
        s = jnp.where(fwd, sf_ref[kk], sb_ref[kb - 1 - kk])
        hr = h[:, 0:half]
        hi = h[:, half:2 * half]
        nr = a_r * hr - a_i * hi + s[:, 0:half]
        ni = a_r * hi + a_i * hr + s[:, half:2 * half]
        return jnp.concatenate([nr, ni], axis=1)

    h = lax.fori_loop(0, kb, body, h_ref[...])
    h_ref[...] = h
    hend_ref[...] = h


def _ssm_scan_call(s_t, a_vec, h0):
    n, g, sw = s_t.shape
    kb = min(128, n)
    nb = n // kb
    return pl.pallas_call(
        functools.partial(_ssm_scan_kernel, kb),
        grid=(nb,),
        in_specs=[pl.BlockSpec((kb, g, sw), lambda i: (i, 0, 0)),
                  pl.BlockSpec((kb, g, sw), lambda i: (nb - 1 - i, 0, 0)),
                  pl.BlockSpec((g, sw), lambda i: (0, 0)),
                  pl.BlockSpec((g, sw), lambda i: (0, 0))],
        out_specs=[pl.BlockSpec((kb, g, sw), lambda i: (i, 0, 0)),
                   pl.BlockSpec((kb, g, sw), lambda i: (nb - 1 - i, 0, 0)),
                   pl.BlockSpec((g, sw), lambda i: (0, 0))],
        out_shape=[jax.ShapeDtypeStruct((n, g, sw), BF16),
                   jax.ShapeDtypeStruct((n, g, sw), BF16),
                   jax.ShapeDtypeStruct((g, sw), F32)],
        scratch_shapes=[pltpu.VMEM((g, sw), F32)],
        compiler_params=_cparams(("arbitrary",)),
        name="ssm_chunk_scan",
    )(s_t, s_t, a_vec, h0)


def _ssm_out_kernel(x_ref, hf_ref, hb_ref, w_ref, y_ref):
    k = x_ref.shape[2]
    sw = hf_ref.shape[2]
    y = jnp.dot(x_ref[0], w_ref[0, 0:k, :], preferred_element_type=F32)
    y = y + jnp.dot(hf_ref[0], w_ref[0, k:k + sw, :], preferred_element_type=F32)
    y = y + jnp.dot(hb_ref[0], w_ref[0, k + sw:k + 2 * sw, :], preferred_element_type=F32)
    y_ref[0] = y.astype(y_ref.dtype)


def _ssm_out_call(xg, hf, hb, w_out):
    g, n, k = xg.shape
    sw = hf.shape[2]
    blk = lambda w: pl.BlockSpec((1, n, w), lambda i: (i, 0, 0))
    return pl.pallas_call(
        _ssm_out_kernel,
        grid=(g,),
        in_specs=[blk(k), blk(sw), blk(sw),
                  pl.BlockSpec((1,) + w_out.shape[1:], lambda i: (i, 0, 0))],
        out_specs=blk(k),
        out_shape=jax.ShapeDtypeStruct((g, n, k), BF16),
        compiler_params=_cparams(("arbitrary",)),
        name="ssm_outputs",
    )(xg, hf, hb, w_out)


def _ssm_apply(us, tables, h0, need_y):
    p_mat, a_vec, w_out = tables
    l, w = us.shape
    g = w // SSM_GROUP
    n = l // SSM_CHUNK
    xg = jnp.transpose(us.reshape(n, SSM_CHUNK, g, SSM_GROUP), (2, 0, 1, 3)).reshape(
        g, n, SSM_CHUNK * SSM_GROUP)
    s = _ssm_sum_call(xg, p_mat)
    hf, hb, hend = _ssm_scan_call(jnp.transpose(s, (1, 0, 2)), a_vec, h0)
    if not need_y:
        return None, hend
    yg = _ssm_out_call(xg, jnp.transpose(hf, (1, 0, 2)), jnp.transpose(hb, (1, 0, 2)), w_out)
    y = jnp.transpose(yg.reshape(g, n, SSM_CHUNK, SSM_GROUP), (1, 2, 0, 3)).reshape(l, w)
    return y, hend


def _dft_tables(l, n1, hw):
    n2 = l // n1
    scale = 1.0 / math.sqrt(l * hw)
    if n1 == 1:
        t1 = None
    else:
        ang = 2.0 * np.pi * ((np.arange(n1)[:, None] * np.arange(n1)[None, :]) % n1) / n1
        c, s = np.cos(ang), np.sin(ang)
        t1 = jnp.asarray(np.block([[c, -s], [-s, -c]]), dtype=BF16)
    k1 = np.arange(n1)[:, None, None]
    k2 = np.arange(n2)[None, :, None]
    m = np.arange(n2)[None, None, :]
    ang = 2.0 * np.pi * ((m * (n1 * k2 + k1)) % l) / l
    sign = -1.0 if n1 == 1 else 1.0
    g = np.concatenate([np.cos(ang), sign * np.sin(ang)], axis=-1) * scale
    return t1, jnp.asarray(g, dtype=BF16)


def _dft1_kernel(t_ref, x_ref, z_ref):
    z_ref[...] = jnp.dot(t_ref[...], x_ref[...], preferred_element_type=F32).astype(z_ref.dtype)


def _dft1_call(t1, ab2d):
    r, n = ab2d.shape
    tn = min(4096, n)
    return pl.pallas_call(
        _dft1_kernel,
        grid=(n // tn,),
        in_specs=[pl.BlockSpec((r, r), lambda j: (0, 0)),
                  pl.BlockSpec((r, tn), lambda j: (0, j))],
        out_specs=pl.BlockSpec((r, tn), lambda j: (0, j)),
        out_shape=jax.ShapeDtypeStruct((r, n), BF16),
        compiler_params=_cparams(("arbitrary",)),
        name="dft_stage1",
    )(t1, ab2d)


def _dft2_kernel(bt, zr_ref, zi_ref, g_ref, w_ref, o_ref):
    fw = w_ref.shape[0]
    for b in range(bt):
        z = jnp.concatenate([zr_ref[b], zi_ref[b]], axis=0)
        x = jnp.dot(g_ref[b], z, preferred_element_type=F32)
        o_ref[:, b * fw:(b + 1) * fw] = jnp.dot(
            x.astype(BF16), w_ref[...], preferred_element_type=F32).astype(o_ref.dtype)


def _dft2_call(z3, g, fourier_w):
    n1x2, n2, fw = z3.shape
    n1 = n1x2 // 2
    bt = min(8, n1)
    return pl.pallas_call(
        functools.partial(_dft2_kernel, bt),
        grid=(n1 // bt,),
        in_specs=[pl.BlockSpec((bt, n2, fw), lambda b: (b, 0, 0)),
                  pl.BlockSpec((bt, n2, fw), lambda b: (b + n1 // bt, 0, 0)),
                  pl.BlockSpec((bt, n2, 2 * n2), lambda b: (b, 0, 0)),
                  pl.BlockSpec(fourier_w.shape, lambda b: (0, 0))],
        out_specs=pl.BlockSpec((n2, bt * fw), lambda b: (0, b)),
        out_shape=jax.ShapeDtypeStruct((n2, n1 * fw), BF16),
        compiler_params=_cparams(("arbitrary",)),
        name="dft_stage2",
    )(z3, z3, g, fourier_w)


def _fourier_apply(ab, tables, fourier_w, n1):
    t1, g = tables
    _, l, fw = ab.shape
    n2 = l // n1
    if n1 == 1:
        z3 = ab
    else:
        z3 = _dft1_call(t1, ab.reshape(2 * n1, n2 * fw)).reshape(2 * n1, n2, fw)
    return _dft2_call(z3, g, fourier_w).reshape(l, fw)


def _outproj_kernel(attn_ref, ys_ref, four_ref, x_ref, wo_ref, gw_ref, gb_ref, npost_ref,
                    gate_ref, npre_ref, sh_ref, sc_ref, xo_ref, h_ref):
    ys = ys_ref[...].astype(F32)
    gl = 0.5 * ys * (1.0 + jnp.tanh(math.sqrt(2.0 / math.pi) * (ys + 0.044715 * (ys * ys * ys))))
    z = jnp.dot(gl.astype(BF16), gw_ref[...], preferred_element_type=F32) + gb_ref[...]
    ssm = (gl * jax.nn.sigmoid(z)).astype(BF16)
    cat = jnp.concatenate([attn_ref[...], ssm, four_ref[...]], axis=-1)
    y = jnp.dot(cat, wo_ref[...], preferred_element_type=F32)
    xn = x_ref[...] + gate_ref[...] * _rms(y, npost_ref[...])
    xo_ref[...] = xn
    h_ref[...] = (_rms(xn, npre_ref[...]) * (1.0 + sc_ref[...]) + sh_ref[...]).astype(BF16)


def _outproj_call(attn, ys, four, x, w_out, glu_w, glu_b, npost, gate, npre, shift, scale):
    l, d = x.shape
    tm = min(512, l)
    row = lambda i: (i, 0)
    fix = lambda i: (0, 0)
    vec = pl.BlockSpec((1, d), fix)
    return pl.pallas_call(
        _outproj_kernel,
        grid=(l // tm,),
        in_specs=[pl.BlockSpec((tm, attn.shape[1]), row),
                  pl.BlockSpec((tm, ys.shape[1]), row),
                  pl.BlockSpec((tm, four.shape[1]), row),
                  pl.BlockSpec((tm, d), row),
                  pl.BlockSpec(w_out.shape, fix),
                  pl.BlockSpec(glu_w.shape, fix),
                  pl.BlockSpec((1, glu_w.shape[1]), fix),
                  vec, vec, vec, vec, vec],
        out_specs=[pl.BlockSpec((tm, d), row), pl.BlockSpec((tm, d), row)],
        out_shape=[jax.ShapeDtypeStruct((l, d), F32), jax.ShapeDtypeStruct((l, d), BF16)],
        compiler_params=_cparams(("arbitrary",)),
        name="out_proj",
    )(attn, ys, four, x, w_out, glu_w, glu_b, npost, gate, npre, shift, scale)


def _ffn_kernel(nj, h_ref, x_ref, wg_ref, wu_ref, wd_ref, npost_ref, gate_ref, o_ref, acc_ref):
    j = pl.program_id(1)
    h = h_ref[...]
    a = jnp.dot(h, wg_ref[...], preferred_element_type=F32)
    u = jnp.dot(h, wu_ref[...], preferred_element_type=F32)
    act = (a * jax.nn.sigmoid(a) * u).astype(BF16)
    part = jnp.dot(act, wd_ref[...], preferred_element_type=F32)

    @pl.when(j == 0)
    def _first():
        acc_ref[...] = part

    @pl.when(j > 0)
    def _rest():
        acc_ref[...] += part

    @pl.when(j == nj - 1)
    def _finish():
        o_ref[...] = x_ref[...] + gate_ref[...] * _rms(acc_ref[...], npost_ref[...])


def _ffn_call(h, x, w_gate, w_up, w_down, npost, gate):
    l, d = x.shape
    fh = w_gate.shape[1]
    tm = min(512, l)
    th = 512
    nj = fh // th
    vec = pl.BlockSpec((1, d), lambda i, j: (0, 0))
    return pl.pallas_call(
        functools.partial(_ffn_kernel, nj),
        grid=(l // tm, nj),
        in_specs=[pl.BlockSpec((tm, d), lambda i, j: (i, 0)),
                  pl.BlockSpec((tm, d), lambda i, j: (i, 0)),
                  pl.BlockSpec((d, th), lambda i, j: (0, j)),
                  pl.BlockSpec((d, th), lambda i, j: (0, j)),
                  pl.BlockSpec((th, d), lambda i, j: (j, 0)),
                  vec, vec],
        out_specs=pl.BlockSpec((tm, d), lambda i, j: (i, 0)),
        out_shape=jax.ShapeDtypeStruct((l, d), F32),
        scratch_shapes=[pltpu.VMEM((tm, d), F32)],
        compiler_params=_cparams(("arbitrary", "arbitrary")),
        name="ffn",
    )(h, x, w_gate, w_up, w_down, npost, gate)


def _rope_tables(l):
    t = np.arange(l)
    row = (t // GRID_W).astype(np.float32)
    col = (t % GRID_W).astype(np.float32)
    freqs = np.float32(ROPE_THETA) ** (-np.arange(ROPE_PAIRS, dtype=np.float32) / np.float32(ROPE_PAIRS))
    ang_r = (row[:, None] * freqs).astype(np.float32)
    ang_c = (col[:, None] * freqs).astype(np.float32)
    cos = np.concatenate([np.cos(ang_r)] * 2 + [np.cos(ang_c)] * 2, axis=-1)
    sin = np.concatenate([-np.sin(ang_r), np.sin(ang_r), -np.sin(ang_c), np.sin(ang_c)], axis=-1)
    return jnp.asarray(cos, dtype=F32), jnp.asarray(sin, dtype=F32)


def _channel_dft_table(fw):
    hw = FOURIER_HEAD_DIM
    ang = 2.0 * np.pi * ((np.arange(hw)[:, None] * np.arange(hw)[None, :]) % hw) / hw
    eye = np.eye(fw // hw)
    return jnp.asarray(np.concatenate([np.kron(eye, np.cos(ang)), np.kron(eye, np.sin(ang))], axis=1),
                       dtype=BF16)


def _dft_split(l):
    n1 = 1
    while n1 * n1 < l:
        n1 *= 2
    return n1 if (l >= 1024 and n1 * n1 == l) else 1


def kernel(x, c, ctx, c_ctx, ada_w, ada_b, norm_mix_pre, norm_mix_post, norm_ffn_pre, norm_ffn_post, w_in, q_norm, k_norm, ssm_lam_re, ssm_lam_im, ssm_log_dt, ssm_b_re, ssm_b_im, ssm_c_re, ssm_c_im, ssm_d, ssm_glu_w, ssm_glu_b, fourier_w, w_out, ffn_w_gate, ffn_w_up, ffn_w_down):
    depth = ada_w.shape[0]
    _, l, d = x.shape
    n_ctx = ctx.shape[1]
    fw = fourier_w.shape[1]
    sw = ssm_d.shape[1]
    n_groups = sw // SSM_GROUP

    cond = jnp.zeros((8, d), F32).at[0].set(c[0]).at[1].set(c_ctx)
    mod = _ada_call(cond, ada_w, ada_b)

    cos, sin = _rope_tables(l)
    zero_tab = jnp.zeros((n_ctx, HEAD_DIM), F32)
    cs = _channel_dft_table(fw)
    n1 = _dft_split(l)
    n1c = _dft_split(n_ctx)
    dft_x = _dft_tables(l, n1, FOURIER_HEAD_DIM)
    dft_c = _dft_tables(n_ctx, n1c, FOURIER_HEAD_DIM)
    h0 = jnp.zeros((n_groups, 4 * SSM_STATE), F32)

    xs = x[0]
    xc = ctx[0]
    for layer in range(depth):
        need_ctx = layer < depth - 1
        vecs = lambda r: [mod[layer, r:r + 1, i * d:(i + 1) * d] for i in range(6)]
        sh_m, sc_m, g_m, sh_f, sc_f, g_f = vecs(0)
        shc_m, scc_m, gc_m, shc_f, scc_f, gc_f = vecs(1)
        row = lambda a: a[layer].reshape(1, -1)

        w_in_l = w_in[layer].astype(BF16)
        w_out_l = w_out[layer].astype(BF16)
        glu_w_l = ssm_glu_w[layer].astype(BF16)
        four_w_l = fourier_w[layer].astype(BF16)
        wg_l = ffn_w_gate[layer].astype(BF16)
        wu_l = ffn_w_up[layer].astype(BF16)
        wd_l = ffn_w_down[layer].astype(BF16)
        tables = _ssm_tables(ssm_lam_re[layer], ssm_lam_im[layer], ssm_log_dt[layer],
                             ssm_b_re[layer], ssm_b_im[layer], ssm_c_re[layer], ssm_c_im[layer],
                             ssm_d[layer])

        qc, kc, vtc, usc, abc = _inproj_call(xc, row(norm_mix_pre), shc_m, scc_m, w_in_l,
                                             row(q_norm), row(k_norm), zero_tab, zero_tab, cs, False)
        ysc, hend_c = _ssm_apply(usc, tables, h0, need_ctx)

        q, k, vt, us, ab = _inproj_call(xs, row(norm_mix_pre), sh_m, sc_m, w_in_l,
                                        row(q_norm), row(k_norm), cos, sin, cs, True)
        attn = _attn_call(q, jnp.concatenate([k, kc], axis=0), jnp.concatenate([vt, vtc], axis=1))
        ys, _ = _ssm_apply(us, tables, hend_c, True)
        four = _fourier_apply(ab, dft_x, four_w_l, n1)
        xs, hs = _outproj_call(attn, ys, four, xs, w_out_l, glu_w_l, row(ssm_glu_b),
                               row(norm_mix_post), g_m, row(norm_ffn_pre), sh_f, sc_f)
        xs = _ffn_call(hs, xs, wg_l, wu_l, wd_l, row(norm_ffn_post), g_f)

        if need_ctx:
            attn_c = _attn_call(qc, kc, vtc)
            four_c = _fourier_apply(abc, dft_c, four_w_l, n1c)
            xc, hc = _outproj_call(attn_c, ysc, four_c, xc, w_out_l, glu_w_l, row(ssm_glu_b),
                                   row(norm_mix_post), gc_m, row(norm_ffn_pre), shc_f, scc_f)
            xc = _ffn_call(hc, xc, wg_l, wu_l, wd_l, row(norm_ffn_post), gc_f)
    return xs[None]
```

```python
import functools
import math

import numpy as np
import jax
import jax.numpy as jnp
from jax import lax
from jax.experimental import pallas as pl
from jax.experimental.pallas import tpu as pltpu

F32 = jnp.float32
BF16 = jnp.bfloat16

HEAD_DIM = 128
N_KV_HEADS = 2
Q_PER_KV = 4
GRID_W = 64
ROPE_THETA = 10000.0
ROPE_PAIRS = HEAD_DIM // 4
SSM_GROUP = 16
SSM_STATE = 64
FOURIER_HEAD_DIM = 128
NORM_EPS = 1e-6
SSM_CHUNK = 8
SSM_LANES = 128
LOG2E = 1.4426950408889634
ATTN_TQ = 512
ATTN_TK = 1280
ATTN_SUB = 256
VT_PAD = 16
VT_ROWS = HEAD_DIM + VT_PAD

VMEM_LIMIT_MB = 56


def _cparams(semantics, vmem_mb=VMEM_LIMIT_MB):
    return pltpu.CompilerParams(dimension_semantics=semantics,
                                vmem_limit_bytes=vmem_mb * 2 ** 20)


def _rms(x, gain):
    return x * lax.rsqrt(jnp.mean(x * x, axis=-1, keepdims=True) + NORM_EPS) * gain


def _ada_kernel(s_ref, w_ref, b_ref, o_ref):
    s = s_ref[...]
    act = (s * jax.nn.sigmoid(s)).astype(BF16)
    o_ref[0] = jnp.dot(act, w_ref[0].astype(BF16), preferred_element_type=F32) + b_ref[0]


def _ada_call(cond, ada_w, ada_b):
    depth, d, n = ada_w.shape
    tn = 1024
    return pl.pallas_call(
        _ada_kernel,
        grid=(depth, n // tn),
        in_specs=[pl.BlockSpec((8, d), lambda l, j: (0, 0)),
                  pl.BlockSpec((1, d, tn), lambda l, j: (l, 0, j)),
                  pl.BlockSpec((1, 1, tn), lambda l, j: (l, 0, j))],
        out_specs=pl.BlockSpec((1, 8, tn), lambda l, j: (l, 0, j)),
        out_shape=jax.ShapeDtypeStruct((depth, 8, n), F32),
        compiler_params=_cparams(("arbitrary", "arbitrary")),
        name="ada_ln",
    )(cond, ada_w, ada_b.reshape(depth, 1, n))


def _inproj_kernel(rope, x_ref, g_ref, sh_ref, sc_ref, w_ref, qn_ref, kn_ref, cos_ref, sin_ref,
                   cs_ref, q_ref, k_ref, vt_ref, us_ref, ab_ref):
    x = x_ref[...]
    h = (_rms(x, g_ref[...]) * (1.0 + sc_ref[...]) + sh_ref[...]).astype(BF16)
    n_q = q_ref.shape[1] // HEAD_DIM
    n_k = k_ref.shape[1] // HEAD_DIM
    q_end = n_q * HEAD_DIM
    k_end = q_end + n_k * HEAD_DIM
    v_end = k_end + n_k * HEAD_DIM
    s_end = v_end + us_ref.shape[1]
    f_end = s_end + ab_ref.shape[2]

    if rope:
        cos = cos_ref[...]
        sin = sin_ref[...]
        lane = lax.broadcasted_iota(jnp.int32, cos.shape, 1)
        low = (lane % 64) < 32

    def head(t, gain, scale):
        t = _rms(t, gain)
        if rope:
            sw = jnp.where(low, pltpu.roll(t, HEAD_DIM - 32, 1), pltpu.roll(t, 32, 1))
            t = t * cos + sw * sin
        if scale != 1.0:
            t = t * scale
        return t.astype(BF16)

    qk = jnp.dot(h, w_ref[:, 0:k_end], preferred_element_type=F32)
    q_scale = HEAD_DIM ** -0.5 * LOG2E
    for i in range(n_q):
        q_ref[:, i * HEAD_DIM:(i + 1) * HEAD_DIM] = head(
            qk[:, i * HEAD_DIM:(i + 1) * HEAD_DIM], qn_ref[...], q_scale)
    for i in range(n_k):
        k_ref[:, i * HEAD_DIM:(i + 1) * HEAD_DIM] = head(
            qk[:, q_end + i * HEAD_DIM:q_end + (i + 1) * HEAD_DIM], kn_ref[...], 1.0)
    rest = jnp.dot(h, w_ref[:, k_end:f_end], preferred_element_type=F32)
    tm = x.shape[0]
    ones_rows = (lax.broadcasted_iota(jnp.int32, (VT_PAD, tm), 0) == 0).astype(BF16)
    for i in range(n_k):
        base = i * VT_ROWS
        vt_ref[base:base + HEAD_DIM, :] = rest[:, i * HEAD_DIM:(i + 1) * HEAD_DIM].T.astype(BF16)
        vt_ref[base + HEAD_DIM:base + VT_ROWS, :] = ones_rows
    us_ref[...] = rest[:, v_end - k_end:s_end - k_end]
    uf = rest[:, s_end - k_end:f_end - k_end].astype(BF16)
    ab = jnp.dot(uf, cs_ref[...], preferred_element_type=F32)
    fw = ab_ref.shape[2]
    ab_ref[0] = ab[:, 0:fw].astype(BF16)
    ab_ref[1] = ab[:, fw:2 * fw].astype(BF16)


def _inproj_call(x, gain, shift, scale, w_in, qn, kn, cos, sin, cs, rope):
    l, d = x.shape
    tm = min(512, l)
    qw = Q_PER_KV * N_KV_HEADS * HEAD_DIM
    kw = N_KV_HEADS * HEAD_DIM
    fw = cs.shape[0]
    sw = w_in.shape[1] - qw - 2 * kw - fw
    row = lambda i: (i, 0)
    fix = lambda i: (0, 0)
    return pl.pallas_call(
        functools.partial(_inproj_kernel, rope),
        grid=(l // tm,),
        in_specs=[pl.BlockSpec((tm, d), row),
                  pl.BlockSpec((1, d), fix), pl.BlockSpec((1, d), fix), pl.BlockSpec((1, d), fix),
                  pl.BlockSpec(w_in.shape, fix),
                  pl.BlockSpec((1, HEAD_DIM), fix), pl.BlockSpec((1, HEAD_DIM), fix),
                  pl.BlockSpec((tm, HEAD_DIM), row), pl.BlockSpec((tm, HEAD_DIM), row),
                  pl.BlockSpec(cs.shape, fix)],
        out_specs=[pl.BlockSpec((tm, qw), row),
                   pl.BlockSpec((tm, kw), row),
                   pl.BlockSpec((N_KV_HEADS * VT_ROWS, tm), lambda i: (0, i)),
                   pl.BlockSpec((tm, sw), row),
                   pl.BlockSpec((2, tm, fw), lambda i: (0, i, 0))],
        out_shape=[jax.ShapeDtypeStruct((l, qw), BF16),
                   jax.ShapeDtypeStruct((l, kw), BF16),
                   jax.ShapeDtypeStruct((N_KV_HEADS * VT_ROWS, l), BF16),
                   jax.ShapeDtypeStruct((l, sw), F32),
                   jax.ShapeDtypeStruct((2, l, fw), BF16)],
        compiler_params=_cparams(("arbitrary",)),
        name="in_proj",
    )(x, gain, shift, scale, w_in, qn, kn, cos, sin, cs)


def _attn_kernel(nkb, tq, q_ref, k_ref, vt_ref, o_ref, qs_ref, s0_ref, s1_ref, bm0_ref, bm1_ref,
                 m_ref, acc_ref):
    j = pl.program_id(2)
    s_refs = (s0_ref, s1_ref)
    bm_refs = (bm0_ref, bm1_ref)

    tk = k_ref.shape[0]
    sub = min(ATTN_SUB, tk)

    def step(score_slot, absorb_slot):
        if absorb_slot is not None:
            m_prev = m_ref[...]
            m_new = jnp.maximum(m_prev, bm_refs[absorb_slot][...])
            alpha = jnp.exp2(m_prev - m_new)
            pv = jnp.zeros(acc_ref.shape, F32)
        bmax = None
        for c in range(tk // sub):
            rows = slice(c * sub, (c + 1) * sub)
            if score_slot is not None:
                s = lax.dot_general(k_ref[rows, :], qs_ref[...], (((1,), (1,)), ((), ())),
                                    preferred_element_type=F32)
                s_refs[score_slot][rows, :] = s
                cmax = jnp.max(s, axis=0, keepdims=True)
                bmax = cmax if bmax is None else jnp.maximum(bmax, cmax)
            if absorb_slot is not None:
                p = jnp.exp2((s_refs[absorb_slot][rows, :] - m_new).astype(BF16))
                pv = pv + jnp.dot(vt_ref[:, rows], p, preferred_element_type=F32)
        if score_slot is not None:
            bm_refs[score_slot][...] = bmax
        if absorb_slot is not None:
            acc_ref[...] = alpha * acc_ref[...] + pv
            m_ref[...] = m_new

    @pl.when(j == 0)
    def _first():
        for h in range(Q_PER_KV):
            qs_ref[h * tq:(h + 1) * tq, :] = q_ref[:, h * HEAD_DIM:(h + 1) * HEAD_DIM]
        m_ref[...] = jnp.full(m_ref.shape, -jnp.inf, F32)
        acc_ref[...] = jnp.zeros(acc_ref.shape, F32)
        step(0, None)

    for parity in (0, 1):
        @pl.when((j > 0) & (j < nkb) & (j % 2 == parity))
        def _mid(parity=parity):
            step(parity, 1 - parity)

    @pl.when(j == nkb)
    def _last():
        step(None, (nkb - 1) % 2)
        o = acc_ref[0:HEAD_DIM, :] / acc_ref[HEAD_DIM:HEAD_DIM + 1, :]
        for h in range(Q_PER_KV):
            o_ref[:, h * HEAD_DIM:(h + 1) * HEAD_DIM] = o[:, h * tq:(h + 1) * tq].T.astype(BF16)


def _pick_block(total, cap):
    best = 128
    for b in range(128, cap + 1, 128):
        if total % b == 0:
            best = b
    return best


def _attn_call(q, k_all, vt_all):
    l = q.shape[0]
    s_len = k_all.shape[0]
    tq = min(ATTN_TQ, l)
    tk = _pick_block(s_len, ATTN_TK)
    nkb = s_len // tk
    gw = Q_PER_KV * HEAD_DIM
    n = Q_PER_KV * tq
    return pl.pallas_call(
        functools.partial(_attn_kernel, nkb, tq),
        grid=(N_KV_HEADS, l // tq, nkb + 1),
        in_specs=[pl.BlockSpec((tq, gw), lambda g, i, j: (i, g)),
                  pl.BlockSpec((tk, HEAD_DIM), lambda g, i, j: (jnp.minimum(j, nkb - 1), g)),
                  pl.BlockSpec((VT_ROWS, tk), lambda g, i, j: (g, jnp.maximum(j - 1, 0)))],
        out_specs=pl.BlockSpec((tq, gw), lambda g, i, j: (i, g)),
        out_shape=jax.ShapeDtypeStruct(q.shape, BF16),
        scratch_shapes=[pltpu.VMEM((n, HEAD_DIM), BF16),
                        pltpu.VMEM((tk, n), F32),
                        pltpu.VMEM((tk, n), F32),
                        pltpu.VMEM((1, n), F32),
                        pltpu.VMEM((1, n), F32),
                        pltpu.VMEM((1, n), F32),
                        pltpu.VMEM((VT_ROWS, n), F32)],
        compiler_params=_cparams(("arbitrary", "arbitrary", "arbitrary")),
        name="attention",
    )(q, k_all, vt_all)


def _ssm_tables(lam_re, lam_im, log_dt, b_re, b_im, c_re, c_im, d):
    t = SSM_CHUNK
    g, p = lam_re.shape[1:]
    hdim = b_re.shape[-1]
    lr = lam_re.astype(F32)
    li = lam_im.astype(F32)
    dt = jnp.exp(log_dt.astype(F32))[..., None]
    mag = jnp.exp(lr * dt)
    ar = mag * jnp.cos(li * dt)
    ai = mag * jnp.sin(li * dt)
    den = lr * lr + li * li
    cr = ((ar - 1.0) * lr + ai * li) / den
    ci = (ai * lr - (ar - 1.0) * li) / den
    br = b_re.astype(F32)
    bi = b_im.astype(F32)
    bbr = cr[..., None] * br - ci[..., None] * bi
    bbi = cr[..., None] * bi + ci[..., None] * br
    n = jnp.arange(t + 1, dtype=F32)[:, None, None, None]
    pmag = jnp.exp(n * (lr * dt)[None])
    pr = pmag * jnp.cos(n * (li * dt)[None])
    pi = pmag * jnp.sin(n * (li * dt)[None])
    cre = c_re.astype(F32)
    cim = c_im.astype(F32)

    gs = SSM_LANES // hdim
    n_sg = g // gs
    eye_g = jnp.eye(gs, dtype=F32)

    def expand(small):
        _, r1, i1, r2, i2 = small.shape
        big = jnp.einsum('sgaibj,gh->sagibhj', small.reshape(n_sg, gs, r1, i1, r2, i2), eye_g)
        return big.reshape(n_sg, r1 * gs * i1, r2 * gs * i2).astype(BF16)

    def summ(direction, powers):
        er = pr[powers, direction][..., None] * bbr[direction][None] \
            - pi[powers, direction][..., None] * bbi[direction][None]
        ei = pr[powers, direction][..., None] * bbi[direction][None] \
            + pi[powers, direction][..., None] * bbr[direction][None]
        tr = lambda e: jnp.transpose(e, (1, 0, 3, 2))
        return expand(jnp.stack([tr(er), tr(ei)], axis=3))
    steps = jnp.arange(t)
    p_f = summ(0, t - 1 - steps)
    p_b = summ(1, steps)

    row = lambda v: v.reshape(n_sg, gs * p)
    a_row = jnp.concatenate([row(pr[t, 0]), row(pr[t, 1]), row(pi[t, 0]), row(pi[t, 1])],
                            axis=-1)[:, None, :]

    def kern(direction):
        er = pr[:t, direction][..., None] * bbr[direction][None] \
            - pi[:t, direction][..., None] * bbi[direction][None]
        ei = pr[:t, direction][..., None] * bbi[direction][None] \
            + pi[:t, direction][..., None] * bbr[direction][None]
        return (jnp.einsum('gop,ngpi->ngoi', cre[direction], er)
                - jnp.einsum('gop,ngpi->ngoi', cim[direction], ei))
    kf = kern(0)
    kb = kern(1)
    s_idx = steps[:, None]
    t_idx = steps[None, :]
    lag_f = jnp.clip(t_idx - s_idx, 0, t - 1)
    lag_b = jnp.clip(s_idx - t_idx, 0, t - 1)
    mf = jnp.where((t_idx >= s_idx)[:, :, None, None, None], kf[lag_f], 0.0)
    mb = jnp.where((s_idx >= t_idx)[:, :, None, None, None], kb[lag_b], 0.0)
    dd = d.astype(F32).reshape(g, hdim)
    eye_t = jnp.eye(t, dtype=F32)[:, :, None, None, None]
    eye_c = jnp.eye(hdim, dtype=F32)[None, None, None]
    m5 = mf + mb + eye_t * eye_c * dd[None, None, :, :, None]
    m_mat = expand(jnp.transpose(m5, (2, 0, 4, 1, 3)))

    def state_out(direction, powers):
        prn = pr[powers, direction]
        pin = pi[powers, direction]
        wr = cre[direction][None] * prn[:, :, None, :] - cim[direction][None] * pin[:, :, None, :]
        wi = -cre[direction][None] * pin[:, :, None, :] - cim[direction][None] * prn[:, :, None, :]
        tr = lambda w: jnp.transpose(w, (1, 3, 0, 2))
        return expand(jnp.stack([tr(wr), tr(wi)], axis=1))
    q_f = state_out(0, steps + 1)
    q_b = state_out(1, t - steps)
    return p_f, p_b, m_mat, q_f, q_b, a_row


def _ssm_kernel(tbk, usf_ref, usb_ref, pf_ref, pb_ref, m_ref, qf_ref, qb_ref, a_ref, h0_ref,
                yf_ref, yb_ref, hend_ref, xf_ref, xb_ref, sf_ref, sb_ref, hf_ref, hb_ref, h_ref):
    t = SSM_CHUNK
    lanes = SSM_LANES

    @pl.when(pl.program_id(1) == 0)
    def _init():
        h_ref[...] = h0_ref[0]

    for tt in range(t):
        cols = slice(tt * lanes, (tt + 1) * lanes)
        xf_ref[:, cols] = usf_ref[pl.ds(tt, tbk, stride=t), :].astype(BF16)
        xb_ref[:, cols] = usb_ref[pl.ds(tt, tbk, stride=t), :].astype(BF16)
    sf_ref[...] = jnp.dot(xf_ref[...], pf_ref[0], preferred_element_type=F32)
    sb_ref[...] = jnp.dot(xb_ref[...], pb_ref[0], preferred_element_type=F32)

    w = a_ref.shape[2] // 4
    a = a_ref[0]
    arf, arb, aif, aib = (a[:, i * w:(i + 1) * w] for i in range(4))
    h = h_ref[...]

    def body(k, carry):
        hrf, hrb, hif, hib = carry
        kb = tbk - 1 - k
        hf_ref[pl.ds(k, 1), 0:w] = hrf
        hf_ref[pl.ds(k, 1), w:2 * w] = hif
        hb_ref[pl.ds(kb, 1), 0:w] = hrb
        hb_ref[pl.ds(kb, 1), w:2 * w] = hib
        sf = sf_ref[pl.ds(k, 1), :]
        sb = sb_ref[pl.ds(kb, 1), :]
        return (arf * hrf - aif * hif + sf[:, 0:w], arb * hrb - aib * hib + sb[:, 0:w],
                arf * hif + aif * hrf + sf[:, w:2 * w], arb * hib + aib * hrb + sb[:, w:2 * w])

    carry = lax.fori_loop(0, tbk, body, tuple(h[:, i * w:(i + 1) * w] for i in range(4)),
                          unroll=8)
    h = jnp.concatenate(carry, axis=1)
    h_ref[...] = h
    hend_ref[0] = h

    yf = (jnp.dot(xf_ref[...], m_ref[0], preferred_element_type=F32)
          + jnp.dot(hf_ref[...].astype(BF16), qf_ref[0], preferred_element_type=F32))
    yb = jnp.dot(hb_ref[...].astype(BF16), qb_ref[0], preferred_element_type=F32)
    for tt in range(t):
        cols = slice(tt * lanes, (tt + 1) * lanes)
        yf_ref[pl.ds(tt, tbk, stride=t), :] = yf[:, cols]
        yb_ref[pl.ds(tt, tbk, stride=t), :] = yb[:, cols]


def _ssm_apply(us, tables, h0):
    p_f, p_b, m_mat, q_f, q_b, a_row = tables
    l, w = us.shape
    n_sg = w // SSM_LANES
    n = l // SSM_CHUNK
    tbk = min(256, n)
    nb = n // tbk
    rows = tbk * SSM_CHUNK
    kx = SSM_CHUNK * SSM_LANES
    sw = p_f.shape[2]
    hw = a_row.shape[2]
    per_sg = lambda a: pl.BlockSpec((1,) + a.shape[1:], lambda s, i: (s, 0, 0))
    fwd = pl.BlockSpec((rows, SSM_LANES), lambda s, i: (i, s))
    bwd = pl.BlockSpec((rows, SSM_LANES), lambda s, i: (nb - 1 - i, s))
    return pl.pallas_call(
        functools.partial(_ssm_kernel, tbk),
        grid=(n_sg, nb),
        in_specs=[fwd, bwd, per_sg(p_f), per_sg(p_b), per_sg(m_mat), per_sg(q_f), per_sg(q_b),
                  per_sg(a_row), per_sg(h0)],
        out_specs=[fwd, bwd, per_sg(h0)],
        out_shape=[jax.ShapeDtypeStruct((l, w), F32), jax.ShapeDtypeStruct((l, w), F32),
                   jax.ShapeDtypeStruct(h0.shape, F32)],
        scratch_shapes=[pltpu.VMEM((tbk, kx), BF16), pltpu.VMEM((tbk, kx), BF16),
                        pltpu.VMEM((tbk, sw), F32), pltpu.VMEM((tbk, sw), F32),
                        pltpu.VMEM((tbk, sw), F32), pltpu.VMEM((tbk, sw), F32),
                        pltpu.VMEM((1, hw), F32)],
        compiler_params=_cparams(("arbitrary", "arbitrary")),
        name="ssm_scan",
    )(us, us, p_f, p_b, m_mat, q_f, q_b, a_row, h0)


def _dft_tables(l, n1, hw):
    n2 = l // n1
    scale = 1.0 / math.sqrt(l * hw)
    if n1 == 1:
        t1 = None
    else:
        ang = 2.0 * np.pi * ((np.arange(n1)[:, None] * np.arange(n1)[None, :]) % n1) / n1
        c, s = np.cos(ang), np.sin(ang)
        t1 = jnp.asarray(np.block([[c, -s], [-s, -c]]), dtype=BF16)
    k1 = np.arange(n1)[:, None, None]
    k2 = np.arange(n2)[None, :, None]
    m = np.arange(n2)[None, None, :]
    ang = 2.0 * np.pi * ((m * (n1 * k2 + k1)) % l) / l
    sign = -1.0 if n1 == 1 else 1.0
    g = np.concatenate([np.cos(ang), sign * np.sin(ang)], axis=-1) * scale
    return t1, jnp.asarray(g, dtype=BF16)


def _dft1_kernel(t_ref, x_ref, z_ref):
    z_ref[...] = jnp.dot(t_ref[...], x_ref[...], preferred_element_type=F32).astype(z_ref.dtype)


def _dft1_call(t1, ab2d):
    r, n = ab2d.shape
    tn = min(4096, n)
    return pl.pallas_call(
        _dft1_kernel,
        grid=(n // tn,),
        in_specs=[pl.BlockSpec((r, r), lambda j: (0, 0)),
                  pl.BlockSpec((r, tn), lambda j: (0, j))],
        out_specs=pl.BlockSpec((r, tn), lambda j: (0, j)),
        out_shape=jax.ShapeDtypeStruct((r, n), BF16),
        compiler_params=_cparams(("arbitrary",)),
        name="dft_stage1",
    )(t1, ab2d)


def _dft2_kernel(bt, zr_ref, zi_ref, g_ref, w_ref, o_ref):
    fw = w_ref.shape[0]
    for b in range(bt):
        z = jnp.concatenate([zr_ref[b], zi_ref[b]], axis=0)
        x = jnp.dot(g_ref[b], z, preferred_element_type=F32)
        o_ref[:, b * fw:(b + 1) * fw] = jnp.dot(
            x.astype(BF16), w_ref[...], preferred_element_type=F32).astype(o_ref.dtype)


def _dft2_call(z3, g, fourier_w):
    n1x2, n2, fw = z3.shape
    n1 = n1x2 // 2
    bt = min(8, n1)
    return pl.pallas_call(
        functools.partial(_dft2_kernel, bt),
        grid=(n1 // bt,),
        in_specs=[pl.BlockSpec((bt, n2, fw), lambda b: (b, 0, 0)),
                  pl.BlockSpec((bt, n2, fw), lambda b: (b + n1 // bt, 0, 0)),
                  pl.BlockSpec((bt, n2, 2 * n2), lambda b: (b, 0, 0)),
                  pl.BlockSpec(fourier_w.shape, lambda b: (0, 0))],
        out_specs=pl.BlockSpec((n2, bt * fw), lambda b: (0, b)),
        out_shape=jax.ShapeDtypeStruct((n2, n1 * fw), BF16),
        compiler_params=_cparams(("arbitrary",)),
        name="dft_stage2",
    )(z3, z3, g, fourier_w)


def _fourier_apply(ab, tables, fourier_w, n1):
    t1, g = tables
    _, l, fw = ab.shape
    n2 = l // n1
    if n1 == 1:
        z3 = ab
    else:
        z3 = _dft1_call(t1, ab.reshape(2 * n1, n2 * fw)).reshape(2 * n1, n2, fw)
    return _dft2_call(z3, g, fourier_w).reshape(l, fw)


def _outproj_kernel(attn_ref, yf_ref, yb_ref, four_ref, x_ref, wo_ref, gw_ref, gb_ref, npost_ref,
                    gate_ref, npre_ref, sh_ref, sc_ref, xo_ref, h_ref):
    ys = yf_ref[...] + yb_ref[...]
    gl = 0.5 * ys * (1.0 + jnp.tanh(math.sqrt(2.0 / math.pi) * (ys + 0.044715 * (ys * ys * ys))))
    z = jnp.dot(gl.astype(BF16), gw_ref[...], preferred_element_type=F32) + gb_ref[...]
    ssm = (gl * jax.nn.sigmoid(z)).astype(BF16)
    cat = jnp.concatenate([attn_ref[...], ssm, four_ref[...]], axis=-1)
    y = jnp.dot(cat, wo_ref[...], preferred_element_type=F32)
    xn = x_ref[...] + gate_ref[...] * _rms(y, npost_ref[...])
    xo_ref[...] = xn
    h_ref[...] = (_rms(xn, npre_ref[...]) * (1.0 + sc_ref[...]) + sh_ref[...]).astype(BF16)


def _outproj_call(attn, yf, yb, four, x, w_out, glu_w, glu_b, npost, gate, npre, shift, scale):
    l, d = x.shape
    tm = min(512, l)
    row = lambda i: (i, 0)
    fix = lambda i: (0, 0)
    vec = pl.BlockSpec((1, d), fix)
    return pl.pallas_call(
        _outproj_kernel,
        grid=(l // tm,),
        in_specs=[pl.BlockSpec((tm, attn.shape[1]), row),
                  pl.BlockSpec((tm, yf.shape[1]), row),
                  pl.BlockSpec((tm, yb.shape[1]), row),
                  pl.BlockSpec((tm, four.shape[1]), row),
                  pl.BlockSpec((tm, d), row),
                  pl.BlockSpec(w_out.shape, fix),
                  pl.BlockSpec(glu_w.shape, fix),
                  pl.BlockSpec((1, glu_w.shape[1]), fix),
                  vec, vec, vec, vec, vec],
        out_specs=[pl.BlockSpec((tm, d), row), pl.BlockSpec((tm, d), row)],
        out_shape=[jax.ShapeDtypeStruct((l, d), F32), jax.ShapeDtypeStruct((l, d), BF16)],
        compiler_params=_cparams(("arbitrary",)),
        name="out_proj",
    )(attn, yf, yb, four, x, w_out, glu_w, glu_b, npost, gate, npre, shift, scale)


def _ffn_kernel(nj1, nj2, th, tn, h_ref, x_ref, wg_ref, wu_ref, wd_ref, npost_ref, gate_ref, o_ref,
                act_ref, tmp_ref):
    j = pl.program_id(1)

    @pl.when(j < nj1)
    def _up():
        h = h_ref[...]
        a = jnp.dot(h, wg_ref[...], preferred_element_type=F32)
        u = jnp.dot(h, wu_ref[...], preferred_element_type=F32)
        tmp_ref[...] = (a * jax.nn.sigmoid(a) * u).astype(BF16)

    for c in range(nj1):
        @pl.when(j == c)
        def _place(c=c):
            act_ref[:, c * th:(c + 1) * th] = tmp_ref[...]

    for c in range(nj2):
        @pl.when(j == nj1 + c)
        def _down(c=c):
            o_ref[:, c * tn:(c + 1) * tn] = jnp.dot(act_ref[...], wd_ref[...],
                                                    preferred_element_type=F32)

    @pl.when(j == nj1 + nj2 - 1)
    def _finish():
        o_ref[...] = x_ref[...] + gate_ref[...] * _rms(o_ref[...], npost_ref[...])


def _ffn_call(h, x, w_gate, w_up, w_down, npost, gate):
    l, d = x.shape
    fh = w_gate.shape[1]
    tm = min(512, l)
    th = 512
    tn = 512
    nj1 = fh // th
    nj2 = d // tn
    vec = pl.BlockSpec((1, d), lambda i, j: (0, 0))
    up = pl.BlockSpec((d, th), lambda i, j: (0, jnp.minimum(j, nj1 - 1)))
    return pl.pallas_call(
        functools.partial(_ffn_kernel, nj1, nj2, th, tn),
        grid=(l // tm, nj1 + nj2),
        in_specs=[pl.BlockSpec((tm, d), lambda i, j: (i, 0)),
                  pl.BlockSpec((tm, d), lambda i, j: (i, 0)),
                  up, up,
                  pl.BlockSpec((fh, tn), lambda i, j: (0, jnp.maximum(j - nj1, 0))),
                  vec, vec],
        out_specs=pl.BlockSpec((tm, d), lambda i, j: (i, 0)),
        out_shape=jax.ShapeDtypeStruct((l, d), F32),
        scratch_shapes=[pltpu.VMEM((tm, fh), BF16), pltpu.VMEM((tm, th), BF16)],
        compiler_params=_cparams(("arbitrary", "arbitrary")),
        name="ffn",
    )(h, x, w_gate, w_up, w_down, npost, gate)


def _rope_tables(l):
    t = np.arange(l)
    row = (t // GRID_W).astype(np.float32)
    col = (t % GRID_W).astype(np.float32)
    freqs = np.float32(ROPE_THETA) ** (-np.arange(ROPE_PAIRS, dtype=np.float32) / np.float32(ROPE_PAIRS))
    ang_r = (row[:, None] * freqs).astype(np.float32)
    ang_c = (col[:, None] * freqs).astype(np.float32)
    cos = np.concatenate([np.cos(ang_r)] * 2 + [np.cos(ang_c)] * 2, axis=-1)
    sin = np.concatenate([-np.sin(ang_r), np.sin(ang_r), -np.sin(ang_c), np.sin(ang_c)], axis=-1)
    return jnp.asarray(cos, dtype=F32), jnp.asarray(sin, dtype=F32)


def _channel_dft_table(fw):
    hw = FOURIER_HEAD_DIM
    ang = 2.0 * np.pi * ((np.arange(hw)[:, None] * np.arange(hw)[None, :]) % hw) / hw
    eye = np.eye(fw // hw)
    return jnp.asarray(np.concatenate([np.kron(eye, np.cos(ang)), np.kron(eye, np.sin(ang))], axis=1),
                       dtype=BF16)


def _dft_split(l):
    n1 = 1
    while n1 * n1 < l:
        n1 *= 2
    return n1 if (l >= 1024 and n1 * n1 == l) else 1


def kernel(x, c, ctx, c_ctx, ada_w, ada_b, norm_mix_pre, norm_mix_post, norm_ffn_pre, norm_ffn_post, w_in, q_norm, k_norm, ssm_lam_re, ssm_lam_im, ssm_log_dt, ssm_b_re, ssm_b_im, ssm_c_re, ssm_c_im, ssm_d, ssm_glu_w, ssm_glu_b, fourier_w, w_out, ffn_w_gate, ffn_w_up, ffn_w_down):
    depth = ada_w.shape[0]
    _, l, d = x.shape
    n_ctx = ctx.shape[1]
    fw = fourier_w.shape[1]
    sw = ssm_d.shape[1]

    cond = jnp.zeros((8, d), F32).at[0].set(c[0]).at[1].set(c_ctx)
    mod = _ada_call(cond, ada_w, ada_b)

    cos, sin = _rope_tables(l)
    zero_tab = jnp.zeros((n_ctx, HEAD_DIM), F32)
    cs = _channel_dft_table(fw)
    n1 = _dft_split(l)
    n1c = _dft_split(n_ctx)
    dft_x = _dft_tables(l, n1, FOURIER_HEAD_DIM)
    dft_c = _dft_tables(n_ctx, n1c, FOURIER_HEAD_DIM)
    h0 = jnp.zeros((sw // SSM_LANES, 1, 4 * (SSM_LANES // SSM_GROUP) * SSM_STATE), F32)

    xs = x[0]
    xc = ctx[0]
    for layer in range(depth):
        need_ctx = layer < depth - 1
        vecs = lambda r: [mod[layer, r:r + 1, i * d:(i + 1) * d] for i in range(6)]
        sh_m, sc_m, g_m, sh_f, sc_f, g_f = vecs(0)
        shc_m, scc_m, gc_m, shc_f, scc_f, gc_f = vecs(1)
        row = lambda a: a[layer].reshape(1, -1)

        w_in_l = w_in[layer].astype(BF16)
        w_out_l = w_out[layer].astype(BF16)
        glu_w_l = ssm_glu_w[layer].astype(BF16)
        four_w_l = fourier_w[layer].astype(BF16)
        wg_l = ffn_w_gate[layer].astype(BF16)
        wu_l = ffn_w_up[layer].astype(BF16)
        wd_l = ffn_w_down[layer].astype(BF16)
        tables = _ssm_tables(ssm_lam_re[layer], ssm_lam_im[layer], ssm_log_dt[layer],
                             ssm_b_re[layer], ssm_b_im[layer], ssm_c_re[layer], ssm_c_im[layer],
                             ssm_d[layer])

        qc, kc, vtc, usc, abc = _inproj_call(xc, row(norm_mix_pre), shc_m, scc_m, w_in_l,
                                             row(q_norm), row(k_norm), zero_tab, zero_tab, cs, False)
        yfc, ybc, hend_c = _ssm_apply(usc, tables, h0)

        q, k, vt, us, ab = _inproj_call(xs, row(norm_mix_pre), sh_m, sc_m, w_in_l,
                                        row(q_norm), row(k_norm), cos, sin, cs, True)
        attn = _attn_call(q, jnp.concatenate([k, kc], axis=0), jnp.concatenate([vt, vtc], axis=1))
        yf, yb, _ = _ssm_apply(us, tables, hend_c)
        four = _fourier_apply(ab, dft_x, four_w_l, n1)
        xs, hs = _outproj_call(attn, yf, yb, four, xs, w_out_l, glu_w_l, row(ssm_glu_b),
                               row(norm_mix_post), g_m, row(norm_ffn_pre), sh_f, sc_f)
        xs = _ffn_call(hs, xs, wg_l, wu_l, wd_l, row(norm_ffn_post), g_f)

        if need_ctx:
            attn_c = _attn_call(qc, kc, vtc)
            four_c = _fourier_apply(abc, dft_c, four_w_l, n1c)
            xc, hc = _outproj_call(attn_c, yfc, ybc, four_c, xc, w_out_l, glu_w_l, row(ssm_glu_b),
                                   row(norm_mix_post), gc_m, row(norm_ffn_pre), shc_f, scc_f)
            xc = _ffn_call(hc, xc, wg_l, wu_l, wd_l, row(norm_ffn_post), gc_f)
    return xs[None]
```

```python
import functools
import math

import numpy as np
import jax
import jax.numpy as jnp
from jax import lax
from jax.experimental import pallas as pl
from jax.experimental.pallas import tpu as pltpu

F32 = jnp.float32
BF16 = jnp.bfloat16

HEAD_DIM = 128
N_KV_HEADS = 2
Q_PER_KV = 4
GRID_W = 64
ROPE_THETA = 10000.0
ROPE_PAIRS = HEAD_DIM // 4
SSM_GROUP = 16
SSM_STATE = 64
FOURIER_HEAD_DIM = 128
NORM_EPS = 1e-6
SSM_CHUNK = 8
SSM_LANES = 128
LOG2E = 1.4426950408889634
ATTN_TQ = 512
ATTN_TK = 1280
ATTN_SUB = 256
FFN_TH = 512
FFN_TN = 512
VT_PAD = 16
VT_ROWS = HEAD_DIM + VT_PAD

VMEM_LIMIT_MB = 56


def _cparams(semantics, vmem_mb=VMEM_LIMIT_MB):
    return pltpu.CompilerParams(dimension_semantics=semantics,
                                vmem_limit_bytes=vmem_mb * 2 ** 20)


def _rms(x, gain):
    return x * lax.rsqrt(jnp.mean(x * x, axis=-1, keepdims=True) + NORM_EPS) * gain


def _ada_kernel(s_ref, w_ref, b_ref, o_ref):
    s = s_ref[...]
    act = (s * jax.nn.sigmoid(s)).astype(BF16)
    o_ref[0] = jnp.dot(act, w_ref[0].astype(BF16), preferred_element_type=F32) + b_ref[0]


def _ada_call(cond, ada_w, ada_b):
    depth, d, n = ada_w.shape
    tn = 1024
    return pl.pallas_call(
        _ada_kernel,
        grid=(depth, n // tn),
        in_specs=[pl.BlockSpec((8, d), lambda l, j: (0, 0)),
                  pl.BlockSpec((1, d, tn), lambda l, j: (l, 0, j)),
                  pl.BlockSpec((1, 1, tn), lambda l, j: (l, 0, j))],
        out_specs=pl.BlockSpec((1, 8, tn), lambda l, j: (l, 0, j)),
        out_shape=jax.ShapeDtypeStruct((depth, 8, n), F32),
        compiler_params=_cparams(("arbitrary", "arbitrary")),
        name="ada_ln",
    )(cond, ada_w, ada_b.reshape(depth, 1, n))


def _inproj_kernel(rope, x_ref, g_ref, sh_ref, sc_ref, w_ref, qn_ref, kn_ref, cos_ref, sin_ref,
                   cs_ref, q_ref, k_ref, vt_ref, us_ref, ab_ref):
    x = x_ref[...]
    h = (_rms(x, g_ref[...]) * (1.0 + sc_ref[...]) + sh_ref[...]).astype(BF16)
    n_q = q_ref.shape[1] // HEAD_DIM
    n_k = k_ref.shape[1] // HEAD_DIM
    q_end = n_q * HEAD_DIM
    k_end = q_end + n_k * HEAD_DIM
    v_end = k_end + n_k * HEAD_DIM
    s_end = v_end + us_ref.shape[1]
    f_end = s_end + ab_ref.shape[2]

    if rope:
        cos = cos_ref[...]
        sin = sin_ref[...]
        lane = lax.broadcasted_iota(jnp.int32, cos.shape, 1)
        low = (lane % 64) < 32

    def head(t, gain, scale):
        t = _rms(t, gain)
        if rope:
            sw = jnp.where(low, pltpu.roll(t, HEAD_DIM - 32, 1), pltpu.roll(t, 32, 1))
            t = t * cos + sw * sin
        if scale != 1.0:
            t = t * scale
        return t.astype(BF16)

    qk = jnp.dot(h, w_ref[0, :, 0:k_end], preferred_element_type=F32)
    q_scale = HEAD_DIM ** -0.5 * LOG2E
    for i in range(n_q):
        q_ref[:, i * HEAD_DIM:(i + 1) * HEAD_DIM] = head(
            qk[:, i * HEAD_DIM:(i + 1) * HEAD_DIM], qn_ref[...], q_scale)
    for i in range(n_k):
        k_ref[:, i * HEAD_DIM:(i + 1) * HEAD_DIM] = head(
            qk[:, q_end + i * HEAD_DIM:q_end + (i + 1) * HEAD_DIM], kn_ref[...], 1.0)
    rest = jnp.dot(h, w_ref[0, :, k_end:f_end], preferred_element_type=F32)
    tm = x.shape[0]
    ones_rows = (lax.broadcasted_iota(jnp.int32, (VT_PAD, tm), 0) == 0).astype(BF16)
    for i in range(n_k):
        base = i * VT_ROWS
        vt_ref[base:base + HEAD_DIM, :] = rest[:, i * HEAD_DIM:(i + 1) * HEAD_DIM].T.astype(BF16)
        vt_ref[base + HEAD_DIM:base + VT_ROWS, :] = ones_rows
    us_ref[...] = rest[:, v_end - k_end:s_end - k_end]
    uf = rest[:, s_end - k_end:f_end - k_end].astype(BF16)
    ab = jnp.dot(uf, cs_ref[...], preferred_element_type=F32)
    fw = ab_ref.shape[2]
    ab_ref[0] = ab[:, 0:fw].astype(BF16)
    ab_ref[1] = ab[:, fw:2 * fw].astype(BF16)


def _layer_block(w, layer):
    return pl.BlockSpec((1,) + w.shape[1:], lambda *_: (layer, 0, 0))


def _inproj_call(x, gain, shift, scale, w_in, layer, qn, kn, cos, sin, cs, rope):
    l, d = x.shape
    tm = min(512, l)
    qw = Q_PER_KV * N_KV_HEADS * HEAD_DIM
    kw = N_KV_HEADS * HEAD_DIM
    fw = cs.shape[0]
    sw = w_in.shape[2] - qw - 2 * kw - fw
    row = lambda i: (i, 0)
    fix = lambda i: (0, 0)
    return pl.pallas_call(
        functools.partial(_inproj_kernel, rope),
        grid=(l // tm,),
        in_specs=[pl.BlockSpec((tm, d), row),
                  pl.BlockSpec((1, d), fix), pl.BlockSpec((1, d), fix), pl.BlockSpec((1, d), fix),
                  _layer_block(w_in, layer),
                  pl.BlockSpec((1, HEAD_DIM), fix), pl.BlockSpec((1, HEAD_DIM), fix),
                  pl.BlockSpec((tm, HEAD_DIM), row), pl.BlockSpec((tm, HEAD_DIM), row),
                  pl.BlockSpec(cs.shape, fix)],
        out_specs=[pl.BlockSpec((tm, qw), row),
                   pl.BlockSpec((tm, kw), row),
                   pl.BlockSpec((N_KV_HEADS * VT_ROWS, tm), lambda i: (0, i)),
                   pl.BlockSpec((tm, sw), row),
                   pl.BlockSpec((2, tm, fw), lambda i: (0, i, 0))],
        out_shape=[jax.ShapeDtypeStruct((l, qw), BF16),
                   jax.ShapeDtypeStruct((l, kw), BF16),
                   jax.ShapeDtypeStruct((N_KV_HEADS * VT_ROWS, l), BF16),
                   jax.ShapeDtypeStruct((l, sw), F32),
                   jax.ShapeDtypeStruct((2, l, fw), BF16)],
        compiler_params=_cparams(("arbitrary",)),
        name="in_proj",
    )(x, gain, shift, scale, w_in, qn, kn, cos, sin, cs)


def _attn_kernel(nkb, tq, q_ref, k_ref, vt_ref, o_ref, qs_ref, s0_ref, s1_ref, bm0_ref, bm1_ref,
                 m_ref, acc_ref):
    j = pl.program_id(2)
    s_refs = (s0_ref, s1_ref)
    bm_refs = (bm0_ref, bm1_ref)

    tk = k_ref.shape[0]
    sub = min(ATTN_SUB, tk)

    def step(score_slot, absorb_slot):
        if absorb_slot is not None:
            m_prev = m_ref[...]
            m_new = jnp.maximum(m_prev, bm_refs[absorb_slot][...])
            alpha = jnp.exp2(m_prev - m_new)
            pv = jnp.zeros(acc_ref.shape, F32)
        bmax = None
        for c in range(tk // sub):
            rows = slice(c * sub, (c + 1) * sub)
            if score_slot is not None:
                s = lax.dot_general(k_ref[rows, :], qs_ref[...], (((1,), (1,)), ((), ())),
                                    preferred_element_type=F32)
                s_refs[score_slot][rows, :] = s
                cmax = jnp.max(s, axis=0, keepdims=True)
                bmax = cmax if bmax is None else jnp.maximum(bmax, cmax)
            if absorb_slot is not None:
                p = jnp.exp2((s_refs[absorb_slot][rows, :] - m_new).astype(BF16))
                pv = pv + jnp.dot(vt_ref[:, rows], p, preferred_element_type=F32)
        if score_slot is not None:
            bm_refs[score_slot][...] = bmax
        if absorb_slot is not None:
            acc_ref[...] = alpha * acc_ref[...] + pv
            m_ref[...] = m_new

    @pl.when(j == 0)
    def _first():
        for h in range(Q_PER_KV):
            qs_ref[h * tq:(h + 1) * tq, :] = q_ref[:, h * HEAD_DIM:(h + 1) * HEAD_DIM]
        m_ref[...] = jnp.full(m_ref.shape, -jnp.inf, F32)
        acc_ref[...] = jnp.zeros(acc_ref.shape, F32)
        step(0, None)

    for parity in (0, 1):
        @pl.when((j > 0) & (j < nkb) & (j % 2 == parity))
        def _mid(parity=parity):
            step(parity, 1 - parity)

    @pl.when(j == nkb)
    def _last():
        step(None, (nkb - 1) % 2)
        o = acc_ref[0:HEAD_DIM, :] / acc_ref[HEAD_DIM:HEAD_DIM + 1, :]
        for h in range(Q_PER_KV):
            o_ref[:, h * HEAD_DIM:(h + 1) * HEAD_DIM] = o[:, h * tq:(h + 1) * tq].T.astype(BF16)


def _pick_block(total, cap):
    best = 128
    for b in range(128, cap + 1, 128):
        if total % b == 0:
            best = b
    return best


def _attn_call(q, k_all, vt_all):
    l = q.shape[0]
    s_len = k_all.shape[0]
    tq = min(ATTN_TQ, l)
    tk = _pick_block(s_len, ATTN_TK)
    nkb = s_len // tk
    gw = Q_PER_KV * HEAD_DIM
    n = Q_PER_KV * tq
    return pl.pallas_call(
        functools.partial(_attn_kernel, nkb, tq),
        grid=(N_KV_HEADS, l // tq, nkb + 1),
        in_specs=[pl.BlockSpec((tq, gw), lambda g, i, j: (i, g)),
                  pl.BlockSpec((tk, HEAD_DIM), lambda g, i, j: (jnp.minimum(j, nkb - 1), g)),
                  pl.BlockSpec((VT_ROWS, tk), lambda g, i, j: (g, jnp.maximum(j - 1, 0)))],
        out_specs=pl.BlockSpec((tq, gw), lambda g, i, j: (i, g)),
        out_shape=jax.ShapeDtypeStruct(q.shape, BF16),
        scratch_shapes=[pltpu.VMEM((n, HEAD_DIM), BF16),
                        pltpu.VMEM((tk, n), F32),
                        pltpu.VMEM((tk, n), F32),
                        pltpu.VMEM((1, n), F32),
                        pltpu.VMEM((1, n), F32),
                        pltpu.VMEM((1, n), F32),
                        pltpu.VMEM((VT_ROWS, n), F32)],
        compiler_params=_cparams(("arbitrary", "arbitrary", "arbitrary")),
        name="attention",
    )(q, k_all, vt_all)


def _ssm_tables(lam_re, lam_im, log_dt, b_re, b_im, c_re, c_im, d):
    t = SSM_CHUNK
    g, p = lam_re.shape[1:]
    hdim = b_re.shape[-1]
    lr = lam_re.astype(F32)
    li = lam_im.astype(F32)
    dt = jnp.exp(log_dt.astype(F32))[..., None]
    mag = jnp.exp(lr * dt)
    ar = mag * jnp.cos(li * dt)
    ai = mag * jnp.sin(li * dt)
    den = lr * lr + li * li
    cr = ((ar - 1.0) * lr + ai * li) / den
    ci = (ai * lr - (ar - 1.0) * li) / den
    br = b_re.astype(F32)
    bi = b_im.astype(F32)
    bbr = cr[..., None] * br - ci[..., None] * bi
    bbi = cr[..., None] * bi + ci[..., None] * br
    n = jnp.arange(t + 1, dtype=F32)[:, None, None, None]
    pmag = jnp.exp(n * (lr * dt)[None])
    pr = pmag * jnp.cos(n * (li * dt)[None])
    pi = pmag * jnp.sin(n * (li * dt)[None])
    cre = c_re.astype(F32)
    cim = c_im.astype(F32)

    gs = SSM_LANES // hdim
    n_sg = g // gs

    def selector(r, inner):
        e = np.zeros((gs, r, gs, inner, r, inner), np.float32)
        for grp in range(gs):
            e[grp, :, grp] = np.eye(r * inner, dtype=np.float32).reshape(r, inner, r, inner)
        return jnp.asarray(e.reshape(gs, r * gs * inner, r * inner), dtype=BF16)

    def expand(small):
        _, r1, i1, r2, i2 = small.shape
        blocks = small.reshape(n_sg, gs, r1 * i1, r2 * i2).astype(BF16)
        left = jnp.einsum('grk,sgkl->sgrl', selector(r1, i1), blocks,
                          preferred_element_type=F32).astype(BF16)
        return jnp.einsum('sgrl,gcl->src', left, selector(r2, i2),
                          preferred_element_type=F32).astype(BF16)

    def summ(direction, powers):
        er = pr[powers, direction][..., None] * bbr[direction][None] \
            - pi[powers, direction][..., None] * bbi[direction][None]
        ei = pr[powers, direction][..., None] * bbi[direction][None] \
            + pi[powers, direction][..., None] * bbr[direction][None]
        tr = lambda e: jnp.transpose(e, (1, 0, 3, 2))
        return expand(jnp.stack([tr(er), tr(ei)], axis=3))
    steps = jnp.arange(t)
    p_f = summ(0, t - 1 - steps)
    p_b = summ(1, steps)

    row = lambda v: v.reshape(n_sg, gs * p)
    a_row = jnp.concatenate([row(pr[t, 0]), row(pr[t, 1]), row(pi[t, 0]), row(pi[t, 1])],
                            axis=-1)[:, None, :]

    def kern(direction):
        er = pr[:t, direction][..., None] * bbr[direction][None] \
            - pi[:t, direction][..., None] * bbi[direction][None]
        ei = pr[:t, direction][..., None] * bbi[direction][None] \
            + pi[:t, direction][..., None] * bbr[direction][None]
        return (jnp.einsum('gop,ngpi->ngoi', cre[direction], er)
                - jnp.einsum('gop,ngpi->ngoi', cim[direction], ei))
    kf = kern(0)
    kb = kern(1)
    s_idx = steps[:, None]
    t_idx = steps[None, :]
    lag_f = jnp.clip(t_idx - s_idx, 0, t - 1)
    lag_b = jnp.clip(s_idx - t_idx, 0, t - 1)
    mf = jnp.where((t_idx >= s_idx)[:, :, None, None, None], kf[lag_f], 0.0)
    mb = jnp.where((s_idx >= t_idx)[:, :, None, None, None], kb[lag_b], 0.0)
    dd = d.astype(F32).reshape(g, hdim)
    eye_t = jnp.eye(t, dtype=F32)[:, :, None, None, None]
    eye_c = jnp.eye(hdim, dtype=F32)[None, None, None]
    m5 = mf + mb + eye_t * eye_c * dd[None, None, :, :, None]
    m_mat = expand(jnp.transpose(m5, (2, 0, 4, 1, 3)))

    def state_out(direction, powers):
        prn = pr[powers, direction]
        pin = pi[powers, direction]
        wr = cre[direction][None] * prn[:, :, None, :] - cim[direction][None] * pin[:, :, None, :]
        wi = -cre[direction][None] * pin[:, :, None, :] - cim[direction][None] * prn[:, :, None, :]
        tr = lambda w: jnp.transpose(w, (1, 3, 0, 2))
        return expand(jnp.stack([tr(wr), tr(wi)], axis=1))
    q_f = state_out(0, steps + 1)
    q_b = state_out(1, t - steps)
    return p_f, p_b, m_mat, q_f, q_b, a_row


def _ssm_kernel(tbk, usf_ref, usb_ref, pf_ref, pb_ref, m_ref, qf_ref, qb_ref, a_ref, h0_ref,
                yf_ref, yb_ref, hend_ref, xf_ref, xb_ref, sf_ref, sb_ref, hf_ref, hb_ref, h_ref):
    t = SSM_CHUNK
    lanes = SSM_LANES

    @pl.when(pl.program_id(1) == 0)
    def _init():
        h_ref[...] = h0_ref[0]

    for tt in range(t):
        cols = slice(tt * lanes, (tt + 1) * lanes)
        xf_ref[:, cols] = usf_ref[pl.ds(tt, tbk, stride=t), :].astype(BF16)
        xb_ref[:, cols] = usb_ref[pl.ds(tt, tbk, stride=t), :].astype(BF16)
    sf_ref[...] = jnp.dot(xf_ref[...], pf_ref[0], preferred_element_type=F32)
    sb_ref[...] = jnp.dot(xb_ref[...], pb_ref[0], preferred_element_type=F32)

    w = a_ref.shape[2] // 4
    a = a_ref[0]
    arf, arb, aif, aib = (a[:, i * w:(i + 1) * w] for i in range(4))
    h = h_ref[...]

    def body(k, carry):
        hrf, hrb, hif, hib = carry
        kb = tbk - 1 - k
        hf_ref[pl.ds(k, 1), 0:w] = hrf
        hf_ref[pl.ds(k, 1), w:2 * w] = hif
        hb_ref[pl.ds(kb, 1), 0:w] = hrb
        hb_ref[pl.ds(kb, 1), w:2 * w] = hib
        sf = sf_ref[pl.ds(k, 1), :]
        sb = sb_ref[pl.ds(kb, 1), :]
        return (arf * hrf - aif * hif + sf[:, 0:w], arb * hrb - aib * hib + sb[:, 0:w],
                arf * hif + aif * hrf + sf[:, w:2 * w], arb * hib + aib * hrb + sb[:, w:2 * w])

    carry = lax.fori_loop(0, tbk, body, tuple(h[:, i * w:(i + 1) * w] for i in range(4)),
                          unroll=8)
    h = jnp.concatenate(carry, axis=1)
    h_ref[...] = h
    hend_ref[0] = h

    yf = (jnp.dot(xf_ref[...], m_ref[0], preferred_element_type=F32)
          + jnp.dot(hf_ref[...].astype(BF16), qf_ref[0], preferred_element_type=F32))
    yb = jnp.dot(hb_ref[...].astype(BF16), qb_ref[0], preferred_element_type=F32)
    for tt in range(t):
        cols = slice(tt * lanes, (tt + 1) * lanes)
        yf_ref[pl.ds(tt, tbk, stride=t), :] = yf[:, cols]
        yb_ref[pl.ds(tt, tbk, stride=t), :] = yb[:, cols]


def _ssm_apply(us, tables, h0):
    p_f, p_b, m_mat, q_f, q_b, a_row = tables
    l, w = us.shape
    n_sg = w // SSM_LANES
    n = l // SSM_CHUNK
    tbk = min(256, n)
    nb = n // tbk
    rows = tbk * SSM_CHUNK
    kx = SSM_CHUNK * SSM_LANES
    sw = p_f.shape[2]
    hw = a_row.shape[2]
    per_sg = lambda a: pl.BlockSpec((1,) + a.shape[1:], lambda s, i: (s, 0, 0))
    fwd = pl.BlockSpec((rows, SSM_LANES), lambda s, i: (i, s))
    bwd = pl.BlockSpec((rows, SSM_LANES), lambda s, i: (nb - 1 - i, s))
    return pl.pallas_call(
        functools.partial(_ssm_kernel, tbk),
        grid=(n_sg, nb),
        in_specs=[fwd, bwd, per_sg(p_f), per_sg(p_b), per_sg(m_mat), per_sg(q_f), per_sg(q_b),
                  per_sg(a_row), per_sg(h0)],
        out_specs=[fwd, bwd, per_sg(h0)],
        out_shape=[jax.ShapeDtypeStruct((l, w), F32), jax.ShapeDtypeStruct((l, w), F32),
                   jax.ShapeDtypeStruct(h0.shape, F32)],
        scratch_shapes=[pltpu.VMEM((tbk, kx), BF16), pltpu.VMEM((tbk, kx), BF16),
                        pltpu.VMEM((tbk, sw), F32), pltpu.VMEM((tbk, sw), F32),
                        pltpu.VMEM((tbk, sw), F32), pltpu.VMEM((tbk, sw), F32),
                        pltpu.VMEM((1, hw), F32)],
        compiler_params=_cparams(("arbitrary", "arbitrary")),
        name="ssm_scan",
    )(us, us, p_f, p_b, m_mat, q_f, q_b, a_row, h0)


def _dft_tables(l, n1, hw):
    n2 = l // n1
    scale = 1.0 / math.sqrt(l * hw)
    if n1 == 1:
        t1 = None
    else:
        ang = 2.0 * np.pi * ((np.arange(n1)[:, None] * np.arange(n1)[None, :]) % n1) / n1
        c, s = np.cos(ang), np.sin(ang)
        t1 = jnp.asarray(np.block([[c, -s], [-s, -c]]), dtype=BF16)
    k1 = np.arange(n1)[:, None, None]
    k2 = np.arange(n2)[None, :, None]
    m = np.arange(n2)[None, None, :]
    ang = 2.0 * np.pi * ((m * (n1 * k2 + k1)) % l) / l
    sign = -1.0 if n1 == 1 else 1.0
    g = np.concatenate([np.cos(ang), sign * np.sin(ang)], axis=-1) * scale
    return t1, jnp.asarray(g, dtype=BF16)


def _dft1_kernel(t_ref, x_ref, z_ref):
    z_ref[...] = jnp.dot(t_ref[...], x_ref[...], preferred_element_type=F32).astype(z_ref.dtype)


def _dft1_call(t1, ab2d):
    r, n = ab2d.shape
    tn = min(4096, n)
    return pl.pallas_call(
        _dft1_kernel,
        grid=(n // tn,),
        in_specs=[pl.BlockSpec((r, r), lambda j: (0, 0)),
                  pl.BlockSpec((r, tn), lambda j: (0, j))],
        out_specs=pl.BlockSpec((r, tn), lambda j: (0, j)),
        out_shape=jax.ShapeDtypeStruct((r, n), BF16),
        compiler_params=_cparams(("arbitrary",)),
        name="dft_stage1",
    )(t1, ab2d)


def _dft2_kernel(bt, zr_ref, zi_ref, g_ref, w_ref, o_ref):
    fw = w_ref.shape[1]
    for b in range(bt):
        z = jnp.concatenate([zr_ref[b], zi_ref[b]], axis=0)
        x = jnp.dot(g_ref[b], z, preferred_element_type=F32)
        o_ref[:, b * fw:(b + 1) * fw] = jnp.dot(
            x.astype(BF16), w_ref[0], preferred_element_type=F32).astype(o_ref.dtype)


def _dft2_call(z3, g, fourier_w, layer):
    n1x2, n2, fw = z3.shape
    n1 = n1x2 // 2
    bt = min(8, n1)
    return pl.pallas_call(
        functools.partial(_dft2_kernel, bt),
        grid=(n1 // bt,),
        in_specs=[pl.BlockSpec((bt, n2, fw), lambda b: (b, 0, 0)),
                  pl.BlockSpec((bt, n2, fw), lambda b: (b + n1 // bt, 0, 0)),
                  pl.BlockSpec((bt, n2, 2 * n2), lambda b: (b, 0, 0)),
                  _layer_block(fourier_w, layer)],
        out_specs=pl.BlockSpec((n2, bt * fw), lambda b: (0, b)),
        out_shape=jax.ShapeDtypeStruct((n2, n1 * fw), BF16),
        compiler_params=_cparams(("arbitrary",)),
        name="dft_stage2",
    )(z3, z3, g, fourier_w)


def _fourier_apply(ab, tables, fourier_w, layer, n1):
    t1, g = tables
    _, l, fw = ab.shape
    n2 = l // n1
    if n1 == 1:
        z3 = ab
    else:
        z3 = _dft1_call(t1, ab.reshape(2 * n1, n2 * fw)).reshape(2 * n1, n2, fw)
    return _dft2_call(z3, g, fourier_w, layer).reshape(l, fw)


def _outproj_kernel(attn_ref, yf_ref, yb_ref, four_ref, x_ref, wo_ref, gw_ref, gb_ref, npost_ref,
                    gate_ref, npre_ref, sh_ref, sc_ref, xo_ref, h_ref):
    ys = yf_ref[...] + yb_ref[...]
    gl = 0.5 * ys * (1.0 + jnp.tanh(math.sqrt(2.0 / math.pi) * (ys + 0.044715 * (ys * ys * ys))))
    z = jnp.dot(gl.astype(BF16), gw_ref[0], preferred_element_type=F32) + gb_ref[...]
    ssm = (gl * jax.nn.sigmoid(z)).astype(BF16)
    cat = jnp.concatenate([attn_ref[...], ssm, four_ref[...]], axis=-1)
    y = jnp.dot(cat, wo_ref[0], preferred_element_type=F32)
    xn = x_ref[...] + gate_ref[...] * _rms(y, npost_ref[...])
    xo_ref[...] = xn
    h_ref[...] = (_rms(xn, npre_ref[...]) * (1.0 + sc_ref[...]) + sh_ref[...]).astype(BF16)


def _outproj_call(attn, yf, yb, four, x, w_out, glu_w, layer, glu_b, npost, gate, npre, shift,
                  scale):
    l, d = x.shape
    tm = min(512, l)
    row = lambda i: (i, 0)
    fix = lambda i: (0, 0)
    vec = pl.BlockSpec((1, d), fix)
    return pl.pallas_call(
        _outproj_kernel,
        grid=(l // tm,),
        in_specs=[pl.BlockSpec((tm, attn.shape[1]), row),
                  pl.BlockSpec((tm, yf.shape[1]), row),
                  pl.BlockSpec((tm, yb.shape[1]), row),
                  pl.BlockSpec((tm, four.shape[1]), row),
                  pl.BlockSpec((tm, d), row),
                  _layer_block(w_out, layer),
                  _layer_block(glu_w, layer),
                  pl.BlockSpec((1, glu_w.shape[2]), fix),
                  vec, vec, vec, vec, vec],
        out_specs=[pl.BlockSpec((tm, d), row), pl.BlockSpec((tm, d), row)],
        out_shape=[jax.ShapeDtypeStruct((l, d), F32), jax.ShapeDtypeStruct((l, d), BF16)],
        compiler_params=_cparams(("arbitrary",)),
        name="out_proj",
    )(attn, yf, yb, four, x, w_out, glu_w, glu_b, npost, gate, npre, shift, scale)


def _ffn_kernel(nj1, nj2, th, tn, h_ref, x_ref, wg_ref, wu_ref, wd_ref, npost_ref, gate_ref, o_ref,
                act_ref, tmp_ref):
    j = pl.program_id(1)

    @pl.when(j < nj1)
    def _up():
        h = h_ref[...]
        a = jnp.dot(h, wg_ref[0, 0], preferred_element_type=F32)
        u = jnp.dot(h, wu_ref[0, 0], preferred_element_type=F32)
        tmp_ref[...] = (a * jax.nn.sigmoid(a) * u).astype(BF16)

    for c in range(nj1):
        @pl.when(j == c)
        def _place(c=c):
            act_ref[:, c * th:(c + 1) * th] = tmp_ref[...]

    for c in range(nj2):
        @pl.when(j == nj1 + c)
        def _down(c=c):
            o_ref[:, c * tn:(c + 1) * tn] = jnp.dot(act_ref[...], wd_ref[0, 0],
                                                    preferred_element_type=F32)

    @pl.when(j == nj1 + nj2 - 1)
    def _finish():
        o_ref[...] = x_ref[...] + gate_ref[...] * _rms(o_ref[...], npost_ref[...])


def _ffn_blocked(w_gate, w_up, w_down):
    depth, d, fh = w_gate.shape
    up = lambda w: jnp.transpose(w.astype(BF16).reshape(depth, d, fh // FFN_TH, FFN_TH),
                                 (0, 2, 1, 3))
    down = jnp.transpose(w_down.astype(BF16).reshape(depth, fh, d // FFN_TN, FFN_TN), (0, 2, 1, 3))
    return up(w_gate), up(w_up), down


def _ffn_call(h, x, w_gate, w_up, w_down, layer, npost, gate):
    l, d = x.shape
    _, nj1, _, th = w_gate.shape
    _, nj2, fh, tn = w_down.shape
    tm = min(512, l)
    vec = pl.BlockSpec((1, d), lambda i, j: (0, 0))
    up = pl.BlockSpec((1, 1, d, th), lambda i, j: (layer, jnp.minimum(j, nj1 - 1), 0, 0))
    return pl.pallas_call(
        functools.partial(_ffn_kernel, nj1, nj2, th, tn),
        grid=(l // tm, nj1 + nj2),
        in_specs=[pl.BlockSpec((tm, d), lambda i, j: (i, 0)),
                  pl.BlockSpec((tm, d), lambda i, j: (i, 0)),
                  up, up,
                  pl.BlockSpec((1, 1, fh, tn),
                               lambda i, j: (layer, jnp.maximum(j - nj1, 0), 0, 0)),
                  vec, vec],
        out_specs=pl.BlockSpec((tm, d), lambda i, j: (i, 0)),
        out_shape=jax.ShapeDtypeStruct((l, d), F32),
        scratch_shapes=[pltpu.VMEM((tm, fh), BF16), pltpu.VMEM((tm, th), BF16)],
        compiler_params=_cparams(("arbitrary", "arbitrary")),
        name="ffn",
    )(h, x, w_gate, w_up, w_down, npost, gate)


def _rope_tables(l):
    t = np.arange(l)
    row = (t // GRID_W).astype(np.float32)
    col = (t % GRID_W).astype(np.float32)
    freqs = np.float32(ROPE_THETA) ** (-np.arange(ROPE_PAIRS, dtype=np.float32) / np.float32(ROPE_PAIRS))
    ang_r = (row[:, None] * freqs).astype(np.float32)
    ang_c = (col[:, None] * freqs).astype(np.float32)
    cos = np.concatenate([np.cos(ang_r)] * 2 + [np.cos(ang_c)] * 2, axis=-1)
    sin = np.concatenate([-np.sin(ang_r), np.sin(ang_r), -np.sin(ang_c), np.sin(ang_c)], axis=-1)
    return jnp.asarray(cos, dtype=F32), jnp.asarray(sin, dtype=F32)


def _channel_dft_table(fw):
    hw = FOURIER_HEAD_DIM
    ang = 2.0 * np.pi * ((np.arange(hw)[:, None] * np.arange(hw)[None, :]) % hw) / hw
    eye = np.eye(fw // hw)
    return jnp.asarray(np.concatenate([np.kron(eye, np.cos(ang)), np.kron(eye, np.sin(ang))], axis=1),
                       dtype=BF16)


def _dft_split(l):
    n1 = 1
    while n1 * n1 < l:
        n1 *= 2
    return n1 if (l >= 1024 and n1 * n1 == l) else 1


def kernel(x, c, ctx, c_ctx, ada_w, ada_b, norm_mix_pre, norm_mix_post, norm_ffn_pre, norm_ffn_post, w_in, q_norm, k_norm, ssm_lam_re, ssm_lam_im, ssm_log_dt, ssm_b_re, ssm_b_im, ssm_c_re, ssm_c_im, ssm_d, ssm_glu_w, ssm_glu_b, fourier_w, w_out, ffn_w_gate, ffn_w_up, ffn_w_down):
    depth = ada_w.shape[0]
    _, l, d = x.shape
    n_ctx = ctx.shape[1]
    fw = fourier_w.shape[1]
    sw = ssm_d.shape[1]

    cond = jnp.zeros((8, d), F32).at[0].set(c[0]).at[1].set(c_ctx)
    mod = _ada_call(cond, ada_w, ada_b)

    cos, sin = _rope_tables(l)
    zero_tab = jnp.zeros((n_ctx, HEAD_DIM), F32)
    cs = _channel_dft_table(fw)
    n1 = _dft_split(l)
    n1c = _dft_split(n_ctx)
    dft_x = _dft_tables(l, n1, FOURIER_HEAD_DIM)
    dft_c = _dft_tables(n_ctx, n1c, FOURIER_HEAD_DIM)
    h0 = jnp.zeros((sw // SSM_LANES, 1, 4 * (SSM_LANES // SSM_GROUP) * SSM_STATE), F32)

    w_in_b = w_in.astype(BF16)
    w_out_b = w_out.astype(BF16)
    glu_w_b = ssm_glu_w.astype(BF16)
    four_w_b = fourier_w.astype(BF16)
    wg_b, wu_b, wd_b = _ffn_blocked(ffn_w_gate, ffn_w_up, ffn_w_down)

    xs = x[0]
    xc = ctx[0]
    for layer in range(depth):
        need_ctx = layer < depth - 1
        vecs = lambda r: [mod[layer, r:r + 1, i * d:(i + 1) * d] for i in range(6)]
        sh_m, sc_m, g_m, sh_f, sc_f, g_f = vecs(0)
        shc_m, scc_m, gc_m, shc_f, scc_f, gc_f = vecs(1)
        row = lambda a: a[layer].reshape(1, -1)
        tables = _ssm_tables(ssm_lam_re[layer], ssm_lam_im[layer], ssm_log_dt[layer],
                             ssm_b_re[layer], ssm_b_im[layer], ssm_c_re[layer], ssm_c_im[layer],
                             ssm_d[layer])

        qc, kc, vtc, usc, abc = _inproj_call(xc, row(norm_mix_pre), shc_m, scc_m, w_in_b, layer,
                                             row(q_norm), row(k_norm), zero_tab, zero_tab, cs, False)
        yfc, ybc, hend_c = _ssm_apply(usc, tables, h0)

        q, k, vt, us, ab = _inproj_call(xs, row(norm_mix_pre), sh_m, sc_m, w_in_b, layer,
                                        row(q_norm), row(k_norm), cos, sin, cs, True)
        attn = _attn_call(q, jnp.concatenate([k, kc], axis=0), jnp.concatenate([vt, vtc], axis=1))
        yf, yb, _ = _ssm_apply(us, tables, hend_c)
        four = _fourier_apply(ab, dft_x, four_w_b, layer, n1)
        xs, hs = _outproj_call(attn, yf, yb, four, xs, w_out_b, glu_w_b, layer, row(ssm_glu_b),
                               row(norm_mix_post), g_m, row(norm_ffn_pre), sh_f, sc_f)
        xs = _ffn_call(hs, xs, wg_b, wu_b, wd_b, layer, row(norm_ffn_post), g_f)

        if need_ctx:
            attn_c = _attn_call(qc, kc, vtc)
            four_c = _fourier_apply(abc, dft_c, four_w_b, layer, n1c)
            xc, hc = _outproj_call(attn_c, yfc, ybc, four_c, xc, w_out_b, glu_w_b, layer,
                                   row(ssm_glu_b), row(norm_mix_post), gc_m, row(norm_ffn_pre),
                                   shc_f, scc_f)
            xc = _ffn_call(hc, xc, wg_b, wu_b, wd_b, layer, row(norm_ffn_post), gc_f)
    return xs[None]
```

```python
import functools
import math

import numpy as np
import jax
import jax.numpy as jnp
from jax import lax
from jax.experimental import pallas as pl
from jax.experimental.pallas import tpu as pltpu

F32 = jnp.float32
BF16 = jnp.bfloat16

HEAD_DIM = 128
N_KV_HEADS = 2
Q_PER_KV = 4
GRID_W = 64
ROPE_THETA = 10000.0
ROPE_PAIRS = HEAD_DIM // 4
SSM_GROUP = 16
SSM_STATE = 64
FOURIER_HEAD_DIM = 128
NORM_EPS = 1e-6
SSM_CHUNK = 8
SSM_LANES = 128
LOG2E = 1.4426950408889634
ATTN_TQ = 512
ATTN_TK = 1280
ATTN_SUB = 256
FFN_TM = 1024
FFN_TH = 512
FFN_TN = 256
VT_PAD = 16
VT_ROWS = HEAD_DIM + VT_PAD

VMEM_LIMIT_MB = 56


def _cparams(semantics, vmem_mb=VMEM_LIMIT_MB):
    return pltpu.CompilerParams(dimension_semantics=semantics,
                                vmem_limit_bytes=vmem_mb * 2 ** 20)


def _rms(x, gain):
    return x * lax.rsqrt(jnp.mean(x * x, axis=-1, keepdims=True) + NORM_EPS) * gain


def _ada_kernel(s_ref, w_ref, b_ref, o_ref):
    s = s_ref[...]
    act = (s * jax.nn.sigmoid(s)).astype(BF16)
    o_ref[0] = jnp.dot(act, w_ref[0].astype(BF16), preferred_element_type=F32) + b_ref[0]


def _ada_call(cond, ada_w, ada_b):
    depth, d, n = ada_w.shape
    tn = 1024
    return pl.pallas_call(
        _ada_kernel,
        grid=(depth, n // tn),
        in_specs=[pl.BlockSpec((8, d), lambda l, j: (0, 0)),
                  pl.BlockSpec((1, d, tn), lambda l, j: (l, 0, j)),
                  pl.BlockSpec((1, 1, tn), lambda l, j: (l, 0, j))],
        out_specs=pl.BlockSpec((1, 8, tn), lambda l, j: (l, 0, j)),
        out_shape=jax.ShapeDtypeStruct((depth, 8, n), F32),
        compiler_params=_cparams(("arbitrary", "arbitrary")),
        name="ada_ln",
    )(cond, ada_w, ada_b.reshape(depth, 1, n))


def _inproj_kernel(rope, x_ref, g_ref, sh_ref, sc_ref, w_ref, qn_ref, kn_ref, cos_ref, sin_ref,
                   cs_ref, q_ref, k_ref, vt_ref, us_ref, ab_ref):
    x = x_ref[...]
    h = (_rms(x, g_ref[...]) * (1.0 + sc_ref[...]) + sh_ref[...]).astype(BF16)
    n_q = q_ref.shape[1] // HEAD_DIM
    n_k = k_ref.shape[1] // HEAD_DIM
    q_end = n_q * HEAD_DIM
    k_end = q_end + n_k * HEAD_DIM
    v_end = k_end + n_k * HEAD_DIM
    s_end = v_end + us_ref.shape[1]
    f_end = s_end + ab_ref.shape[2]

    if rope:
        cos = cos_ref[...]
        sin = sin_ref[...]
        lane = lax.broadcasted_iota(jnp.int32, cos.shape, 1)
        low = (lane % 64) < 32

    def head(t, gain, scale):
        t = _rms(t, gain)
        if rope:
            sw = jnp.where(low, pltpu.roll(t, HEAD_DIM - 32, 1), pltpu.roll(t, 32, 1))
            t = t * cos + sw * sin
        if scale != 1.0:
            t = t * scale
        return t.astype(BF16)

    qk = jnp.dot(h, w_ref[0, :, 0:k_end], preferred_element_type=F32)
    q_scale = HEAD_DIM ** -0.5 * LOG2E
    for i in range(n_q):
        q_ref[:, i * HEAD_DIM:(i + 1) * HEAD_DIM] = head(
            qk[:, i * HEAD_DIM:(i + 1) * HEAD_DIM], qn_ref[...], q_scale)
    for i in range(n_k):
        k_ref[:, i * HEAD_DIM:(i + 1) * HEAD_DIM] = head(
            qk[:, q_end + i * HEAD_DIM:q_end + (i + 1) * HEAD_DIM], kn_ref[...], 1.0)
    rest = jnp.dot(h, w_ref[0, :, k_end:f_end], preferred_element_type=F32)
    tm = x.shape[0]
    ones_rows = (lax.broadcasted_iota(jnp.int32, (VT_PAD, tm), 0) == 0).astype(BF16)
    for i in range(n_k):
        base = i * VT_ROWS
        vt_ref[base:base + HEAD_DIM, :] = rest[:, i * HEAD_DIM:(i + 1) * HEAD_DIM].T.astype(BF16)
        vt_ref[base + HEAD_DIM:base + VT_ROWS, :] = ones_rows
    us_ref[...] = rest[:, v_end - k_end:s_end - k_end]
    uf = rest[:, s_end - k_end:f_end - k_end].astype(BF16)
    ab = jnp.dot(uf, cs_ref[...], preferred_element_type=F32)
    fw = ab_ref.shape[2]
    ab_ref[0] = ab[:, 0:fw].astype(BF16)
    ab_ref[1] = ab[:, fw:2 * fw].astype(BF16)


def _layer_block(w, layer):
    return pl.BlockSpec((1,) + w.shape[1:], lambda *_: (layer, 0, 0))


def _inproj_call(x, gain, shift, scale, w_in, layer, qn, kn, cos, sin, cs, rope):
    l, d = x.shape
    tm = min(512, l)
    qw = Q_PER_KV * N_KV_HEADS * HEAD_DIM
    kw = N_KV_HEADS * HEAD_DIM
    fw = cs.shape[0]
    sw = w_in.shape[2] - qw - 2 * kw - fw
    row = lambda i: (i, 0)
    fix = lambda i: (0, 0)
    return pl.pallas_call(
        functools.partial(_inproj_kernel, rope),
        grid=(l // tm,),
        in_specs=[pl.BlockSpec((tm, d), row),
                  pl.BlockSpec((1, d), fix), pl.BlockSpec((1, d), fix), pl.BlockSpec((1, d), fix),
                  _layer_block(w_in, layer),
                  pl.BlockSpec((1, HEAD_DIM), fix), pl.BlockSpec((1, HEAD_DIM), fix),
                  pl.BlockSpec((tm, HEAD_DIM), row), pl.BlockSpec((tm, HEAD_DIM), row),
                  pl.BlockSpec(cs.shape, fix)],
        out_specs=[pl.BlockSpec((tm, qw), row),
                   pl.BlockSpec((tm, kw), row),
                   pl.BlockSpec((N_KV_HEADS * VT_ROWS, tm), lambda i: (0, i)),
                   pl.BlockSpec((tm, sw), row),
                   pl.BlockSpec((2, tm, fw), lambda i: (0, i, 0))],
        out_shape=[jax.ShapeDtypeStruct((l, qw), BF16),
                   jax.ShapeDtypeStruct((l, kw), BF16),
                   jax.ShapeDtypeStruct((N_KV_HEADS * VT_ROWS, l), BF16),
                   jax.ShapeDtypeStruct((l, sw), F32),
                   jax.ShapeDtypeStruct((2, l, fw), BF16)],
        compiler_params=_cparams(("arbitrary",)),
        name="in_proj",
    )(x, gain, shift, scale, w_in, qn, kn, cos, sin, cs)


def _attn_kernel(nkb, tq, q_ref, k_ref, vt_ref, o_ref, qs_ref, s0_ref, s1_ref, bm0_ref, bm1_ref,
                 m_ref, acc_ref):
    j = pl.program_id(2)
    s_refs = (s0_ref, s1_ref)
    bm_refs = (bm0_ref, bm1_ref)

    tk = k_ref.shape[0]
    sub = min(ATTN_SUB, tk)

    def step(score_slot, absorb_slot):
        if absorb_slot is not None:
            m_prev = m_ref[...]
            m_new = jnp.maximum(m_prev, bm_refs[absorb_slot][...])
            alpha = jnp.exp2(m_prev - m_new)
            pv = jnp.zeros(acc_ref.shape, F32)
        bmax = None
        for c in range(tk // sub):
            rows = slice(c * sub, (c + 1) * sub)
            if score_slot is not None:
                s = lax.dot_general(k_ref[rows, :], qs_ref[...], (((1,), (1,)), ((), ())),
                                    preferred_element_type=F32)
                s_refs[score_slot][rows, :] = s
                cmax = jnp.max(s, axis=0, keepdims=True)
                bmax = cmax if bmax is None else jnp.maximum(bmax, cmax)
            if absorb_slot is not None:
                p = jnp.exp2((s_refs[absorb_slot][rows, :] - m_new).astype(BF16))
                pv = pv + jnp.dot(vt_ref[:, rows], p, preferred_element_type=F32)
        if score_slot is not None:
            bm_refs[score_slot][...] = bmax
        if absorb_slot is not None:
            acc_ref[...] = alpha * acc_ref[...] + pv
            m_ref[...] = m_new

    @pl.when(j == 0)
    def _first():
        for h in range(Q_PER_KV):
            qs_ref[h * tq:(h + 1) * tq, :] = q_ref[:, h * HEAD_DIM:(h + 1) * HEAD_DIM]
        m_ref[...] = jnp.full(m_ref.shape, -jnp.inf, F32)
        acc_ref[...] = jnp.zeros(acc_ref.shape, F32)
        step(0, None)

    for parity in (0, 1):
        @pl.when((j > 0) & (j < nkb) & (j % 2 == parity))
        def _mid(parity=parity):
            step(parity, 1 - parity)

    @pl.when(j == nkb)
    def _last():
        step(None, (nkb - 1) % 2)
        o = acc_ref[0:HEAD_DIM, :] / acc_ref[HEAD_DIM:HEAD_DIM + 1, :]
        for h in range(Q_PER_KV):
            o_ref[:, h * HEAD_DIM:(h + 1) * HEAD_DIM] = o[:, h * tq:(h + 1) * tq].T.astype(BF16)


def _pick_block(total, cap):
    best = 128
    for b in range(128, cap + 1, 128):
        if total % b == 0:
            best = b
    return best


def _attn_call(q, k_all, vt_all):
    l = q.shape[0]
    s_len = k_all.shape[0]
    tq = min(ATTN_TQ, l)
    tk = _pick_block(s_len, ATTN_TK)
    nkb = s_len // tk
    gw = Q_PER_KV * HEAD_DIM
    n = Q_PER_KV * tq
    return pl.pallas_call(
        functools.partial(_attn_kernel, nkb, tq),
        grid=(N_KV_HEADS, l // tq, nkb + 1),
        in_specs=[pl.BlockSpec((tq, gw), lambda g, i, j: (i, g)),
                  pl.BlockSpec((tk, HEAD_DIM), lambda g, i, j: (jnp.minimum(j, nkb - 1), g)),
                  pl.BlockSpec((VT_ROWS, tk), lambda g, i, j: (g, jnp.maximum(j - 1, 0)))],
        out_specs=pl.BlockSpec((tq, gw), lambda g, i, j: (i, g)),
        out_shape=jax.ShapeDtypeStruct(q.shape, BF16),
        scratch_shapes=[pltpu.VMEM((n, HEAD_DIM), BF16),
                        pltpu.VMEM((tk, n), F32),
                        pltpu.VMEM((tk, n), F32),
                        pltpu.VMEM((1, n), F32),
                        pltpu.VMEM((1, n), F32),
                        pltpu.VMEM((1, n), F32),
                        pltpu.VMEM((VT_ROWS, n), F32)],
        compiler_params=_cparams(("arbitrary", "arbitrary", "arbitrary")),
        name="attention",
    )(q, k_all, vt_all)


def _ssm_tables(lam_re, lam_im, log_dt, b_re, b_im, c_re, c_im, d):
    t = SSM_CHUNK
    g, p = lam_re.shape[1:]
    hdim = b_re.shape[-1]
    lr = lam_re.astype(F32)
    li = lam_im.astype(F32)
    dt = jnp.exp(log_dt.astype(F32))[..., None]
    mag = jnp.exp(lr * dt)
    ar = mag * jnp.cos(li * dt)
    ai = mag * jnp.sin(li * dt)
    den = lr * lr + li * li
    cr = ((ar - 1.0) * lr + ai * li) / den
    ci = (ai * lr - (ar - 1.0) * li) / den
    br = b_re.astype(F32)
    bi = b_im.astype(F32)
    bbr = cr[..., None] * br - ci[..., None] * bi
    bbi = cr[..., None] * bi + ci[..., None] * br
    n = jnp.arange(t + 1, dtype=F32)[:, None, None, None]
    pmag = jnp.exp(n * (lr * dt)[None])
    pr = pmag * jnp.cos(n * (li * dt)[None])
    pi = pmag * jnp.sin(n * (li * dt)[None])
    cre = c_re.astype(F32)
    cim = c_im.astype(F32)

    gs = SSM_LANES // hdim
    n_sg = g // gs

    def selector(r, inner):
        e = np.zeros((gs, r, gs, inner, r, inner), np.float32)
        for grp in range(gs):
            e[grp, :, grp] = np.eye(r * inner, dtype=np.float32).reshape(r, inner, r, inner)
        return jnp.asarray(e.reshape(gs, r * gs * inner, r * inner), dtype=BF16)

    def expand(small):
        _, r1, i1, r2, i2 = small.shape
        blocks = small.reshape(n_sg, gs, r1 * i1, r2 * i2).astype(BF16)
        left = jnp.einsum('grk,sgkl->sgrl', selector(r1, i1), blocks,
                          preferred_element_type=F32).astype(BF16)
        return jnp.einsum('sgrl,gcl->src', left, selector(r2, i2),
                          preferred_element_type=F32).astype(BF16)

    def summ(direction, powers):
        er = pr[powers, direction][..., None] * bbr[direction][None] \
            - pi[powers, direction][..., None] * bbi[direction][None]
        ei = pr[powers, direction][..., None] * bbi[direction][None] \
            + pi[powers, direction][..., None] * bbr[direction][None]
        tr = lambda e: jnp.transpose(e, (1, 0, 3, 2))
        return expand(jnp.stack([tr(er), tr(ei)], axis=3))
    steps = jnp.arange(t)
    p_f = summ(0, t - 1 - steps)
    p_b = summ(1, steps)

    row = lambda v: v.reshape(n_sg, gs * p)
    a_row = jnp.concatenate([row(pr[t, 0]), row(pr[t, 1]), row(pi[t, 0]), row(pi[t, 1])],
                            axis=-1)[:, None, :]

    def kern(direction):
        er = pr[:t, direction][..., None] * bbr[direction][None] \
            - pi[:t, direction][..., None] * bbi[direction][None]
        ei = pr[:t, direction][..., None] * bbi[direction][None] \
            + pi[:t, direction][..., None] * bbr[direction][None]
        return (jnp.einsum('gop,ngpi->ngoi', cre[direction], er)
                - jnp.einsum('gop,ngpi->ngoi', cim[direction], ei))
    kf = kern(0)
    kb = kern(1)
    s_idx = steps[:, None]
    t_idx = steps[None, :]
    lag_f = jnp.clip(t_idx - s_idx, 0, t - 1)
    lag_b = jnp.clip(s_idx - t_idx, 0, t - 1)
    mf = jnp.where((t_idx >= s_idx)[:, :, None, None, None], kf[lag_f], 0.0)
    mb = jnp.where((s_idx >= t_idx)[:, :, None, None, None], kb[lag_b], 0.0)
    dd = d.astype(F32).reshape(g, hdim)
    eye_t = jnp.eye(t, dtype=F32)[:, :, None, None, None]
    eye_c = jnp.eye(hdim, dtype=F32)[None, None, None]
    m5 = mf + mb + eye_t * eye_c * dd[None, None, :, :, None]
    m_mat = expand(jnp.transpose(m5, (2, 0, 4, 1, 3)))

    def state_out(direction, powers):
        prn = pr[powers, direction]
        pin = pi[powers, direction]
        wr = cre[direction][None] * prn[:, :, None, :] - cim[direction][None] * pin[:, :, None, :]
        wi = -cre[direction][None] * pin[:, :, None, :] - cim[direction][None] * prn[:, :, None, :]
        tr = lambda w: jnp.transpose(w, (1, 3, 0, 2))
        return expand(jnp.stack([tr(wr), tr(wi)], axis=1))
    q_f = state_out(0, steps + 1)
    q_b = state_out(1, t - steps)
    return p_f, p_b, m_mat, q_f, q_b, a_row


def _ssm_kernel(tbk, usf_ref, usb_ref, pf_ref, pb_ref, m_ref, qf_ref, qb_ref, a_ref, h0_ref,
                yf_ref, yb_ref, hend_ref, xf_ref, xb_ref, sf_ref, sb_ref, hf_ref, hb_ref, h_ref):
    t = SSM_CHUNK
    lanes = SSM_LANES

    @pl.when(pl.program_id(1) == 0)
    def _init():
        h_ref[...] = h0_ref[0]

    for tt in range(t):
        cols = slice(tt * lanes, (tt + 1) * lanes)
        xf_ref[:, cols] = usf_ref[pl.ds(tt, tbk, stride=t), :].astype(BF16)
        xb_ref[:, cols] = usb_ref[pl.ds(tt, tbk, stride=t), :].astype(BF16)
    sf_ref[...] = jnp.dot(xf_ref[...], pf_ref[0], preferred_element_type=F32)
    sb_ref[...] = jnp.dot(xb_ref[...], pb_ref[0], preferred_element_type=F32)

    w = a_ref.shape[2] // 4
    a = a_ref[0]
    arf, arb, aif, aib = (a[:, i * w:(i + 1) * w] for i in range(4))
    h = h_ref[...]

    def body(k, carry):
        hrf, hrb, hif, hib = carry
        kb = tbk - 1 - k
        hf_ref[pl.ds(k, 1), 0:w] = hrf
        hf_ref[pl.ds(k, 1), w:2 * w] = hif
        hb_ref[pl.ds(kb, 1), 0:w] = hrb
        hb_ref[pl.ds(kb, 1), w:2 * w] = hib
        sf = sf_ref[pl.ds(k, 1), :]
        sb = sb_ref[pl.ds(kb, 1), :]
        return (arf * hrf - aif * hif + sf[:, 0:w], arb * hrb - aib * hib + sb[:, 0:w],
                arf * hif + aif * hrf + sf[:, w:2 * w], arb * hib + aib * hrb + sb[:, w:2 * w])

    carry = lax.fori_loop(0, tbk, body, tuple(h[:, i * w:(i + 1) * w] for i in range(4)),
                          unroll=8)
    h = jnp.concatenate(carry, axis=1)
    h_ref[...] = h
    hend_ref[0] = h

    yf = (jnp.dot(xf_ref[...], m_ref[0], preferred_element_type=F32)
          + jnp.dot(hf_ref[...].astype(BF16), qf_ref[0], preferred_element_type=F32))
    yb = jnp.dot(hb_ref[...].astype(BF16), qb_ref[0], preferred_element_type=F32)
    for tt in range(t):
        cols = slice(tt * lanes, (tt + 1) * lanes)
        yf_ref[pl.ds(tt, tbk, stride=t), :] = yf[:, cols]
        yb_ref[pl.ds(tt, tbk, stride=t), :] = yb[:, cols]


def _ssm_apply(us, tables, h0):
    p_f, p_b, m_mat, q_f, q_b, a_row = tables
    l, w = us.shape
    n_sg = w // SSM_LANES
    n = l // SSM_CHUNK
    tbk = min(256, n)
    nb = n // tbk
    rows = tbk * SSM_CHUNK
    kx = SSM_CHUNK * SSM_LANES
    sw = p_f.shape[2]
    hw = a_row.shape[2]
    per_sg = lambda a: pl.BlockSpec((1,) + a.shape[1:], lambda s, i: (s, 0, 0))
    fwd = pl.BlockSpec((rows, SSM_LANES), lambda s, i: (i, s))
    bwd = pl.BlockSpec((rows, SSM_LANES), lambda s, i: (nb - 1 - i, s))
    return pl.pallas_call(
        functools.partial(_ssm_kernel, tbk),
        grid=(n_sg, nb),
        in_specs=[fwd, bwd, per_sg(p_f), per_sg(p_b), per_sg(m_mat), per_sg(q_f), per_sg(q_b),
                  per_sg(a_row), per_sg(h0)],
        out_specs=[fwd, bwd, per_sg(h0)],
        out_shape=[jax.ShapeDtypeStruct((l, w), F32), jax.ShapeDtypeStruct((l, w), F32),
                   jax.ShapeDtypeStruct(h0.shape, F32)],
        scratch_shapes=[pltpu.VMEM((tbk, kx), BF16), pltpu.VMEM((tbk, kx), BF16),
                        pltpu.VMEM((tbk, sw), F32), pltpu.VMEM((tbk, sw), F32),
                        pltpu.VMEM((tbk, sw), F32), pltpu.VMEM((tbk, sw), F32),
                        pltpu.VMEM((1, hw), F32)],
        compiler_params=_cparams(("arbitrary", "arbitrary")),
        name="ssm_scan",
    )(us, us, p_f, p_b, m_mat, q_f, q_b, a_row, h0)


def _dft_tables(l, n1, hw):
    n2 = l // n1
    scale = 1.0 / math.sqrt(l * hw)
    if n1 == 1:
        t1 = None
    else:
        ang = 2.0 * np.pi * ((np.arange(n1)[:, None] * np.arange(n1)[None, :]) % n1) / n1
        c, s = np.cos(ang), np.sin(ang)
        t1 = jnp.asarray(np.block([[c, -s], [-s, -c]]), dtype=BF16)
    k1 = np.arange(n1)[:, None, None]
    k2 = np.arange(n2)[None, :, None]
    m = np.arange(n2)[None, None, :]
    ang = 2.0 * np.pi * ((m * (n1 * k2 + k1)) % l) / l
    sign = -1.0 if n1 == 1 else 1.0
    g = np.concatenate([np.cos(ang), sign * np.sin(ang)], axis=-1) * scale
    return t1, jnp.asarray(g, dtype=BF16)


def _dft1_kernel(t_ref, x_ref, z_ref):
    z_ref[...] = jnp.dot(t_ref[...], x_ref[...], preferred_element_type=F32).astype(z_ref.dtype)


def _dft1_call(t1, ab2d):
    r, n = ab2d.shape
    tn = min(4096, n)
    return pl.pallas_call(
        _dft1_kernel,
        grid=(n // tn,),
        in_specs=[pl.BlockSpec((r, r), lambda j: (0, 0)),
                  pl.BlockSpec((r, tn), lambda j: (0, j))],
        out_specs=pl.BlockSpec((r, tn), lambda j: (0, j)),
        out_shape=jax.ShapeDtypeStruct((r, n), BF16),
        compiler_params=_cparams(("arbitrary",)),
        name="dft_stage1",
    )(t1, ab2d)


def _dft2_kernel(bt, zr_ref, zi_ref, g_ref, w_ref, o_ref):
    fw = w_ref.shape[1]
    for b in range(bt):
        z = jnp.concatenate([zr_ref[b], zi_ref[b]], axis=0)
        x = jnp.dot(g_ref[b], z, preferred_element_type=F32)
        o_ref[:, b * fw:(b + 1) * fw] = jnp.dot(
            x.astype(BF16), w_ref[0], preferred_element_type=F32).astype(o_ref.dtype)


def _dft2_call(z3, g, fourier_w, layer):
    n1x2, n2, fw = z3.shape
    n1 = n1x2 // 2
    bt = min(8, n1)
    return pl.pallas_call(
        functools.partial(_dft2_kernel, bt),
        grid=(n1 // bt,),
        in_specs=[pl.BlockSpec((bt, n2, fw), lambda b: (b, 0, 0)),
                  pl.BlockSpec((bt, n2, fw), lambda b: (b + n1 // bt, 0, 0)),
                  pl.BlockSpec((bt, n2, 2 * n2), lambda b: (b, 0, 0)),
                  _layer_block(fourier_w, layer)],
        out_specs=pl.BlockSpec((n2, bt * fw), lambda b: (0, b)),
        out_shape=jax.ShapeDtypeStruct((n2, n1 * fw), BF16),
        compiler_params=_cparams(("arbitrary",)),
        name="dft_stage2",
    )(z3, z3, g, fourier_w)


def _fourier_apply(ab, tables, fourier_w, layer, n1):
    t1, g = tables
    _, l, fw = ab.shape
    n2 = l // n1
    if n1 == 1:
        z3 = ab
    else:
        z3 = _dft1_call(t1, ab.reshape(2 * n1, n2 * fw)).reshape(2 * n1, n2, fw)
    return _dft2_call(z3, g, fourier_w, layer).reshape(l, fw)


def _outproj_kernel(attn_ref, yf_ref, yb_ref, four_ref, x_ref, wo_ref, gw_ref, gb_ref, npost_ref,
                    gate_ref, npre_ref, sh_ref, sc_ref, xo_ref, h_ref):
    ys = yf_ref[...] + yb_ref[...]
    gl = 0.5 * ys * (1.0 + jnp.tanh(math.sqrt(2.0 / math.pi) * (ys + 0.044715 * (ys * ys * ys))))
    z = jnp.dot(gl.astype(BF16), gw_ref[0], preferred_element_type=F32) + gb_ref[...]
    ssm = (gl * jax.nn.sigmoid(z)).astype(BF16)
    cat = jnp.concatenate([attn_ref[...], ssm, four_ref[...]], axis=-1)
    y = jnp.dot(cat, wo_ref[0], preferred_element_type=F32)
    xn = x_ref[...] + gate_ref[...] * _rms(y, npost_ref[...])
    xo_ref[...] = xn
    h_ref[...] = (_rms(xn, npre_ref[...]) * (1.0 + sc_ref[...]) + sh_ref[...]).astype(BF16)


def _outproj_call(attn, yf, yb, four, x, w_out, glu_w, layer, glu_b, npost, gate, npre, shift,
                  scale):
    l, d = x.shape
    tm = min(512, l)
    row = lambda i: (i, 0)
    fix = lambda i: (0, 0)
    vec = pl.BlockSpec((1, d), fix)
    return pl.pallas_call(
        _outproj_kernel,
        grid=(l // tm,),
        in_specs=[pl.BlockSpec((tm, attn.shape[1]), row),
                  pl.BlockSpec((tm, yf.shape[1]), row),
                  pl.BlockSpec((tm, yb.shape[1]), row),
                  pl.BlockSpec((tm, four.shape[1]), row),
                  pl.BlockSpec((tm, d), row),
                  _layer_block(w_out, layer),
                  _layer_block(glu_w, layer),
                  pl.BlockSpec((1, glu_w.shape[2]), fix),
                  vec, vec, vec, vec, vec],
        out_specs=[pl.BlockSpec((tm, d), row), pl.BlockSpec((tm, d), row)],
        out_shape=[jax.ShapeDtypeStruct((l, d), F32), jax.ShapeDtypeStruct((l, d), BF16)],
        compiler_params=_cparams(("arbitrary",)),
        name="out_proj",
    )(attn, yf, yb, four, x, w_out, glu_w, glu_b, npost, gate, npre, shift, scale)


def _ffn_kernel(nj1, nj2, th, tn, h_ref, x_ref, wg_ref, wu_ref, wd_ref, npost_ref, gate_ref, o_ref,
                act_ref, tmp_ref):
    j = pl.program_id(1)

    @pl.when(j < nj1)
    def _up():
        h = h_ref[...]
        a = jnp.dot(h, wg_ref[0], preferred_element_type=F32)
        u = jnp.dot(h, wu_ref[0], preferred_element_type=F32)
        tmp_ref[...] = (a * jax.nn.sigmoid(a) * u).astype(BF16)

    for c in range(nj1):
        @pl.when(j == c)
        def _place(c=c):
            act_ref[:, c * th:(c + 1) * th] = tmp_ref[...]

    for c in range(nj2):
        @pl.when(j == nj1 + c)
        def _down(c=c):
            o_ref[:, c * tn:(c + 1) * tn] = jnp.dot(act_ref[...], wd_ref[0],
                                                    preferred_element_type=F32)

    @pl.when(j == nj1 + nj2 - 1)
    def _finish():
        o_ref[...] = x_ref[...] + gate_ref[...] * _rms(o_ref[...], npost_ref[...])


def _ffn_call(h, x, w_gate, w_up, w_down, layer, npost, gate):
    l, d = x.shape
    fh = w_gate.shape[2]
    th, tn = FFN_TH, FFN_TN
    nj1 = fh // th
    nj2 = d // tn
    tm = min(FFN_TM, l)
    vec = pl.BlockSpec((1, d), lambda i, j: (0, 0))
    up = pl.BlockSpec((1, d, th), lambda i, j: (layer, 0, jnp.minimum(j, nj1 - 1)))
    once = pl.Buffered(1)
    return pl.pallas_call(
        functools.partial(_ffn_kernel, nj1, nj2, th, tn),
        grid=(l // tm, nj1 + nj2),
        in_specs=[pl.BlockSpec((tm, d), lambda i, j: (i, 0)),
                  pl.BlockSpec((tm, d), lambda i, j: (i, 0), pipeline_mode=once),
                  up, up,
                  pl.BlockSpec((1, fh, tn), lambda i, j: (layer, 0, jnp.maximum(j - nj1, 0))),
                  vec, vec],
        out_specs=pl.BlockSpec((tm, d), lambda i, j: (i, 0), pipeline_mode=once),
        out_shape=jax.ShapeDtypeStruct((l, d), F32),
        scratch_shapes=[pltpu.VMEM((tm, fh), BF16), pltpu.VMEM((tm, th), BF16)],
        compiler_params=_cparams(("arbitrary", "arbitrary")),
        name="ffn",
    )(h, x, w_gate, w_up, w_down, npost, gate)


def _rope_tables(l):
    t = np.arange(l)
    row = (t // GRID_W).astype(np.float32)
    col = (t % GRID_W).astype(np.float32)
    freqs = np.float32(ROPE_THETA) ** (-np.arange(ROPE_PAIRS, dtype=np.float32) / np.float32(ROPE_PAIRS))
    ang_r = (row[:, None] * freqs).astype(np.float32)
    ang_c = (col[:, None] * freqs).astype(np.float32)
    cos = np.concatenate([np.cos(ang_r)] * 2 + [np.cos(ang_c)] * 2, axis=-1)
    sin = np.concatenate([-np.sin(ang_r), np.sin(ang_r), -np.sin(ang_c), np.sin(ang_c)], axis=-1)
    return jnp.asarray(cos, dtype=F32), jnp.asarray(sin, dtype=F32)


def _channel_dft_table(fw):
    hw = FOURIER_HEAD_DIM
    ang = 2.0 * np.pi * ((np.arange(hw)[:, None] * np.arange(hw)[None, :]) % hw) / hw
    eye = np.eye(fw // hw)
    return jnp.asarray(np.concatenate([np.kron(eye, np.cos(ang)), np.kron(eye, np.sin(ang))], axis=1),
                       dtype=BF16)


def _dft_split(l):
    n1 = 1
    while n1 * n1 < l:
        n1 *= 2
    return n1 if (l >= 1024 and n1 * n1 == l) else 1


def kernel(x, c, ctx, c_ctx, ada_w, ada_b, norm_mix_pre, norm_mix_post, norm_ffn_pre, norm_ffn_post, w_in, q_norm, k_norm, ssm_lam_re, ssm_lam_im, ssm_log_dt, ssm_b_re, ssm_b_im, ssm_c_re, ssm_c_im, ssm_d, ssm_glu_w, ssm_glu_b, fourier_w, w_out, ffn_w_gate, ffn_w_up, ffn_w_down):
    depth = ada_w.shape[0]
    _, l, d = x.shape
    n_ctx = ctx.shape[1]
    fw = fourier_w.shape[1]
    sw = ssm_d.shape[1]

    cond = jnp.zeros((8, d), F32).at[0].set(c[0]).at[1].set(c_ctx)
    mod = _ada_call(cond, ada_w, ada_b)

    cos, sin = _rope_tables(l)
    zero_tab = jnp.zeros((n_ctx, HEAD_DIM), F32)
    cs = _channel_dft_table(fw)
    n1 = _dft_split(l)
    n1c = _dft_split(n_ctx)
    dft_x = _dft_tables(l, n1, FOURIER_HEAD_DIM)
    dft_c = _dft_tables(n_ctx, n1c, FOURIER_HEAD_DIM)
    h0 = jnp.zeros((sw // SSM_LANES, 1, 4 * (SSM_LANES // SSM_GROUP) * SSM_STATE), F32)

    w_in_b = w_in.astype(BF16)
    w_out_b = w_out.astype(BF16)
    glu_w_b = ssm_glu_w.astype(BF16)
    four_w_b = fourier_w.astype(BF16)
    wg_b = ffn_w_gate.astype(BF16)
    wu_b = ffn_w_up.astype(BF16)
    wd_b = ffn_w_down.astype(BF16)

    xs = x[0]
    xc = ctx[0]
    for layer in range(depth):
        need_ctx = layer < depth - 1
        vecs = lambda r: [mod[layer, r:r + 1, i * d:(i + 1) * d] for i in range(6)]
        sh_m, sc_m, g_m, sh_f, sc_f, g_f = vecs(0)
        shc_m, scc_m, gc_m, shc_f, scc_f, gc_f = vecs(1)
        row = lambda a: a[layer].reshape(1, -1)
        tables = _ssm_tables(ssm_lam_re[layer], ssm_lam_im[layer], ssm_log_dt[layer],
                             ssm_b_re[layer], ssm_b_im[layer], ssm_c_re[layer], ssm_c_im[layer],
                             ssm_d[layer])

        qc, kc, vtc, usc, abc = _inproj_call(xc, row(norm_mix_pre), shc_m, scc_m, w_in_b, layer,
                                             row(q_norm), row(k_norm), zero_tab, zero_tab, cs, False)
        yfc, ybc, hend_c = _ssm_apply(usc, tables, h0)

        q, k, vt, us, ab = _inproj_call(xs, row(norm_mix_pre), sh_m, sc_m, w_in_b, layer,
                                        row(q_norm), row(k_norm), cos, sin, cs, True)
        attn = _attn_call(q, jnp.concatenate([k, kc], axis=0), jnp.concatenate([vt, vtc], axis=1))
        yf, yb, _ = _ssm_apply(us, tables, hend_c)
        four = _fourier_apply(ab, dft_x, four_w_b, layer, n1)
        xs, hs = _outproj_call(attn, yf, yb, four, xs, w_out_b, glu_w_b, layer, row(ssm_glu_b),
                               row(norm_mix_post), g_m, row(norm_ffn_pre), sh_f, sc_f)
        xs = _ffn_call(hs, xs, wg_b, wu_b, wd_b, layer, row(norm_ffn_post), g_f)

        if need_ctx:
            attn_c = _attn_call(qc, kc, vtc)
            four_c = _fourier_apply(abc, dft_c, four_w_b, layer, n1c)
            xc, hc = _outproj_call(attn_c, yfc, ybc, four_c, xc, w_out_b, glu_w_b, layer,
                                   row(ssm_glu_b), row(norm_mix_post), gc_m, row(norm_ffn_pre),
                                   shc_f, scc_f)
            xc = _ffn_call(hc, xc, wg_b, wu_b, wd_b, layer, row(norm_ffn_post), gc_f)
    return xs[None]
```

```python
import functools
import math

import numpy as np
import jax
import jax.numpy as jnp
from jax import lax
from jax.experimental import pallas as pl
from jax.experimental.pallas import tpu as pltpu

F32 = jnp.float32
BF16 = jnp.bfloat16

HEAD_DIM = 128
N_KV_HEADS = 2
Q_PER_KV = 4
GRID_W = 64
ROPE_THETA = 10000.0
ROPE_PAIRS = HEAD_DIM // 4
SSM_GROUP = 16
SSM_STATE = 64
FOURIER_HEAD_DIM = 128
NORM_EPS = 1e-6
SSM_CHUNK = 8
SSM_LANES = 128
LOG2E = 1.4426950408889634
ATTN_TQ = 512
ATTN_TK = 1280
ATTN_MAX_GAP = 100.0
ATTN_SUB = 256
FFN_TM = 1024
FFN_TH = 512
FFN_TN = 256
VT_PAD = 16
VT_ROWS = HEAD_DIM + VT_PAD

VMEM_LIMIT_MB = 56


def _cparams(semantics, vmem_mb=VMEM_LIMIT_MB):
    return pltpu.CompilerParams(dimension_semantics=semantics,
                                vmem_limit_bytes=vmem_mb * 2 ** 20)


def _rms(x, gain):
    return x * lax.rsqrt(jnp.mean(x * x, axis=-1, keepdims=True) + NORM_EPS) * gain


def _ada_kernel(s_ref, w_ref, b_ref, o_ref):
    s = s_ref[...]
    act = (s * jax.nn.sigmoid(s)).astype(BF16)
    o_ref[0] = jnp.dot(act, w_ref[0].astype(BF16), preferred_element_type=F32) + b_ref[0]


def _ada_call(cond, ada_w, ada_b):
    depth, d, n = ada_w.shape
    tn = 1024
    return pl.pallas_call(
        _ada_kernel,
        grid=(depth, n // tn),
        in_specs=[pl.BlockSpec((8, d), lambda l, j: (0, 0)),
                  pl.BlockSpec((1, d, tn), lambda l, j: (l, 0, j)),
                  pl.BlockSpec((1, 1, tn), lambda l, j: (l, 0, j))],
        out_specs=pl.BlockSpec((1, 8, tn), lambda l, j: (l, 0, j)),
        out_shape=jax.ShapeDtypeStruct((depth, 8, n), F32),
        compiler_params=_cparams(("arbitrary", "arbitrary")),
        name="ada_ln",
    )(cond, ada_w, ada_b.reshape(depth, 1, n))


def _inproj_kernel(rope, x_ref, g_ref, sh_ref, sc_ref, w_ref, qn_ref, kn_ref, cos_ref, sin_ref,
                   cs_ref, q_ref, k_ref, vt_ref, us_ref, ab_ref):
    x = x_ref[...]
    h = (_rms(x, g_ref[...]) * (1.0 + sc_ref[...]) + sh_ref[...]).astype(BF16)
    n_q = q_ref.shape[1] // HEAD_DIM
    n_k = k_ref.shape[1] // HEAD_DIM
    q_end = n_q * HEAD_DIM
    k_end = q_end + n_k * HEAD_DIM
    v_end = k_end + n_k * HEAD_DIM
    s_end = v_end + us_ref.shape[1]
    f_end = s_end + ab_ref.shape[2]

    if rope:
        cos = cos_ref[...]
        sin = sin_ref[...]
        lane = lax.broadcasted_iota(jnp.int32, cos.shape, 1)
        low = (lane % 64) < 32

    def head(t, gain, scale):
        t = _rms(t, gain)
        if rope:
            sw = jnp.where(low, pltpu.roll(t, HEAD_DIM - 32, 1), pltpu.roll(t, 32, 1))
            t = t * cos + sw * sin
        if scale != 1.0:
            t = t * scale
        return t.astype(BF16)

    qk = jnp.dot(h, w_ref[0, :, 0:k_end], preferred_element_type=F32)
    q_scale = HEAD_DIM ** -0.5 * LOG2E
    for i in range(n_q):
        q_ref[:, i * HEAD_DIM:(i + 1) * HEAD_DIM] = head(
            qk[:, i * HEAD_DIM:(i + 1) * HEAD_DIM], qn_ref[...], q_scale)
    for i in range(n_k):
        k_ref[:, i * HEAD_DIM:(i + 1) * HEAD_DIM] = head(
            qk[:, q_end + i * HEAD_DIM:q_end + (i + 1) * HEAD_DIM], kn_ref[...], 1.0)
    rest = jnp.dot(h, w_ref[0, :, k_end:f_end], preferred_element_type=F32)
    ones_rows = (lax.broadcasted_iota(jnp.int32, (VT_PAD, ATTN_SUB), 0) == 0).astype(BF16)
    for i in range(n_k):
        vt = rest[:, i * HEAD_DIM:(i + 1) * HEAD_DIM].T.astype(BF16)
        for b in range(vt_ref.shape[1]):
            vt_ref[i, b, 0:HEAD_DIM, :] = vt[:, b * ATTN_SUB:(b + 1) * ATTN_SUB]
            vt_ref[i, b, HEAD_DIM:VT_ROWS, :] = ones_rows
    us_ref[...] = rest[:, v_end - k_end:s_end - k_end]
    uf = rest[:, s_end - k_end:f_end - k_end].astype(BF16)
    ab = jnp.dot(uf, cs_ref[...], preferred_element_type=F32)
    fw = ab_ref.shape[2]
    ab_ref[0] = ab[:, 0:fw].astype(BF16)
    ab_ref[1] = ab[:, fw:2 * fw].astype(BF16)


def _layer_block(w, layer):
    return pl.BlockSpec((1,) + w.shape[1:], lambda *_: (layer, 0, 0))


def _inproj_call(x, gain, shift, scale, w_in, layer, qn, kn, cos, sin, cs, rope):
    l, d = x.shape
    tm = min(512, l)
    qw = Q_PER_KV * N_KV_HEADS * HEAD_DIM
    kw = N_KV_HEADS * HEAD_DIM
    fw = cs.shape[0]
    sw = w_in.shape[2] - qw - 2 * kw - fw
    row = lambda i: (i, 0)
    fix = lambda i: (0, 0)
    return pl.pallas_call(
        functools.partial(_inproj_kernel, rope),
        grid=(l // tm,),
        in_specs=[pl.BlockSpec((tm, d), row),
                  pl.BlockSpec((1, d), fix), pl.BlockSpec((1, d), fix), pl.BlockSpec((1, d), fix),
                  _layer_block(w_in, layer),
                  pl.BlockSpec((1, HEAD_DIM), fix), pl.BlockSpec((1, HEAD_DIM), fix),
                  pl.BlockSpec((tm, HEAD_DIM), row), pl.BlockSpec((tm, HEAD_DIM), row),
                  pl.BlockSpec(cs.shape, fix)],
        out_specs=[pl.BlockSpec((tm, qw), row),
                   pl.BlockSpec((tm, kw), row),
                   pl.BlockSpec((N_KV_HEADS, tm // ATTN_SUB, VT_ROWS, ATTN_SUB),
                                lambda i: (0, i, 0, 0)),
                   pl.BlockSpec((tm, sw), row),
                   pl.BlockSpec((2, tm, fw), lambda i: (0, i, 0))],
        out_shape=[jax.ShapeDtypeStruct((l, qw), BF16),
                   jax.ShapeDtypeStruct((l, kw), BF16),
                   jax.ShapeDtypeStruct((N_KV_HEADS, l // ATTN_SUB, VT_ROWS, ATTN_SUB), BF16),
                   jax.ShapeDtypeStruct((l, sw), F32),
                   jax.ShapeDtypeStruct((2, l, fw), BF16)],
        compiler_params=_cparams(("arbitrary",)),
        name="in_proj",
    )(x, gain, shift, scale, w_in, qn, kn, cos, sin, cs)


def _attn_kernel(nkb, tq, q_ref, k_ref, vt_ref, o_ref, qs_ref, s0_ref, s1_ref, bm0_ref, bm1_ref,
                 m_ref, acc_ref):
    j = pl.program_id(2)
    s_refs = (s0_ref, s1_ref)
    bm_refs = (bm0_ref, bm1_ref)

    tk = k_ref.shape[0]
    sub = vt_ref.shape[3]

    def step(score_slot, absorb_slot):
        if absorb_slot is not None:
            m_prev = m_ref[...]
            m_new = jnp.maximum(m_prev, bm_refs[absorb_slot][...])
            alpha = jnp.exp2(m_prev - m_new)
            pv = jnp.zeros(acc_ref.shape, F32)
        bmax = None
        for c in range(tk // sub):
            rows = slice(c * sub, (c + 1) * sub)
            if score_slot is not None:
                s = lax.dot_general(k_ref[rows, :], qs_ref[...], (((1,), (1,)), ((), ())),
                                    preferred_element_type=F32)
                s_refs[score_slot][rows, :] = s
                cmax = jnp.max(s, axis=0, keepdims=True)
                bmax = cmax if bmax is None else jnp.maximum(bmax, cmax)
            if absorb_slot is not None:
                p = jnp.exp2((s_refs[absorb_slot][rows, :] - m_new).astype(BF16))
                pv = pv + jnp.dot(vt_ref[0, c], p, preferred_element_type=F32)
        if score_slot is not None:
            bm_refs[score_slot][...] = bmax
        if absorb_slot is not None:
            acc_ref[...] = alpha * acc_ref[...] + pv
            m_ref[...] = m_new

    @pl.when(j == 0)
    def _first():
        for h in range(Q_PER_KV):
            qs_ref[h * tq:(h + 1) * tq, :] = q_ref[:, h * HEAD_DIM:(h + 1) * HEAD_DIM]
        m_ref[...] = jnp.full(m_ref.shape, -jnp.inf, F32)
        acc_ref[...] = jnp.zeros(acc_ref.shape, F32)
        step(0, None)

    for parity in (0, 1):
        @pl.when((j > 0) & (j < nkb) & (j % 2 == parity))
        def _mid(parity=parity):
            step(parity, 1 - parity)

    @pl.when(j == nkb)
    def _last():
        step(None, (nkb - 1) % 2)
        o = acc_ref[0:HEAD_DIM, :] / acc_ref[HEAD_DIM:HEAD_DIM + 1, :]
        for h in range(Q_PER_KV):
            o_ref[:, h * HEAD_DIM:(h + 1) * HEAD_DIM] = o[:, h * tq:(h + 1) * tq].T.astype(BF16)


def _pick_block(total, unit, cap):
    best = unit
    for b in range(unit, cap + 1, unit):
        if total % b == 0:
            best = b
    return best


def _attn_bounded_kernel(unroll, tq, kn_ref, q_ref, k_ref, vt_ref, o_ref, qs_ref, acc_ref):
    sub = vt_ref.shape[3]
    nblk = vt_ref.shape[1]
    for h in range(Q_PER_KV):
        qs_ref[h * tq:(h + 1) * tq, :] = q_ref[:, h * HEAD_DIM:(h + 1) * HEAD_DIM]
    qf = qs_ref[...].astype(F32)
    norm2 = lax.dot_general(jnp.ones((8, HEAD_DIM), F32), qf * qf, (((1,), (1,)), ((), ())),
                            preferred_element_type=F32)[0:1, :]
    shift = jnp.sqrt(norm2) * kn_ref[pl.program_id(0)]
    acc_ref[...] = jnp.zeros(acc_ref.shape, F32)

    def body(t, carry):
        pv = None
        for u in range(unroll):
            c = t * unroll + u
            start = pl.multiple_of(c * sub, sub)
            s = lax.dot_general(k_ref[pl.ds(start, sub), :], qs_ref[...],
                                (((1,), (1,)), ((), ())), preferred_element_type=F32)
            p = jnp.exp2(s - shift).astype(BF16)
            d = jnp.dot(vt_ref[0, c], p, preferred_element_type=F32)
            pv = d if pv is None else pv + d
        acc_ref[...] += pv
        return carry

    lax.fori_loop(0, nblk // unroll, body, 0)
    o = acc_ref[0:HEAD_DIM, :] / acc_ref[HEAD_DIM:HEAD_DIM + 1, :]
    for h in range(Q_PER_KV):
        o_ref[:, h * HEAD_DIM:(h + 1) * HEAD_DIM] = o[:, h * tq:(h + 1) * tq].T.astype(BF16)


def _attn_bounded_call(q, k_all, vt_all, k_norm_max):
    l = q.shape[0]
    s_len = k_all.shape[0]
    nblk = vt_all.shape[1]
    tq = min(ATTN_TQ, l)
    gw = Q_PER_KV * HEAD_DIM
    n = Q_PER_KV * tq
    unroll = max(u for u in (5, 4, 3, 2, 1) if nblk % u == 0)
    return pl.pallas_call(
        functools.partial(_attn_bounded_kernel, unroll, tq),
        grid=(N_KV_HEADS, l // tq),
        in_specs=[pl.BlockSpec(memory_space=pltpu.SMEM),
                  pl.BlockSpec((tq, gw), lambda g, i: (i, g)),
                  pl.BlockSpec((s_len, HEAD_DIM), lambda g, i: (0, g)),
                  pl.BlockSpec((1,) + vt_all.shape[1:], lambda g, i: (g, 0, 0, 0))],
        out_specs=pl.BlockSpec((tq, gw), lambda g, i: (i, g)),
        out_shape=jax.ShapeDtypeStruct(q.shape, BF16),
        scratch_shapes=[pltpu.VMEM((n, HEAD_DIM), BF16), pltpu.VMEM((VT_ROWS, n), F32)],
        compiler_params=_cparams(("arbitrary", "arbitrary")),
        name="attention_bounded",
    )(k_norm_max, q, k_all, vt_all)


def _attn_dispatch(q, k_all, vt_all):
    qn = jnp.sqrt(jnp.max(jnp.sum(jnp.square(q.astype(F32)).reshape(q.shape[0], -1, HEAD_DIM),
                                  axis=-1)))
    kn = jnp.sqrt(jnp.max(jnp.sum(jnp.square(k_all.astype(F32)).reshape(
        k_all.shape[0], N_KV_HEADS, HEAD_DIM), axis=-1), axis=0))
    safe = 2.0 * qn * jnp.max(kn) <= ATTN_MAX_GAP
    return lax.cond(safe,
                    lambda: _attn_bounded_call(q, k_all, vt_all, kn),
                    lambda: _attn_call(q, k_all, vt_all))


def _attn_call(q, k_all, vt_all):
    l = q.shape[0]
    s_len = k_all.shape[0]
    sub = vt_all.shape[3]
    tq = min(ATTN_TQ, l)
    tk = _pick_block(s_len, sub, ATTN_TK)
    nkb = s_len // tk
    gw = Q_PER_KV * HEAD_DIM
    n = Q_PER_KV * tq
    return pl.pallas_call(
        functools.partial(_attn_kernel, nkb, tq),
        grid=(N_KV_HEADS, l // tq, nkb + 1),
        in_specs=[pl.BlockSpec((tq, gw), lambda g, i, j: (i, g)),
                  pl.BlockSpec((tk, HEAD_DIM), lambda g, i, j: (jnp.minimum(j, nkb - 1), g)),
                  pl.BlockSpec((1, tk // sub, VT_ROWS, sub),
                               lambda g, i, j: (g, jnp.maximum(j - 1, 0), 0, 0))],
        out_specs=pl.BlockSpec((tq, gw), lambda g, i, j: (i, g)),
        out_shape=jax.ShapeDtypeStruct(q.shape, BF16),
        scratch_shapes=[pltpu.VMEM((n, HEAD_DIM), BF16),
                        pltpu.VMEM((tk, n), F32),
                        pltpu.VMEM((tk, n), F32),
                        pltpu.VMEM((1, n), F32),
                        pltpu.VMEM((1, n), F32),
                        pltpu.VMEM((1, n), F32),
                        pltpu.VMEM((VT_ROWS, n), F32)],
        compiler_params=_cparams(("arbitrary", "arbitrary", "arbitrary")),
        name="attention",
    )(q, k_all, vt_all)


def _ssm_tables(lam_re, lam_im, log_dt, b_re, b_im, c_re, c_im, d):
    t = SSM_CHUNK
    g, p = lam_re.shape[1:]
    hdim = b_re.shape[-1]
    lr = lam_re.astype(F32)
    li = lam_im.astype(F32)
    dt = jnp.exp(log_dt.astype(F32))[..., None]
    mag = jnp.exp(lr * dt)
    ar = mag * jnp.cos(li * dt)
    ai = mag * jnp.sin(li * dt)
    den = lr * lr + li * li
    cr = ((ar - 1.0) * lr + ai * li) / den
    ci = (ai * lr - (ar - 1.0) * li) / den
    br = b_re.astype(F32)
    bi = b_im.astype(F32)
    bbr = cr[..., None] * br - ci[..., None] * bi
    bbi = cr[..., None] * bi + ci[..., None] * br
    n = jnp.arange(t + 1, dtype=F32)[:, None, None, None]
    pmag = jnp.exp(n * (lr * dt)[None])
    pr = pmag * jnp.cos(n * (li * dt)[None])
    pi = pmag * jnp.sin(n * (li * dt)[None])
    cre = c_re.astype(F32)
    cim = c_im.astype(F32)

    gs = SSM_LANES // hdim
    n_sg = g // gs

    def selector(r, inner):
        e = np.zeros((gs, r, gs, inner, r, inner), np.float32)
        for grp in range(gs):
            e[grp, :, grp] = np.eye(r * inner, dtype=np.float32).reshape(r, inner, r, inner)
        return jnp.asarray(e.reshape(gs, r * gs * inner, r * inner), dtype=BF16)

    def expand(small):
        _, r1, i1, r2, i2 = small.shape
        blocks = small.reshape(n_sg, gs, r1 * i1, r2 * i2).astype(BF16)
        left = jnp.einsum('grk,sgkl->sgrl', selector(r1, i1), blocks,
                          preferred_element_type=F32).astype(BF16)
        return jnp.einsum('sgrl,gcl->src', left, selector(r2, i2),
                          preferred_element_type=F32).astype(BF16)

    def summ(direction, powers):
        er = pr[powers, direction][..., None] * bbr[direction][None] \
            - pi[powers, direction][..., None] * bbi[direction][None]
        ei = pr[powers, direction][..., None] * bbi[direction][None] \
            + pi[powers, direction][..., None] * bbr[direction][None]
        tr = lambda e: jnp.transpose(e, (1, 0, 3, 2))
        return expand(jnp.stack([tr(er), tr(ei)], axis=3))
    steps = jnp.arange(t)
    p_f = summ(0, t - 1 - steps)
    p_b = summ(1, steps)

    row = lambda v: v.reshape(n_sg, gs * p)
    a_row = jnp.concatenate([row(pr[t, 0]), row(pr[t, 1]), row(pi[t, 0]), row(pi[t, 1])],
                            axis=-1)[:, None, :]

    def kern(direction):
        er = pr[:t, direction][..., None] * bbr[direction][None] \
            - pi[:t, direction][..., None] * bbi[direction][None]
        ei = pr[:t, direction][..., None] * bbi[direction][None] \
            + pi[:t, direction][..., None] * bbr[direction][None]
        return (jnp.einsum('gop,ngpi->ngoi', cre[direction], er)
                - jnp.einsum('gop,ngpi->ngoi', cim[direction], ei))
    kf = kern(0)
    kb = kern(1)
    s_idx = steps[:, None]
    t_idx = steps[None, :]
    lag_f = jnp.clip(t_idx - s_idx, 0, t - 1)
    lag_b = jnp.clip(s_idx - t_idx, 0, t - 1)
    mf = jnp.where((t_idx >= s_idx)[:, :, None, None, None], kf[lag_f], 0.0)
    mb = jnp.where((s_idx >= t_idx)[:, :, None, None, None], kb[lag_b], 0.0)
    dd = d.astype(F32).reshape(g, hdim)
    eye_t = jnp.eye(t, dtype=F32)[:, :, None, None, None]
    eye_c = jnp.eye(hdim, dtype=F32)[None, None, None]
    m5 = mf + mb + eye_t * eye_c * dd[None, None, :, :, None]
    m_mat = expand(jnp.transpose(m5, (2, 0, 4, 1, 3)))

    def state_out(direction, powers):
        prn = pr[powers, direction]
        pin = pi[powers, direction]
        wr = cre[direction][None] * prn[:, :, None, :] - cim[direction][None] * pin[:, :, None, :]
        wi = -cre[direction][None] * pin[:, :, None, :] - cim[direction][None] * prn[:, :, None, :]
        tr = lambda w: jnp.transpose(w, (1, 3, 0, 2))
        return expand(jnp.stack([tr(wr), tr(wi)], axis=1))
    q_f = state_out(0, steps + 1)
    q_b = state_out(1, t - steps)
    return p_f, p_b, m_mat, q_f, q_b, a_row


def _ssm_kernel(tbk, usf_ref, usb_ref, pf_ref, pb_ref, m_ref, qf_ref, qb_ref, a_ref, h0_ref,
                yf_ref, yb_ref, hend_ref, xf_ref, xb_ref, sf_ref, sb_ref, hf_ref, hb_ref, h_ref):
    t = SSM_CHUNK
    lanes = SSM_LANES

    @pl.when(pl.program_id(1) == 0)
    def _init():
        h_ref[...] = h0_ref[0]

    for tt in range(t):
        cols = slice(tt * lanes, (tt + 1) * lanes)
        xf_ref[:, cols] = usf_ref[pl.ds(tt, tbk, stride=t), :].astype(BF16)
        xb_ref[:, cols] = usb_ref[pl.ds(tt, tbk, stride=t), :].astype(BF16)
    sf_ref[...] = jnp.dot(xf_ref[...], pf_ref[0], preferred_element_type=F32)
    sb_ref[...] = jnp.dot(xb_ref[...], pb_ref[0], preferred_element_type=F32)

    w = a_ref.shape[2] // 4
    a = a_ref[0]
    arf, arb, aif, aib = (a[:, i * w:(i + 1) * w] for i in range(4))
    h = h_ref[...]

    def body(k, carry):
        hrf, hrb, hif, hib = carry
        kb = tbk - 1 - k
        hf_ref[pl.ds(k, 1), 0:w] = hrf
        hf_ref[pl.ds(k, 1), w:2 * w] = hif
        hb_ref[pl.ds(kb, 1), 0:w] = hrb
        hb_ref[pl.ds(kb, 1), w:2 * w] = hib
        sf = sf_ref[pl.ds(k, 1), :]
        sb = sb_ref[pl.ds(kb, 1), :]
        return (arf * hrf - aif * hif + sf[:, 0:w], arb * hrb - aib * hib + sb[:, 0:w],
                arf * hif + aif * hrf + sf[:, w:2 * w], arb * hib + aib * hrb + sb[:, w:2 * w])

    carry = lax.fori_loop(0, tbk, body, tuple(h[:, i * w:(i + 1) * w] for i in range(4)),
                          unroll=8)
    h = jnp.concatenate(carry, axis=1)
    h_ref[...] = h
    hend_ref[0] = h

    yf = (jnp.dot(xf_ref[...], m_ref[0], preferred_element_type=F32)
          + jnp.dot(hf_ref[...].astype(BF16), qf_ref[0], preferred_element_type=F32))
    yb = jnp.dot(hb_ref[...].astype(BF16), qb_ref[0], preferred_element_type=F32)
    for tt in range(t):
        cols = slice(tt * lanes, (tt + 1) * lanes)
        yf_ref[pl.ds(tt, tbk, stride=t), :] = yf[:, cols]
        yb_ref[pl.ds(tt, tbk, stride=t), :] = yb[:, cols]


def _ssm_apply(us, tables, h0):
    p_f, p_b, m_mat, q_f, q_b, a_row = tables
    l, w = us.shape
    n_sg = w // SSM_LANES
    n = l // SSM_CHUNK
    tbk = min(256, n)
    nb = n // tbk
    rows = tbk * SSM_CHUNK
    kx = SSM_CHUNK * SSM_LANES
    sw = p_f.shape[2]
    hw = a_row.shape[2]
    per_sg = lambda a: pl.BlockSpec((1,) + a.shape[1:], lambda s, i: (s, 0, 0))
    fwd = pl.BlockSpec((rows, SSM_LANES), lambda s, i: (i, s))
    bwd = pl.BlockSpec((rows, SSM_LANES), lambda s, i: (nb - 1 - i, s))
    return pl.pallas_call(
        functools.partial(_ssm_kernel, tbk),
        grid=(n_sg, nb),
        in_specs=[fwd, bwd, per_sg(p_f), per_sg(p_b), per_sg(m_mat), per_sg(q_f), per_sg(q_b),
                  per_sg(a_row), per_sg(h0)],
        out_specs=[fwd, bwd, per_sg(h0)],
        out_shape=[jax.ShapeDtypeStruct((l, w), F32), jax.ShapeDtypeStruct((l, w), F32),
                   jax.ShapeDtypeStruct(h0.shape, F32)],
        scratch_shapes=[pltpu.VMEM((tbk, kx), BF16), pltpu.VMEM((tbk, kx), BF16),
                        pltpu.VMEM((tbk, sw), F32), pltpu.VMEM((tbk, sw), F32),
                        pltpu.VMEM((tbk, sw), F32), pltpu.VMEM((tbk, sw), F32),
                        pltpu.VMEM((1, hw), F32)],
        compiler_params=_cparams(("arbitrary", "arbitrary")),
        name="ssm_scan",
    )(us, us, p_f, p_b, m_mat, q_f, q_b, a_row, h0)


def _dft_tables(l, n1, hw):
    n2 = l // n1
    scale = 1.0 / math.sqrt(l * hw)
    if n1 == 1:
        t1 = None
    else:
        ang = 2.0 * np.pi * ((np.arange(n1)[:, None] * np.arange(n1)[None, :]) % n1) / n1
        c, s = np.cos(ang), np.sin(ang)
        t1 = jnp.asarray(np.block([[c, -s], [-s, -c]]), dtype=BF16)
    k1 = np.arange(n1)[:, None, None]
    k2 = np.arange(n2)[None, :, None]
    m = np.arange(n2)[None, None, :]
    ang = 2.0 * np.pi * ((m * (n1 * k2 + k1)) % l) / l
    sign = -1.0 if n1 == 1 else 1.0
    g = np.concatenate([np.cos(ang), sign * np.sin(ang)], axis=-1) * scale
    return t1, jnp.asarray(g, dtype=BF16)


def _dft1_kernel(t_ref, x_ref, z_ref):
    z_ref[...] = jnp.dot(t_ref[...], x_ref[...], preferred_element_type=F32).astype(z_ref.dtype)


def _dft1_call(t1, ab2d):
    r, n = ab2d.shape
    tn = min(4096, n)
    return pl.pallas_call(
        _dft1_kernel,
        grid=(n // tn,),
        in_specs=[pl.BlockSpec((r, r), lambda j: (0, 0)),
                  pl.BlockSpec((r, tn), lambda j: (0, j))],
        out_specs=pl.BlockSpec((r, tn), lambda j: (0, j)),
        out_shape=jax.ShapeDtypeStruct((r, n), BF16),
        compiler_params=_cparams(("arbitrary",)),
        name="dft_stage1",
    )(t1, ab2d)


def _dft2_kernel(bt, zr_ref, zi_ref, g_ref, w_ref, o_ref):
    fw = w_ref.shape[1]
    for b in range(bt):
        z = jnp.concatenate([zr_ref[b], zi_ref[b]], axis=0)
        x = jnp.dot(g_ref[b], z, preferred_element_type=F32)
        o_ref[:, b * fw:(b + 1) * fw] = jnp.dot(
            x.astype(BF16), w_ref[0], preferred_element_type=F32).astype(o_ref.dtype)


def _dft2_call(z3, g, fourier_w, layer):
    n1x2, n2, fw = z3.shape
    n1 = n1x2 // 2
    bt = min(8, n1)
    return pl.pallas_call(
        functools.partial(_dft2_kernel, bt),
        grid=(n1 // bt,),
        in_specs=[pl.BlockSpec((bt, n2, fw), lambda b: (b, 0, 0)),
                  pl.BlockSpec((bt, n2, fw), lambda b: (b + n1 // bt, 0, 0)),
                  pl.BlockSpec((bt, n2, 2 * n2), lambda b: (b, 0, 0)),
                  _layer_block(fourier_w, layer)],
        out_specs=pl.BlockSpec((n2, bt * fw), lambda b: (0, b)),
        out_shape=jax.ShapeDtypeStruct((n2, n1 * fw), BF16),
        compiler_params=_cparams(("arbitrary",)),
        name="dft_stage2",
    )(z3, z3, g, fourier_w)


def _fourier_apply(ab, tables, fourier_w, layer, n1):
    t1, g = tables
    _, l, fw = ab.shape
    n2 = l // n1
    if n1 == 1:
        z3 = ab
    else:
        z3 = _dft1_call(t1, ab.reshape(2 * n1, n2 * fw)).reshape(2 * n1, n2, fw)
    return _dft2_call(z3, g, fourier_w, layer).reshape(l, fw)


def _outproj_kernel(attn_ref, yf_ref, yb_ref, four_ref, x_ref, wo_ref, gw_ref, gb_ref, npost_ref,
                    gate_ref, npre_ref, sh_ref, sc_ref, xo_ref, h_ref):
    ys = yf_ref[...] + yb_ref[...]
    gl = 0.5 * ys * (1.0 + jnp.tanh(math.sqrt(2.0 / math.pi) * (ys + 0.044715 * (ys * ys * ys))))
    z = jnp.dot(gl.astype(BF16), gw_ref[0], preferred_element_type=F32) + gb_ref[...]
    ssm = (gl * jax.nn.sigmoid(z)).astype(BF16)
    cat = jnp.concatenate([attn_ref[...], ssm, four_ref[...]], axis=-1)
    y = jnp.dot(cat, wo_ref[0], preferred_element_type=F32)
    xn = x_ref[...] + gate_ref[...] * _rms(y, npost_ref[...])
    xo_ref[...] = xn
    h_ref[...] = (_rms(xn, npre_ref[...]) * (1.0 + sc_ref[...]) + sh_ref[...]).astype(BF16)


def _outproj_call(attn, yf, yb, four, x, w_out, glu_w, layer, glu_b, npost, gate, npre, shift,
                  scale):
    l, d = x.shape
    tm = min(512, l)
    row = lambda i: (i, 0)
    fix = lambda i: (0, 0)
    vec = pl.BlockSpec((1, d), fix)
    return pl.pallas_call(
        _outproj_kernel,
        grid=(l // tm,),
        in_specs=[pl.BlockSpec((tm, attn.shape[1]), row),
                  pl.BlockSpec((tm, yf.shape[1]), row),
                  pl.BlockSpec((tm, yb.shape[1]), row),
                  pl.BlockSpec((tm, four.shape[1]), row),
                  pl.BlockSpec((tm, d), row),
                  _layer_block(w_out, layer),
                  _layer_block(glu_w, layer),
                  pl.BlockSpec((1, glu_w.shape[2]), fix),
                  vec, vec, vec, vec, vec],
        out_specs=[pl.BlockSpec((tm, d), row), pl.BlockSpec((tm, d), row)],
        out_shape=[jax.ShapeDtypeStruct((l, d), F32), jax.ShapeDtypeStruct((l, d), BF16)],
        compiler_params=_cparams(("arbitrary",)),
        name="out_proj",
    )(attn, yf, yb, four, x, w_out, glu_w, glu_b, npost, gate, npre, shift, scale)


def _ffn_kernel(nj1, nj2, th, tn, h_ref, x_ref, wg_ref, wu_ref, wd_ref, npost_ref, gate_ref, o_ref,
                act_ref, tmp_ref):
    j = pl.program_id(1)

    @pl.when(j < nj1)
    def _up():
        h = h_ref[...]
        a = jnp.dot(h, wg_ref[0], preferred_element_type=F32)
        u = jnp.dot(h, wu_ref[0], preferred_element_type=F32)
        tmp_ref[...] = (a * jax.nn.sigmoid(a) * u).astype(BF16)

    for c in range(nj1):
        @pl.when(j == c)
        def _place(c=c):
            act_ref[:, c * th:(c + 1) * th] = tmp_ref[...]

    for c in range(nj2):
        @pl.when(j == nj1 + c)
        def _down(c=c):
            o_ref[:, c * tn:(c + 1) * tn] = jnp.dot(act_ref[...], wd_ref[0],
                                                    preferred_element_type=F32)

    @pl.when(j == nj1 + nj2 - 1)
    def _finish():
        o_ref[...] = x_ref[...] + gate_ref[...] * _rms(o_ref[...], npost_ref[...])


def _ffn_call(h, x, w_gate, w_up, w_down, layer, npost, gate):
    l, d = x.shape
    fh = w_gate.shape[2]
    th, tn = FFN_TH, FFN_TN
    nj1 = fh // th
    nj2 = d // tn
    tm = min(FFN_TM, l)
    vec = pl.BlockSpec((1, d), lambda i, j: (0, 0))
    up = pl.BlockSpec((1, d, th), lambda i, j: (layer, 0, jnp.minimum(j, nj1 - 1)))
    once = pl.Buffered(1)
    return pl.pallas_call(
        functools.partial(_ffn_kernel, nj1, nj2, th, tn),
        grid=(l // tm, nj1 + nj2),
        in_specs=[pl.BlockSpec((tm, d), lambda i, j: (i, 0)),
                  pl.BlockSpec((tm, d), lambda i, j: (i, 0), pipeline_mode=once),
                  up, up,
                  pl.BlockSpec((1, fh, tn), lambda i, j: (layer, 0, jnp.maximum(j - nj1, 0))),
                  vec, vec],
        out_specs=pl.BlockSpec((tm, d), lambda i, j: (i, 0), pipeline_mode=once),
        out_shape=jax.ShapeDtypeStruct((l, d), F32),
        scratch_shapes=[pltpu.VMEM((tm, fh), BF16), pltpu.VMEM((tm, th), BF16)],
        compiler_params=_cparams(("arbitrary", "arbitrary")),
        name="ffn",
    )(h, x, w_gate, w_up, w_down, npost, gate)


def _rope_tables(l):
    t = np.arange(l)
    row = (t // GRID_W).astype(np.float32)
    col = (t % GRID_W).astype(np.float32)
    freqs = np.float32(ROPE_THETA) ** (-np.arange(ROPE_PAIRS, dtype=np.float32) / np.float32(ROPE_PAIRS))
    ang_r = (row[:, None] * freqs).astype(np.float32)
    ang_c = (col[:, None] * freqs).astype(np.float32)
    cos = np.concatenate([np.cos(ang_r)] * 2 + [np.cos(ang_c)] * 2, axis=-1)
    sin = np.concatenate([-np.sin(ang_r), np.sin(ang_r), -np.sin(ang_c), np.sin(ang_c)], axis=-1)
    return jnp.asarray(cos, dtype=F32), jnp.asarray(sin, dtype=F32)


def _channel_dft_table(fw):
    hw = FOURIER_HEAD_DIM
    ang = 2.0 * np.pi * ((np.arange(hw)[:, None] * np.arange(hw)[None, :]) % hw) / hw
    eye = np.eye(fw // hw)
    return jnp.asarray(np.concatenate([np.kron(eye, np.cos(ang)), np.kron(eye, np.sin(ang))], axis=1),
                       dtype=BF16)


def _dft_split(l):
    n1 = 1
    while n1 * n1 < l:
        n1 *= 2
    return n1 if (l >= 1024 and n1 * n1 == l) else 1


def kernel(x, c, ctx, c_ctx, ada_w, ada_b, norm_mix_pre, norm_mix_post, norm_ffn_pre, norm_ffn_post, w_in, q_norm, k_norm, ssm_lam_re, ssm_lam_im, ssm_log_dt, ssm_b_re, ssm_b_im, ssm_c_re, ssm_c_im, ssm_d, ssm_glu_w, ssm_glu_b, fourier_w, w_out, ffn_w_gate, ffn_w_up, ffn_w_down):
    depth = ada_w.shape[0]
    _, l, d = x.shape
    n_ctx = ctx.shape[1]
    fw = fourier_w.shape[1]
    sw = ssm_d.shape[1]

    cond = jnp.zeros((8, d), F32).at[0].set(c[0]).at[1].set(c_ctx)
    mod = _ada_call(cond, ada_w, ada_b)

    cos, sin = _rope_tables(l)
    zero_tab = jnp.zeros((n_ctx, HEAD_DIM), F32)
    cs = _channel_dft_table(fw)
    n1 = _dft_split(l)
    n1c = _dft_split(n_ctx)
    dft_x = _dft_tables(l, n1, FOURIER_HEAD_DIM)
    dft_c = _dft_tables(n_ctx, n1c, FOURIER_HEAD_DIM)
    h0 = jnp.zeros((sw // SSM_LANES, 1, 4 * (SSM_LANES // SSM_GROUP) * SSM_STATE), F32)

    w_in_b = w_in.astype(BF16)
    w_out_b = w_out.astype(BF16)
    glu_w_b = ssm_glu_w.astype(BF16)
    four_w_b = fourier_w.astype(BF16)
    wg_b = ffn_w_gate.astype(BF16)
    wu_b = ffn_w_up.astype(BF16)
    wd_b = ffn_w_down.astype(BF16)

    xs = x[0]
    xc = ctx[0]
    for layer in range(depth):
        need_ctx = layer < depth - 1
        vecs = lambda r: [mod[layer, r:r + 1, i * d:(i + 1) * d] for i in range(6)]
        sh_m, sc_m, g_m, sh_f, sc_f, g_f = vecs(0)
        shc_m, scc_m, gc_m, shc_f, scc_f, gc_f = vecs(1)
        row = lambda a: a[layer].reshape(1, -1)
        tables = _ssm_tables(ssm_lam_re[layer], ssm_lam_im[layer], ssm_log_dt[layer],
                             ssm_b_re[layer], ssm_b_im[layer], ssm_c_re[layer], ssm_c_im[layer],
                             ssm_d[layer])

        qc, kc, vtc, usc, abc = _inproj_call(xc, row(norm_mix_pre), shc_m, scc_m, w_in_b, layer,
                                             row(q_norm), row(k_norm), zero_tab, zero_tab, cs, False)
        yfc, ybc, hend_c = _ssm_apply(usc, tables, h0)

        q, k, vt, us, ab = _inproj_call(xs, row(norm_mix_pre), sh_m, sc_m, w_in_b, layer,
                                        row(q_norm), row(k_norm), cos, sin, cs, True)
        attn = _attn_dispatch(q, jnp.concatenate([k, kc], axis=0),
                              jnp.concatenate([vt, vtc], axis=1))
        yf, yb, _ = _ssm_apply(us, tables, hend_c)
        four = _fourier_apply(ab, dft_x, four_w_b, layer, n1)
        xs, hs = _outproj_call(attn, yf, yb, four, xs, w_out_b, glu_w_b, layer, row(ssm_glu_b),
                               row(norm_mix_post), g_m, row(norm_ffn_pre), sh_f, sc_f)
        xs = _ffn_call(hs, xs, wg_b, wu_b, wd_b, layer, row(norm_ffn_post), g_f)

        if need_ctx:
            attn_c = _attn_call(qc, kc, vtc)
            four_c = _fourier_apply(abc, dft_c, four_w_b, layer, n1c)
            xc, hc = _outproj_call(attn_c, yfc, ybc, four_c, xc, w_out_b, glu_w_b, layer,
                                   row(ssm_glu_b), row(norm_mix_post), gc_m, row(norm_ffn_pre),
                                   shc_f, scc_f)
            xc = _ffn_call(hc, xc, wg_b, wu_b, wd_b, layer, row(norm_ffn_post), gc_f)
    return xs[None]
```

```python
import functools
import math

import numpy as np
import jax
import jax.numpy as jnp
from jax import lax
from jax.experimental import pallas as pl
from jax.experimental.pallas import tpu as pltpu

F32 = jnp.float32
BF16 = jnp.bfloat16

HEAD_DIM = 128
N_KV_HEADS = 2
Q_PER_KV = 4
GRID_W = 64
ROPE_THETA = 10000.0
ROPE_PAIRS = HEAD_DIM // 4
SSM_GROUP = 16
SSM_STATE = 64
FOURIER_HEAD_DIM = 128
NORM_EPS = 1e-6
SSM_CHUNK = 8
SSM_LANES = 128
LOG2E = 1.4426950408889634
ATTN_TQ = 512
ATTN_TK = 1280
ATTN_MAX_GAP = 100.0
ATTN_SUB = 256
FFN_TM = 1024
FFN_TH = 512
FFN_TN = 256
VT_PAD = 16
VT_ROWS = HEAD_DIM + VT_PAD

VMEM_LIMIT_MB = 56


def _cparams(semantics, vmem_mb=VMEM_LIMIT_MB):
    return pltpu.CompilerParams(dimension_semantics=semantics,
                                vmem_limit_bytes=vmem_mb * 2 ** 20)


def _rms(x, gain):
    return x * lax.rsqrt(jnp.mean(x * x, axis=-1, keepdims=True) + NORM_EPS) * gain


def _ada_kernel(s_ref, w_ref, b_ref, o_ref):
    s = s_ref[...]
    act = (s * jax.nn.sigmoid(s)).astype(BF16)
    o_ref[0] = jnp.dot(act, w_ref[0].astype(BF16), preferred_element_type=F32) + b_ref[0]


def _ada_call(cond, ada_w, ada_b):
    depth, d, n = ada_w.shape
    tn = 1024
    return pl.pallas_call(
        _ada_kernel,
        grid=(depth, n // tn),
        in_specs=[pl.BlockSpec((8, d), lambda l, j: (0, 0)),
                  pl.BlockSpec((1, d, tn), lambda l, j: (l, 0, j)),
                  pl.BlockSpec((1, 1, tn), lambda l, j: (l, 0, j))],
        out_specs=pl.BlockSpec((1, 8, tn), lambda l, j: (l, 0, j)),
        out_shape=jax.ShapeDtypeStruct((depth, 8, n), F32),
        compiler_params=_cparams(("arbitrary", "arbitrary")),
        name="ada_ln",
    )(cond, ada_w, ada_b.reshape(depth, 1, n))


def _inproj_kernel(rope, x_ref, g_ref, sh_ref, sc_ref, w_ref, qn_ref, kn_ref, cos_ref, sin_ref,
                   cs_ref, q_ref, k_ref, vt_ref, us_ref, ab_ref, nrm_ref):
    x = x_ref[...]
    h = (_rms(x, g_ref[...]) * (1.0 + sc_ref[...]) + sh_ref[...]).astype(BF16)
    n_q = q_ref.shape[1] // HEAD_DIM
    n_k = k_ref.shape[1] // HEAD_DIM
    q_end = n_q * HEAD_DIM
    k_end = q_end + n_k * HEAD_DIM
    v_end = k_end + n_k * HEAD_DIM
    s_end = v_end + us_ref.shape[1]
    f_end = s_end + ab_ref.shape[2]

    if rope:
        cos = cos_ref[...]
        sin = sin_ref[...]
        lane = lax.broadcasted_iota(jnp.int32, cos.shape, 1)
        low = (lane % 64) < 32

    def head(t, gain, scale):
        t = _rms(t, gain)
        if rope:
            sw = jnp.where(low, pltpu.roll(t, HEAD_DIM - 32, 1), pltpu.roll(t, 32, 1))
            t = t * cos + sw * sin
        if scale != 1.0:
            t = t * scale
        norm2 = jnp.max(jnp.sum(t * t, axis=-1, keepdims=True), axis=0, keepdims=True)
        return t.astype(BF16), norm2

    qk = jnp.dot(h, w_ref[0, :, 0:k_end], preferred_element_type=F32)
    q_scale = HEAD_DIM ** -0.5 * LOG2E
    sub = lax.broadcasted_iota(jnp.int32, nrm_ref.shape[1:], 0)
    nrm = jnp.zeros(nrm_ref.shape[1:], F32)
    for i in range(n_q):
        t, n2 = head(qk[:, i * HEAD_DIM:(i + 1) * HEAD_DIM], qn_ref[...], q_scale)
        q_ref[:, i * HEAD_DIM:(i + 1) * HEAD_DIM] = t
        nrm = jnp.where(sub == 0, jnp.maximum(nrm, n2), nrm)
    for i in range(n_k):
        t, n2 = head(qk[:, q_end + i * HEAD_DIM:q_end + (i + 1) * HEAD_DIM], kn_ref[...], 1.0)
        k_ref[:, i * HEAD_DIM:(i + 1) * HEAD_DIM] = t
        nrm = jnp.where(sub == 1 + i, n2, nrm)
    nrm_ref[0] = nrm
    rest = jnp.dot(h, w_ref[0, :, k_end:f_end], preferred_element_type=F32)
    ones_rows = (lax.broadcasted_iota(jnp.int32, (VT_PAD, ATTN_SUB), 0) == 0).astype(BF16)
    for i in range(n_k):
        vt = rest[:, i * HEAD_DIM:(i + 1) * HEAD_DIM].T.astype(BF16)
        for b in range(vt_ref.shape[1]):
            vt_ref[i, b, 0:HEAD_DIM, :] = vt[:, b * ATTN_SUB:(b + 1) * ATTN_SUB]
            vt_ref[i, b, HEAD_DIM:VT_ROWS, :] = ones_rows
    us_ref[...] = rest[:, v_end - k_end:s_end - k_end]
    uf = rest[:, s_end - k_end:f_end - k_end].astype(BF16)
    ab = jnp.dot(uf, cs_ref[...], preferred_element_type=F32)
    fw = ab_ref.shape[2]
    ab_ref[0] = ab[:, 0:fw].astype(BF16)
    ab_ref[1] = ab[:, fw:2 * fw].astype(BF16)


def _layer_block(w, layer):
    return pl.BlockSpec((1,) + w.shape[1:], lambda *_: (layer, 0, 0))


def _inproj_call(x, gain, shift, scale, w_in, layer, qn, kn, cos, sin, cs, rope):
    l, d = x.shape
    tm = min(512, l)
    qw = Q_PER_KV * N_KV_HEADS * HEAD_DIM
    kw = N_KV_HEADS * HEAD_DIM
    fw = cs.shape[0]
    sw = w_in.shape[2] - qw - 2 * kw - fw
    row = lambda i: (i, 0)
    fix = lambda i: (0, 0)
    return pl.pallas_call(
        functools.partial(_inproj_kernel, rope),
        grid=(l // tm,),
        in_specs=[pl.BlockSpec((tm, d), row),
                  pl.BlockSpec((1, d), fix), pl.BlockSpec((1, d), fix), pl.BlockSpec((1, d), fix),
                  _layer_block(w_in, layer),
                  pl.BlockSpec((1, HEAD_DIM), fix), pl.BlockSpec((1, HEAD_DIM), fix),
                  pl.BlockSpec((tm, HEAD_DIM), row), pl.BlockSpec((tm, HEAD_DIM), row),
                  pl.BlockSpec(cs.shape, fix)],
        out_specs=[pl.BlockSpec((tm, qw), row),
                   pl.BlockSpec((tm, kw), row),
                   pl.BlockSpec((N_KV_HEADS, tm // ATTN_SUB, VT_ROWS, ATTN_SUB),
                                lambda i: (0, i, 0, 0)),
                   pl.BlockSpec((tm, sw), row),
                   pl.BlockSpec((2, tm, fw), lambda i: (0, i, 0)),
                   pl.BlockSpec((1, 8, HEAD_DIM), lambda i: (i, 0, 0))],
        out_shape=[jax.ShapeDtypeStruct((l, qw), BF16),
                   jax.ShapeDtypeStruct((l, kw), BF16),
                   jax.ShapeDtypeStruct((N_KV_HEADS, l // ATTN_SUB, VT_ROWS, ATTN_SUB), BF16),
                   jax.ShapeDtypeStruct((l, sw), F32),
                   jax.ShapeDtypeStruct((2, l, fw), BF16),
                   jax.ShapeDtypeStruct((l // tm, 8, HEAD_DIM), F32)],
        compiler_params=_cparams(("arbitrary",)),
        name="in_proj",
    )(x, gain, shift, scale, w_in, qn, kn, cos, sin, cs)


def _attn_kernel(nkb, tq, q_ref, k_ref, vt_ref, o_ref, qs_ref, s0_ref, s1_ref, bm0_ref, bm1_ref,
                 m_ref, acc_ref):
    j = pl.program_id(2)
    s_refs = (s0_ref, s1_ref)
    bm_refs = (bm0_ref, bm1_ref)

    tk = k_ref.shape[0]
    sub = vt_ref.shape[3]

    def step(score_slot, absorb_slot):
        if absorb_slot is not None:
            m_prev = m_ref[...]
            m_new = jnp.maximum(m_prev, bm_refs[absorb_slot][...])
            alpha = jnp.exp2(m_prev - m_new)
            pv = jnp.zeros(acc_ref.shape, F32)
        bmax = None
        for c in range(tk // sub):
            rows = slice(c * sub, (c + 1) * sub)
            if score_slot is not None:
                s = lax.dot_general(k_ref[rows, :], qs_ref[...], (((1,), (1,)), ((), ())),
                                    preferred_element_type=F32)
                s_refs[score_slot][rows, :] = s
                cmax = jnp.max(s, axis=0, keepdims=True)
                bmax = cmax if bmax is None else jnp.maximum(bmax, cmax)
            if absorb_slot is not None:
                p = jnp.exp2((s_refs[absorb_slot][rows, :] - m_new).astype(BF16))
                pv = pv + jnp.dot(vt_ref[0, c], p, preferred_element_type=F32)
        if score_slot is not None:
            bm_refs[score_slot][...] = bmax
        if absorb_slot is not None:
            acc_ref[...] = alpha * acc_ref[...] + pv
            m_ref[...] = m_new

    @pl.when(j == 0)
    def _first():
        for h in range(Q_PER_KV):
            qs_ref[h * tq:(h + 1) * tq, :] = q_ref[:, h * HEAD_DIM:(h + 1) * HEAD_DIM]
        m_ref[...] = jnp.full(m_ref.shape, -jnp.inf, F32)
        acc_ref[...] = jnp.zeros(acc_ref.shape, F32)
        step(0, None)

    for parity in (0, 1):
        @pl.when((j > 0) & (j < nkb) & (j % 2 == parity))
        def _mid(parity=parity):
            step(parity, 1 - parity)

    @pl.when(j == nkb)
    def _last():
        step(None, (nkb - 1) % 2)
        o = acc_ref[0:HEAD_DIM, :] / acc_ref[HEAD_DIM:HEAD_DIM + 1, :]
        for h in range(Q_PER_KV):
            o_ref[:, h * HEAD_DIM:(h + 1) * HEAD_DIM] = o[:, h * tq:(h + 1) * tq].T.astype(BF16)


def _pick_block(total, unit, cap):
    best = unit
    for b in range(unit, cap + 1, unit):
        if total % b == 0:
            best = b
    return best


def _attn_bounded_kernel(unroll, tq, kn_ref, q_ref, k_ref, vt_ref, kc_ref, vtc_ref, o_ref, qs_ref,
                         acc_ref):
    sub = vt_ref.shape[3]
    for h in range(Q_PER_KV):
        qs_ref[h * tq:(h + 1) * tq, :] = q_ref[:, h * HEAD_DIM:(h + 1) * HEAD_DIM]
    qf = qs_ref[...].astype(F32)
    norm2 = lax.dot_general(jnp.ones((8, HEAD_DIM), F32), qf * qf, (((1,), (1,)), ((), ())),
                            preferred_element_type=F32)[0:1, :]
    shift = jnp.sqrt(norm2) * kn_ref[pl.program_id(0)]

    def weighted_values(keys, vt):
        s = lax.dot_general(keys, qs_ref[...], (((1,), (1,)), ((), ())),
                            preferred_element_type=F32)
        return jnp.dot(vt, jnp.exp2(s - shift).astype(BF16), preferred_element_type=F32)

    acc = None
    for c in range(vtc_ref.shape[1]):
        d = weighted_values(kc_ref[c * sub:(c + 1) * sub, :], vtc_ref[0, c])
        acc = d if acc is None else acc + d
    acc_ref[...] = acc

    def body(t, carry):
        pv = None
        for u in range(unroll):
            c = t * unroll + u
            start = pl.multiple_of(c * sub, sub)
            d = weighted_values(k_ref[pl.ds(start, sub), :], vt_ref[0, c])
            pv = d if pv is None else pv + d
        acc_ref[...] += pv
        return carry

    lax.fori_loop(0, vt_ref.shape[1] // unroll, body, 0)
    o = acc_ref[0:HEAD_DIM, :] / acc_ref[HEAD_DIM:HEAD_DIM + 1, :]
    for h in range(Q_PER_KV):
        o_ref[:, h * HEAD_DIM:(h + 1) * HEAD_DIM] = o[:, h * tq:(h + 1) * tq].T.astype(BF16)


def _attn_bounded_call(q, k, vt, kc, vtc, k_norm_max):
    l = q.shape[0]
    nblk = vt.shape[1]
    tq = min(ATTN_TQ, l)
    gw = Q_PER_KV * HEAD_DIM
    n = Q_PER_KV * tq
    unroll = max(u for u in (5, 4, 3, 2, 1) if nblk % u == 0)
    head_rows = lambda a: pl.BlockSpec((a.shape[0], HEAD_DIM), lambda g, i: (0, g))
    head_blocks = lambda a: pl.BlockSpec((1,) + a.shape[1:], lambda g, i: (g, 0, 0, 0))
    return pl.pallas_call(
        functools.partial(_attn_bounded_kernel, unroll, tq),
        grid=(N_KV_HEADS, l // tq),
        in_specs=[pl.BlockSpec(memory_space=pltpu.SMEM),
                  pl.BlockSpec((tq, gw), lambda g, i: (i, g)),
                  head_rows(k), head_blocks(vt), head_rows(kc), head_blocks(vtc)],
        out_specs=pl.BlockSpec((tq, gw), lambda g, i: (i, g)),
        out_shape=jax.ShapeDtypeStruct(q.shape, BF16),
        scratch_shapes=[pltpu.VMEM((n, HEAD_DIM), BF16), pltpu.VMEM((VT_ROWS, n), F32)],
        compiler_params=_cparams(("arbitrary", "arbitrary")),
        name="attention_bounded",
    )(k_norm_max, q, k, vt, kc, vtc)


def _attn_dispatch(q, k, vt, kc, vtc, norms):
    qn = jnp.sqrt(jnp.max(norms[:, 0, 0]))
    kn = jnp.sqrt(jnp.max(norms[:, 1:1 + N_KV_HEADS, 0], axis=0))
    safe = 2.0 * qn * jnp.max(kn) <= ATTN_MAX_GAP
    return lax.cond(safe,
                    lambda: _attn_bounded_call(q, k, vt, kc, vtc, kn),
                    lambda: _attn_call(q, jnp.concatenate([k, kc], axis=0),
                                       jnp.concatenate([vt, vtc], axis=1)))


def _attn_call(q, k_all, vt_all):
    l = q.shape[0]
    s_len = k_all.shape[0]
    sub = vt_all.shape[3]
    tq = min(ATTN_TQ, l)
    tk = _pick_block(s_len, sub, ATTN_TK)
    nkb = s_len // tk
    gw = Q_PER_KV * HEAD_DIM
    n = Q_PER_KV * tq
    return pl.pallas_call(
        functools.partial(_attn_kernel, nkb, tq),
        grid=(N_KV_HEADS, l // tq, nkb + 1),
        in_specs=[pl.BlockSpec((tq, gw), lambda g, i, j: (i, g)),
                  pl.BlockSpec((tk, HEAD_DIM), lambda g, i, j: (jnp.minimum(j, nkb - 1), g)),
                  pl.BlockSpec((1, tk // sub, VT_ROWS, sub),
                               lambda g, i, j: (g, jnp.maximum(j - 1, 0), 0, 0))],
        out_specs=pl.BlockSpec((tq, gw), lambda g, i, j: (i, g)),
        out_shape=jax.ShapeDtypeStruct(q.shape, BF16),
        scratch_shapes=[pltpu.VMEM((n, HEAD_DIM), BF16),
                        pltpu.VMEM((tk, n), F32),
                        pltpu.VMEM((tk, n), F32),
                        pltpu.VMEM((1, n), F32),
                        pltpu.VMEM((1, n), F32),
                        pltpu.VMEM((1, n), F32),
                        pltpu.VMEM((VT_ROWS, n), F32)],
        compiler_params=_cparams(("arbitrary", "arbitrary", "arbitrary")),
        name="attention",
    )(q, k_all, vt_all)


def _ssm_tables(lam_re, lam_im, log_dt, b_re, b_im, c_re, c_im, d):
    t = SSM_CHUNK
    g, p = lam_re.shape[1:]
    hdim = b_re.shape[-1]
    lr = lam_re.astype(F32)
    li = lam_im.astype(F32)
    dt = jnp.exp(log_dt.astype(F32))[..., None]
    mag = jnp.exp(lr * dt)
    ar = mag * jnp.cos(li * dt)
    ai = mag * jnp.sin(li * dt)
    den = lr * lr + li * li
    cr = ((ar - 1.0) * lr + ai * li) / den
    ci = (ai * lr - (ar - 1.0) * li) / den
    br = b_re.astype(F32)
    bi = b_im.astype(F32)
    bbr = cr[..., None] * br - ci[..., None] * bi
    bbi = cr[..., None] * bi + ci[..., None] * br
    n = jnp.arange(t + 1, dtype=F32)[:, None, None, None]
    pmag = jnp.exp(n * (lr * dt)[None])
    pr = pmag * jnp.cos(n * (li * dt)[None])
    pi = pmag * jnp.sin(n * (li * dt)[None])
    cre = c_re.astype(F32)
    cim = c_im.astype(F32)

    gs = SSM_LANES // hdim
    n_sg = g // gs

    def selector(r, inner):
        e = np.zeros((gs, r, gs, inner, r, inner), np.float32)
        for grp in range(gs):
            e[grp, :, grp] = np.eye(r * inner, dtype=np.float32).reshape(r, inner, r, inner)
        return jnp.asarray(e.reshape(gs, r * gs * inner, r * inner), dtype=BF16)

    def expand(small):
        _, r1, i1, r2, i2 = small.shape
        blocks = small.reshape(n_sg, gs, r1 * i1, r2 * i2).astype(BF16)
        left = jnp.einsum('grk,sgkl->sgrl', selector(r1, i1), blocks,
                          preferred_element_type=F32).astype(BF16)
        return jnp.einsum('sgrl,gcl->src', left, selector(r2, i2),
                          preferred_element_type=F32).astype(BF16)

    def summ(direction, powers):
        er = pr[powers, direction][..., None] * bbr[direction][None] \
            - pi[powers, direction][..., None] * bbi[direction][None]
        ei = pr[powers, direction][..., None] * bbi[direction][None] \
            + pi[powers, direction][..., None] * bbr[direction][None]
        tr = lambda e: jnp.transpose(e, (1, 0, 3, 2))
        return expand(jnp.stack([tr(er), tr(ei)], axis=3))
    steps = jnp.arange(t)
    p_f = summ(0, t - 1 - steps)
    p_b = summ(1, steps)

    row = lambda v: v.reshape(n_sg, gs * p)
    a_row = jnp.concatenate([row(pr[t, 0]), row(pr[t, 1]), row(pi[t, 0]), row(pi[t, 1])],
                            axis=-1)[:, None, :]

    def kern(direction):
        er = pr[:t, direction][..., None] * bbr[direction][None] \
            - pi[:t, direction][..., None] * bbi[direction][None]
        ei = pr[:t, direction][..., None] * bbi[direction][None] \
            + pi[:t, direction][..., None] * bbr[direction][None]
        return (jnp.einsum('gop,ngpi->ngoi', cre[direction], er)
                - jnp.einsum('gop,ngpi->ngoi', cim[direction], ei))
    kf = kern(0)
    kb = kern(1)
    s_idx = steps[:, None]
    t_idx = steps[None, :]
    lag_f = jnp.clip(t_idx - s_idx, 0, t - 1)
    lag_b = jnp.clip(s_idx - t_idx, 0, t - 1)
    mf = jnp.where((t_idx >= s_idx)[:, :, None, None, None], kf[lag_f], 0.0)
    mb = jnp.where((s_idx >= t_idx)[:, :, None, None, None], kb[lag_b], 0.0)
    dd = d.astype(F32).reshape(g, hdim)
    eye_t = jnp.eye(t, dtype=F32)[:, :, None, None, None]
    eye_c = jnp.eye(hdim, dtype=F32)[None, None, None]
    m5 = mf + mb + eye_t * eye_c * dd[None, None, :, :, None]
    m_mat = expand(jnp.transpose(m5, (2, 0, 4, 1, 3)))

    def state_out(direction, powers):
        prn = pr[powers, direction]
        pin = pi[powers, direction]
        wr = cre[direction][None] * prn[:, :, None, :] - cim[direction][None] * pin[:, :, None, :]
        wi = -cre[direction][None] * pin[:, :, None, :] - cim[direction][None] * prn[:, :, None, :]
        tr = lambda w: jnp.transpose(w, (1, 3, 0, 2))
        return expand(jnp.stack([tr(wr), tr(wi)], axis=1))
    q_f = state_out(0, steps + 1)
    q_b = state_out(1, t - steps)
    return p_f, p_b, m_mat, q_f, q_b, a_row


def _ssm_kernel(tbk, usf_ref, usb_ref, pf_ref, pb_ref, m_ref, qf_ref, qb_ref, a_ref, h0_ref,
                yf_ref, yb_ref, hend_ref, xf_ref, xb_ref, sf_ref, sb_ref, hf_ref, hb_ref, h_ref):
    t = SSM_CHUNK
    lanes = SSM_LANES

    @pl.when(pl.program_id(1) == 0)
    def _init():
        h_ref[...] = h0_ref[0]

    for tt in range(t):
        cols = slice(tt * lanes, (tt + 1) * lanes)
        xf_ref[:, cols] = usf_ref[pl.ds(tt, tbk, stride=t), :].astype(BF16)
        xb_ref[:, cols] = usb_ref[pl.ds(tt, tbk, stride=t), :].astype(BF16)
    sf_ref[...] = jnp.dot(xf_ref[...], pf_ref[0], preferred_element_type=F32)
    sb_ref[...] = jnp.dot(xb_ref[...], pb_ref[0], preferred_element_type=F32)

    w = a_ref.shape[2] // 4
    a = a_ref[0]
    arf, arb, aif, aib = (a[:, i * w:(i + 1) * w] for i in range(4))
    h = h_ref[...]

    def body(k, carry):
        hrf, hrb, hif, hib = carry
        kb = tbk - 1 - k
        hf_ref[pl.ds(k, 1), 0:w] = hrf
        hf_ref[pl.ds(k, 1), w:2 * w] = hif
        hb_ref[pl.ds(kb, 1), 0:w] = hrb
        hb_ref[pl.ds(kb, 1), w:2 * w] = hib
        sf = sf_ref[pl.ds(k, 1), :]
        sb = sb_ref[pl.ds(kb, 1), :]
        return (arf * hrf - aif * hif + sf[:, 0:w], arb * hrb - aib * hib + sb[:, 0:w],
                arf * hif + aif * hrf + sf[:, w:2 * w], arb * hib + aib * hrb + sb[:, w:2 * w])

    carry = lax.fori_loop(0, tbk, body, tuple(h[:, i * w:(i + 1) * w] for i in range(4)),
                          unroll=8)
    h = jnp.concatenate(carry, axis=1)
    h_ref[...] = h
    hend_ref[0] = h

    yf = (jnp.dot(xf_ref[...], m_ref[0], preferred_element_type=F32)
          + jnp.dot(hf_ref[...].astype(BF16), qf_ref[0], preferred_element_type=F32))
    yb = jnp.dot(hb_ref[...].astype(BF16), qb_ref[0], preferred_element_type=F32)
    for tt in range(t):
        cols = slice(tt * lanes, (tt + 1) * lanes)
        yf_ref[pl.ds(tt, tbk, stride=t), :] = yf[:, cols]
        yb_ref[pl.ds(tt, tbk, stride=t), :] = yb[:, cols]


def _ssm_apply(us, tables, h0):
    p_f, p_b, m_mat, q_f, q_b, a_row = tables
    l, w = us.shape
    n_sg = w // SSM_LANES
    n = l // SSM_CHUNK
    tbk = min(256, n)
    nb = n // tbk
    rows = tbk * SSM_CHUNK
    kx = SSM_CHUNK * SSM_LANES
    sw = p_f.shape[2]
    hw = a_row.shape[2]
    per_sg = lambda a: pl.BlockSpec((1,) + a.shape[1:], lambda s, i: (s, 0, 0))
    fwd = pl.BlockSpec((rows, SSM_LANES), lambda s, i: (i, s))
    bwd = pl.BlockSpec((rows, SSM_LANES), lambda s, i: (nb - 1 - i, s))
    return pl.pallas_call(
        functools.partial(_ssm_kernel, tbk),
        grid=(n_sg, nb),
        in_specs=[fwd, bwd, per_sg(p_f), per_sg(p_b), per_sg(m_mat), per_sg(q_f), per_sg(q_b),
                  per_sg(a_row), per_sg(h0)],
        out_specs=[fwd, bwd, per_sg(h0)],
        out_shape=[jax.ShapeDtypeStruct((l, w), F32), jax.ShapeDtypeStruct((l, w), F32),
                   jax.ShapeDtypeStruct(h0.shape, F32)],
        scratch_shapes=[pltpu.VMEM((tbk, kx), BF16), pltpu.VMEM((tbk, kx), BF16),
                        pltpu.VMEM((tbk, sw), F32), pltpu.VMEM((tbk, sw), F32),
                        pltpu.VMEM((tbk, sw), F32), pltpu.VMEM((tbk, sw), F32),
                        pltpu.VMEM((1, hw), F32)],
        compiler_params=_cparams(("arbitrary", "arbitrary")),
        name="ssm_scan",
    )(us, us, p_f, p_b, m_mat, q_f, q_b, a_row, h0)


def _dft_tables(l, n1, hw):
    n2 = l // n1
    scale = 1.0 / math.sqrt(l * hw)
    if n1 == 1:
        t1 = None
    else:
        ang = 2.0 * np.pi * ((np.arange(n1)[:, None] * np.arange(n1)[None, :]) % n1) / n1
        c, s = np.cos(ang), np.sin(ang)
        t1 = jnp.asarray(np.block([[c, -s], [-s, -c]]), dtype=BF16)
    k1 = np.arange(n1)[:, None, None]
    k2 = np.arange(n2)[None, :, None]
    m = np.arange(n2)[None, None, :]
    ang = 2.0 * np.pi * ((m * (n1 * k2 + k1)) % l) / l
    sign = -1.0 if n1 == 1 else 1.0
    g = np.concatenate([np.cos(ang), sign * np.sin(ang)], axis=-1) * scale
    return t1, jnp.asarray(g, dtype=BF16)


def _dft1_kernel(t_ref, x_ref, z_ref):
    z_ref[...] = jnp.dot(t_ref[...], x_ref[...], preferred_element_type=F32).astype(z_ref.dtype)


def _dft1_call(t1, ab2d):
    r, n = ab2d.shape
    tn = min(4096, n)
    return pl.pallas_call(
        _dft1_kernel,
        grid=(n // tn,),
        in_specs=[pl.BlockSpec((r, r), lambda j: (0, 0)),
                  pl.BlockSpec((r, tn), lambda j: (0, j))],
        out_specs=pl.BlockSpec((r, tn), lambda j: (0, j)),
        out_shape=jax.ShapeDtypeStruct((r, n), BF16),
        compiler_params=_cparams(("arbitrary",)),
        name="dft_stage1",
    )(t1, ab2d)


def _dft2_kernel(bt, zr_ref, zi_ref, g_ref, w_ref, o_ref):
    fw = w_ref.shape[1]
    for b in range(bt):
        z = jnp.concatenate([zr_ref[b], zi_ref[b]], axis=0)
        x = jnp.dot(g_ref[b], z, preferred_element_type=F32)
        o_ref[:, b * fw:(b + 1) * fw] = jnp.dot(
            x.astype(BF16), w_ref[0], preferred_element_type=F32).astype(o_ref.dtype)


def _dft2_call(z3, g, fourier_w, layer):
    n1x2, n2, fw = z3.shape
    n1 = n1x2 // 2
    bt = min(8, n1)
    return pl.pallas_call(
        functools.partial(_dft2_kernel, bt),
        grid=(n1 // bt,),
        in_specs=[pl.BlockSpec((bt, n2, fw), lambda b: (b, 0, 0)),
                  pl.BlockSpec((bt, n2, fw), lambda b: (b + n1 // bt, 0, 0)),
                  pl.BlockSpec((bt, n2, 2 * n2), lambda b: (b, 0, 0)),
                  _layer_block(fourier_w, layer)],
        out_specs=pl.BlockSpec((n2, bt * fw), lambda b: (0, b)),
        out_shape=jax.ShapeDtypeStruct((n2, n1 * fw), BF16),
        compiler_params=_cparams(("arbitrary",)),
        name="dft_stage2",
    )(z3, z3, g, fourier_w)


def _fourier_apply(ab, tables, fourier_w, layer, n1):
    t1, g = tables
    _, l, fw = ab.shape
    n2 = l // n1
    if n1 == 1:
        z3 = ab
    else:
        z3 = _dft1_call(t1, ab.reshape(2 * n1, n2 * fw)).reshape(2 * n1, n2, fw)
    return _dft2_call(z3, g, fourier_w, layer).reshape(l, fw)


def _outproj_kernel(attn_ref, yf_ref, yb_ref, four_ref, x_ref, wo_ref, gw_ref, gb_ref, npost_ref,
                    gate_ref, npre_ref, sh_ref, sc_ref, xo_ref, h_ref):
    ys = yf_ref[...] + yb_ref[...]
    gl = 0.5 * ys * (1.0 + jnp.tanh(math.sqrt(2.0 / math.pi) * (ys + 0.044715 * (ys * ys * ys))))
    z = jnp.dot(gl.astype(BF16), gw_ref[0], preferred_element_type=F32) + gb_ref[...]
    ssm = (gl * jax.nn.sigmoid(z)).astype(BF16)
    cat = jnp.concatenate([attn_ref[...], ssm, four_ref[...]], axis=-1)
    y = jnp.dot(cat, wo_ref[0], preferred_element_type=F32)
    xn = x_ref[...] + gate_ref[...] * _rms(y, npost_ref[...])
    xo_ref[...] = xn
    h_ref[...] = (_rms(xn, npre_ref[...]) * (1.0 + sc_ref[...]) + sh_ref[...]).astype(BF16)


def _outproj_call(attn, yf, yb, four, x, w_out, glu_w, layer, glu_b, npost, gate, npre, shift,
                  scale):
    l, d = x.shape
    tm = min(512, l)
    row = lambda i: (i, 0)
    fix = lambda i: (0, 0)
    vec = pl.BlockSpec((1, d), fix)
    return pl.pallas_call(
        _outproj_kernel,
        grid=(l // tm,),
        in_specs=[pl.BlockSpec((tm, attn.shape[1]), row),
                  pl.BlockSpec((tm, yf.shape[1]), row),
                  pl.BlockSpec((tm, yb.shape[1]), row),
                  pl.BlockSpec((tm, four.shape[1]), row),
                  pl.BlockSpec((tm, d), row),
                  _layer_block(w_out, layer),
                  _layer_block(glu_w, layer),
                  pl.BlockSpec((1, glu_w.shape[2]), fix),
                  vec, vec, vec, vec, vec],
        out_specs=[pl.BlockSpec((tm, d), row), pl.BlockSpec((tm, d), row)],
        out_shape=[jax.ShapeDtypeStruct((l, d), F32), jax.ShapeDtypeStruct((l, d), BF16)],
        compiler_params=_cparams(("arbitrary",)),
        name="out_proj",
    )(attn, yf, yb, four, x, w_out, glu_w, glu_b, npost, gate, npre, shift, scale)


def _ffn_kernel(nj1, nj2, th, tn, h_ref, x_ref, wg_ref, wu_ref, wd_ref, npost_ref, gate_ref, o_ref,
                act_ref, tmp_ref):
    j = pl.program_id(1)

    @pl.when(j < nj1)
    def _up():
        h = h_ref[...]
        a = jnp.dot(h, wg_ref[0], preferred_element_type=F32)
        u = jnp.dot(h, wu_ref[0], preferred_element_type=F32)
        tmp_ref[...] = (a * jax.nn.sigmoid(a) * u).astype(BF16)

    for c in range(nj1):
        @pl.when(j == c)
        def _place(c=c):
            act_ref[:, c * th:(c + 1) * th] = tmp_ref[...]

    for c in range(nj2):
        @pl.when(j == nj1 + c)
        def _down(c=c):
            o_ref[:, c * tn:(c + 1) * tn] = jnp.dot(act_ref[...], wd_ref[0],
                                                    preferred_element_type=F32)

    @pl.when(j == nj1 + nj2 - 1)
    def _finish():
        o_ref[...] = x_ref[...] + gate_ref[...] * _rms(o_ref[...], npost_ref[...])


def _ffn_call(h, x, w_gate, w_up, w_down, layer, npost, gate):
    l, d = x.shape
    fh = w_gate.shape[2]
    th, tn = FFN_TH, FFN_TN
    nj1 = fh // th
    nj2 = d // tn
    tm = min(FFN_TM, l)
    vec = pl.BlockSpec((1, d), lambda i, j: (0, 0))
    up = pl.BlockSpec((1, d, th), lambda i, j: (layer, 0, jnp.minimum(j, nj1 - 1)))
    once = pl.Buffered(1)
    return pl.pallas_call(
        functools.partial(_ffn_kernel, nj1, nj2, th, tn),
        grid=(l // tm, nj1 + nj2),
        in_specs=[pl.BlockSpec((tm, d), lambda i, j: (i, 0)),
                  pl.BlockSpec((tm, d), lambda i, j: (i, 0), pipeline_mode=once),
                  up, up,
                  pl.BlockSpec((1, fh, tn), lambda i, j: (layer, 0, jnp.maximum(j - nj1, 0))),
                  vec, vec],
        out_specs=pl.BlockSpec((tm, d), lambda i, j: (i, 0), pipeline_mode=once),
        out_shape=jax.ShapeDtypeStruct((l, d), F32),
        scratch_shapes=[pltpu.VMEM((tm, fh), BF16), pltpu.VMEM((tm, th), BF16)],
        compiler_params=_cparams(("arbitrary", "arbitrary")),
        name="ffn",
    )(h, x, w_gate, w_up, w_down, npost, gate)


def _rope_tables(l):
    t = np.arange(l)
    row = (t // GRID_W).astype(np.float32)
    col = (t % GRID_W).astype(np.float32)
    freqs = np.float32(ROPE_THETA) ** (-np.arange(ROPE_PAIRS, dtype=np.float32) / np.float32(ROPE_PAIRS))
    ang_r = (row[:, None] * freqs).astype(np.float32)
    ang_c = (col[:, None] * freqs).astype(np.float32)
    cos = np.concatenate([np.cos(ang_r)] * 2 + [np.cos(ang_c)] * 2, axis=-1)
    sin = np.concatenate([-np.sin(ang_r), np.sin(ang_r), -np.sin(ang_c), np.sin(ang_c)], axis=-1)
    return jnp.asarray(cos, dtype=F32), jnp.asarray(sin, dtype=F32)


def _channel_dft_table(fw):
    hw = FOURIER_HEAD_DIM
    ang = 2.0 * np.pi * ((np.arange(hw)[:, None] * np.arange(hw)[None, :]) % hw) / hw
    eye = np.eye(fw // hw)
    return jnp.asarray(np.concatenate([np.kron(eye, np.cos(ang)), np.kron(eye, np.sin(ang))], axis=1),
                       dtype=BF16)


def _dft_split(l):
    n1 = 1
    while n1 * n1 < l:
        n1 *= 2
    return n1 if (l >= 1024 and n1 * n1 == l) else 1


def kernel(x, c, ctx, c_ctx, ada_w, ada_b, norm_mix_pre, norm_mix_post, norm_ffn_pre, norm_ffn_post, w_in, q_norm, k_norm, ssm_lam_re, ssm_lam_im, ssm_log_dt, ssm_b_re, ssm_b_im, ssm_c_re, ssm_c_im, ssm_d, ssm_glu_w, ssm_glu_b, fourier_w, w_out, ffn_w_gate, ffn_w_up, ffn_w_down):
    depth = ada_w.shape[0]
    _, l, d = x.shape
    n_ctx = ctx.shape[1]
    fw = fourier_w.shape[1]
    sw = ssm_d.shape[1]

    cond = jnp.zeros((8, d), F32).at[0].set(c[0]).at[1].set(c_ctx)
    mod = _ada_call(cond, ada_w, ada_b)

    cos, sin = _rope_tables(l)
    zero_tab = jnp.zeros((n_ctx, HEAD_DIM), F32)
    cs = _channel_dft_table(fw)
    n1 = _dft_split(l)
    n1c = _dft_split(n_ctx)
    dft_x = _dft_tables(l, n1, FOURIER_HEAD_DIM)
    dft_c = _dft_tables(n_ctx, n1c, FOURIER_HEAD_DIM)
    h0 = jnp.zeros((sw // SSM_LANES, 1, 4 * (SSM_LANES // SSM_GROUP) * SSM_STATE), F32)

    w_in_b = w_in.astype(BF16)
    w_out_b = w_out.astype(BF16)
    glu_w_b = ssm_glu_w.astype(BF16)
    four_w_b = fourier_w.astype(BF16)
    wg_b = ffn_w_gate.astype(BF16)
    wu_b = ffn_w_up.astype(BF16)
    wd_b = ffn_w_down.astype(BF16)

    xs = x[0]
    xc = ctx[0]
    for layer in range(depth):
        need_ctx = layer < depth - 1
        vecs = lambda r: [mod[layer, r:r + 1, i * d:(i + 1) * d] for i in range(6)]
        sh_m, sc_m, g_m, sh_f, sc_f, g_f = vecs(0)
        shc_m, scc_m, gc_m, shc_f, scc_f, gc_f = vecs(1)
        row = lambda a: a[layer].reshape(1, -1)
        tables = _ssm_tables(ssm_lam_re[layer], ssm_lam_im[layer], ssm_log_dt[layer],
                             ssm_b_re[layer], ssm_b_im[layer], ssm_c_re[layer], ssm_c_im[layer],
                             ssm_d[layer])

        qc, kc, vtc, usc, abc, nrm_c = _inproj_call(
            xc, row(norm_mix_pre), shc_m, scc_m, w_in_b, layer, row(q_norm), row(k_norm),
            zero_tab, zero_tab, cs, False)
        yfc, ybc, hend_c = _ssm_apply(usc, tables, h0)

        q, k, vt, us, ab, nrm = _inproj_call(xs, row(norm_mix_pre), sh_m, sc_m, w_in_b, layer,
                                             row(q_norm), row(k_norm), cos, sin, cs, True)
        attn = _attn_dispatch(q, k, vt, kc, vtc, jnp.concatenate([nrm, nrm_c], axis=0))
        yf, yb, _ = _ssm_apply(us, tables, hend_c)
        four = _fourier_apply(ab, dft_x, four_w_b, layer, n1)
        xs, hs = _outproj_call(attn, yf, yb, four, xs, w_out_b, glu_w_b, layer, row(ssm_glu_b),
                               row(norm_mix_post), g_m, row(norm_ffn_pre), sh_f, sc_f)
        xs = _ffn_call(hs, xs, wg_b, wu_b, wd_b, layer, row(norm_ffn_post), g_f)

        if need_ctx:
            attn_c = _attn_call(qc, kc, vtc)
            four_c = _fourier_apply(abc, dft_c, four_w_b, layer, n1c)
            xc, hc = _outproj_call(attn_c, yfc, ybc, four_c, xc, w_out_b, glu_w_b, layer,
                                   row(ssm_glu_b), row(norm_mix_post), gc_m, row(norm_ffn_pre),
                                   shc_f, scc_f)
            xc = _ffn_call(hc, xc, wg_b, wu_b, wd_b, layer, row(norm_ffn_post), gc_f)
    return xs[None]
```

```python
import functools
import math

import numpy as np
import jax
import jax.numpy as jnp
from jax import lax
from jax.experimental import pallas as pl
from jax.experimental.pallas import tpu as pltpu

F32 = jnp.float32
BF16 = jnp.bfloat16

HEAD_DIM = 128
N_KV_HEADS = 2
Q_PER_KV = 4
GRID_W = 64
ROPE_THETA = 10000.0
ROPE_PAIRS = HEAD_DIM // 4
SSM_GROUP = 16
SSM_STATE = 64
FOURIER_HEAD_DIM = 128
NORM_EPS = 1e-6
SSM_CHUNK = 8
SSM_LANES = 128
LOG2E = 1.4426950408889634
ATTN_TQ = 512
ATTN_TK = 1280
Q_SCALE = HEAD_DIM ** -0.5 * LOG2E
NORM_SLACK = 1.01
ATTN_MAX_GAP = 100.0
ATTN_SUB = 256
FFN_TM = 1024
FFN_TH = 512
FFN_TN = 256
VT_PAD = 16
VT_ROWS = HEAD_DIM + VT_PAD

VMEM_LIMIT_MB = 56


def _cparams(semantics, vmem_mb=VMEM_LIMIT_MB):
    return pltpu.CompilerParams(dimension_semantics=semantics,
                                vmem_limit_bytes=vmem_mb * 2 ** 20)


def _rms(x, gain):
    return x * lax.rsqrt(jnp.mean(x * x, axis=-1, keepdims=True) + NORM_EPS) * gain


def _ada_kernel(s_ref, w_ref, b_ref, o_ref):
    s = s_ref[...]
    act = (s * jax.nn.sigmoid(s)).astype(BF16)
    o_ref[0] = jnp.dot(act, w_ref[0].astype(BF16), preferred_element_type=F32) + b_ref[0]


def _ada_call(cond, ada_w, ada_b):
    depth, d, n = ada_w.shape
    tn = 1024
    return pl.pallas_call(
        _ada_kernel,
        grid=(depth, n // tn),
        in_specs=[pl.BlockSpec((8, d), lambda l, j: (0, 0)),
                  pl.BlockSpec((1, d, tn), lambda l, j: (l, 0, j)),
                  pl.BlockSpec((1, 1, tn), lambda l, j: (l, 0, j))],
        out_specs=pl.BlockSpec((1, 8, tn), lambda l, j: (l, 0, j)),
        out_shape=jax.ShapeDtypeStruct((depth, 8, n), F32),
        compiler_params=_cparams(("arbitrary", "arbitrary")),
        name="ada_ln",
    )(cond, ada_w, ada_b.reshape(depth, 1, n))


def _inproj_kernel(rope, x_ref, g_ref, sh_ref, sc_ref, w_ref, qn_ref, kn_ref, cos_ref, sin_ref,
                   cs_ref, q_ref, k_ref, vt_ref, us_ref, ab_ref):
    x = x_ref[...]
    h = (_rms(x, g_ref[...]) * (1.0 + sc_ref[...]) + sh_ref[...]).astype(BF16)
    n_q = q_ref.shape[1] // HEAD_DIM
    n_k = k_ref.shape[1] // HEAD_DIM
    q_end = n_q * HEAD_DIM
    k_end = q_end + n_k * HEAD_DIM
    v_end = k_end + n_k * HEAD_DIM
    s_end = v_end + us_ref.shape[1]
    f_end = s_end + ab_ref.shape[2]

    if rope:
        cos = cos_ref[...]
        sin = sin_ref[...]
        lane = lax.broadcasted_iota(jnp.int32, cos.shape, 1)
        low = (lane % 64) < 32

    def head(t, gain, scale):
        t = _rms(t, gain)
        if rope:
            sw = jnp.where(low, pltpu.roll(t, HEAD_DIM - 32, 1), pltpu.roll(t, 32, 1))
            t = t * cos + sw * sin
        if scale != 1.0:
            t = t * scale
        return t.astype(BF16)

    qk = jnp.dot(h, w_ref[0, :, 0:k_end], preferred_element_type=F32)
    for i in range(n_q):
        q_ref[:, i * HEAD_DIM:(i + 1) * HEAD_DIM] = head(
            qk[:, i * HEAD_DIM:(i + 1) * HEAD_DIM], qn_ref[...], Q_SCALE)
    for i in range(n_k):
        k_ref[:, i * HEAD_DIM:(i + 1) * HEAD_DIM] = head(
            qk[:, q_end + i * HEAD_DIM:q_end + (i + 1) * HEAD_DIM], kn_ref[...], 1.0)
    rest = jnp.dot(h, w_ref[0, :, k_end:f_end], preferred_element_type=F32)
    ones_rows = (lax.broadcasted_iota(jnp.int32, (VT_PAD, ATTN_SUB), 0) == 0).astype(BF16)
    for i in range(n_k):
        vt = rest[:, i * HEAD_DIM:(i + 1) * HEAD_DIM].T.astype(BF16)
        for b in range(vt_ref.shape[1]):
            vt_ref[i, b, 0:HEAD_DIM, :] = vt[:, b * ATTN_SUB:(b + 1) * ATTN_SUB]
            vt_ref[i, b, HEAD_DIM:VT_ROWS, :] = ones_rows
    us_ref[...] = rest[:, v_end - k_end:s_end - k_end]
    uf = rest[:, s_end - k_end:f_end - k_end].astype(BF16)
    ab = jnp.dot(uf, cs_ref[...], preferred_element_type=F32)
    fw = ab_ref.shape[2]
    ab_ref[0] = ab[:, 0:fw].astype(BF16)
    ab_ref[1] = ab[:, fw:2 * fw].astype(BF16)


def _layer_block(w, layer):
    return pl.BlockSpec((1,) + w.shape[1:], lambda *_: (layer, 0, 0))


def _inproj_call(x, gain, shift, scale, w_in, layer, qn, kn, cos, sin, cs, rope):
    l, d = x.shape
    tm = min(512, l)
    qw = Q_PER_KV * N_KV_HEADS * HEAD_DIM
    kw = N_KV_HEADS * HEAD_DIM
    fw = cs.shape[0]
    sw = w_in.shape[2] - qw - 2 * kw - fw
    row = lambda i: (i, 0)
    fix = lambda i: (0, 0)
    return pl.pallas_call(
        functools.partial(_inproj_kernel, rope),
        grid=(l // tm,),
        in_specs=[pl.BlockSpec((tm, d), row),
                  pl.BlockSpec((1, d), fix), pl.BlockSpec((1, d), fix), pl.BlockSpec((1, d), fix),
                  _layer_block(w_in, layer),
                  pl.BlockSpec((1, HEAD_DIM), fix), pl.BlockSpec((1, HEAD_DIM), fix),
                  pl.BlockSpec((tm, HEAD_DIM), row), pl.BlockSpec((tm, HEAD_DIM), row),
                  pl.BlockSpec(cs.shape, fix)],
        out_specs=[pl.BlockSpec((tm, qw), row),
                   pl.BlockSpec((tm, kw), row),
                   pl.BlockSpec((N_KV_HEADS, tm // ATTN_SUB, VT_ROWS, ATTN_SUB),
                                lambda i: (0, i, 0, 0)),
                   pl.BlockSpec((tm, sw), row),
                   pl.BlockSpec((2, tm, fw), lambda i: (0, i, 0))],
        out_shape=[jax.ShapeDtypeStruct((l, qw), BF16),
                   jax.ShapeDtypeStruct((l, kw), BF16),
                   jax.ShapeDtypeStruct((N_KV_HEADS, l // ATTN_SUB, VT_ROWS, ATTN_SUB), BF16),
                   jax.ShapeDtypeStruct((l, sw), F32),
                   jax.ShapeDtypeStruct((2, l, fw), BF16)],
        compiler_params=_cparams(("arbitrary",)),
        name="in_proj",
    )(x, gain, shift, scale, w_in, qn, kn, cos, sin, cs)


def _attn_kernel(nkb, tq, q_ref, k_ref, vt_ref, o_ref, qs_ref, s0_ref, s1_ref, bm0_ref, bm1_ref,
                 m_ref, acc_ref):
    j = pl.program_id(2)
    s_refs = (s0_ref, s1_ref)
    bm_refs = (bm0_ref, bm1_ref)

    tk = k_ref.shape[0]
    sub = vt_ref.shape[3]

    def step(score_slot, absorb_slot):
        if absorb_slot is not None:
            m_prev = m_ref[...]
            m_new = jnp.maximum(m_prev, bm_refs[absorb_slot][...])
            alpha = jnp.exp2(m_prev - m_new)
            pv = jnp.zeros(acc_ref.shape, F32)
        bmax = None
        for c in range(tk // sub):
            rows = slice(c * sub, (c + 1) * sub)
            if score_slot is not None:
                s = lax.dot_general(k_ref[rows, :], qs_ref[...], (((1,), (1,)), ((), ())),
                                    preferred_element_type=F32)
                s_refs[score_slot][rows, :] = s
                cmax = jnp.max(s, axis=0, keepdims=True)
                bmax = cmax if bmax is None else jnp.maximum(bmax, cmax)
            if absorb_slot is not None:
                p = jnp.exp2((s_refs[absorb_slot][rows, :] - m_new).astype(BF16))
                pv = pv + jnp.dot(vt_ref[0, c], p, preferred_element_type=F32)
        if score_slot is not None:
            bm_refs[score_slot][...] = bmax
        if absorb_slot is not None:
            acc_ref[...] = alpha * acc_ref[...] + pv
            m_ref[...] = m_new

    @pl.when(j == 0)
    def _first():
        for h in range(Q_PER_KV):
            qs_ref[h * tq:(h + 1) * tq, :] = q_ref[:, h * HEAD_DIM:(h + 1) * HEAD_DIM]
        m_ref[...] = jnp.full(m_ref.shape, -jnp.inf, F32)
        acc_ref[...] = jnp.zeros(acc_ref.shape, F32)
        step(0, None)

    for parity in (0, 1):
        @pl.when((j > 0) & (j < nkb) & (j % 2 == parity))
        def _mid(parity=parity):
            step(parity, 1 - parity)

    @pl.when(j == nkb)
    def _last():
        step(None, (nkb - 1) % 2)
        o = acc_ref[0:HEAD_DIM, :] / acc_ref[HEAD_DIM:HEAD_DIM + 1, :]
        for h in range(Q_PER_KV):
            o_ref[:, h * HEAD_DIM:(h + 1) * HEAD_DIM] = o[:, h * tq:(h + 1) * tq].T.astype(BF16)


def _pick_block(total, unit, cap):
    best = unit
    for b in range(unit, cap + 1, unit):
        if total % b == 0:
            best = b
    return best


def _attn_bounded_kernel(unroll, tq, kn_ref, q_ref, k_ref, vt_ref, kc_ref, vtc_ref, o_ref, qs_ref,
                         acc_ref):
    sub = vt_ref.shape[3]
    for h in range(Q_PER_KV):
        qs_ref[h * tq:(h + 1) * tq, :] = q_ref[:, h * HEAD_DIM:(h + 1) * HEAD_DIM]
    qf = qs_ref[...].astype(F32)
    norm2 = lax.dot_general(jnp.ones((8, HEAD_DIM), F32), qf * qf, (((1,), (1,)), ((), ())),
                            preferred_element_type=F32)[0:1, :]
    shift = jnp.sqrt(norm2) * kn_ref[pl.program_id(0)]

    def weighted_values(keys, vt):
        s = lax.dot_general(keys, qs_ref[...], (((1,), (1,)), ((), ())),
                            preferred_element_type=F32)
        return jnp.dot(vt, jnp.exp2(s - shift).astype(BF16), preferred_element_type=F32)

    acc = None
    for c in range(vtc_ref.shape[1]):
        d = weighted_values(kc_ref[c * sub:(c + 1) * sub, :], vtc_ref[0, c])
        acc = d if acc is None else acc + d
    acc_ref[...] = acc

    def body(t, carry):
        pv = None
        for u in range(unroll):
            c = t * unroll + u
            start = pl.multiple_of(c * sub, sub)
            d = weighted_values(k_ref[pl.ds(start, sub), :], vt_ref[0, c])
            pv = d if pv is None else pv + d
        acc_ref[...] += pv
        return carry

    lax.fori_loop(0, vt_ref.shape[1] // unroll, body, 0)
    o = acc_ref[0:HEAD_DIM, :] / acc_ref[HEAD_DIM:HEAD_DIM + 1, :]
    for h in range(Q_PER_KV):
        o_ref[:, h * HEAD_DIM:(h + 1) * HEAD_DIM] = o[:, h * tq:(h + 1) * tq].T.astype(BF16)


def _attn_bounded_call(q, k, vt, kc, vtc, k_norm_max):
    l = q.shape[0]
    nblk = vt.shape[1]
    tq = min(ATTN_TQ, l)
    gw = Q_PER_KV * HEAD_DIM
    n = Q_PER_KV * tq
    unroll = max(u for u in (8, 5, 4, 3, 2, 1) if nblk % u == 0)
    head_rows = lambda a: pl.BlockSpec((a.shape[0], HEAD_DIM), lambda g, i: (0, g))
    head_blocks = lambda a: pl.BlockSpec((1,) + a.shape[1:], lambda g, i: (g, 0, 0, 0))
    return pl.pallas_call(
        functools.partial(_attn_bounded_kernel, unroll, tq),
        grid=(N_KV_HEADS, l // tq),
        in_specs=[pl.BlockSpec(memory_space=pltpu.SMEM),
                  pl.BlockSpec((tq, gw), lambda g, i: (i, g)),
                  head_rows(k), head_blocks(vt), head_rows(kc), head_blocks(vtc)],
        out_specs=pl.BlockSpec((tq, gw), lambda g, i: (i, g)),
        out_shape=jax.ShapeDtypeStruct(q.shape, BF16),
        scratch_shapes=[pltpu.VMEM((n, HEAD_DIM), BF16), pltpu.VMEM((VT_ROWS, n), F32)],
        compiler_params=_cparams(("arbitrary", "arbitrary")),
        name="attention_bounded",
    )(k_norm_max, q, k, vt, kc, vtc)


def _attn_dispatch(q, k, vt, kc, vtc, q_gain, k_gain):
    bound = lambda gain: NORM_SLACK * math.sqrt(HEAD_DIM) * jnp.max(jnp.abs(gain.astype(F32)))
    qn = Q_SCALE * bound(q_gain)
    kn = jnp.full((N_KV_HEADS,), bound(k_gain), F32)
    safe = 2.0 * qn * jnp.max(kn) <= ATTN_MAX_GAP
    return lax.cond(safe,
                    lambda: _attn_bounded_call(q, k, vt, kc, vtc, kn),
                    lambda: _attn_call(q, jnp.concatenate([k, kc], axis=0),
                                       jnp.concatenate([vt, vtc], axis=1)))


def _attn_call(q, k_all, vt_all):
    l = q.shape[0]
    s_len = k_all.shape[0]
    sub = vt_all.shape[3]
    tq = min(ATTN_TQ, l)
    tk = _pick_block(s_len, sub, ATTN_TK)
    nkb = s_len // tk
    gw = Q_PER_KV * HEAD_DIM
    n = Q_PER_KV * tq
    return pl.pallas_call(
        functools.partial(_attn_kernel, nkb, tq),
        grid=(N_KV_HEADS, l // tq, nkb + 1),
        in_specs=[pl.BlockSpec((tq, gw), lambda g, i, j: (i, g)),
                  pl.BlockSpec((tk, HEAD_DIM), lambda g, i, j: (jnp.minimum(j, nkb - 1), g)),
                  pl.BlockSpec((1, tk // sub, VT_ROWS, sub),
                               lambda g, i, j: (g, jnp.maximum(j - 1, 0), 0, 0))],
        out_specs=pl.BlockSpec((tq, gw), lambda g, i, j: (i, g)),
        out_shape=jax.ShapeDtypeStruct(q.shape, BF16),
        scratch_shapes=[pltpu.VMEM((n, HEAD_DIM), BF16),
                        pltpu.VMEM((tk, n), F32),
                        pltpu.VMEM((tk, n), F32),
                        pltpu.VMEM((1, n), F32),
                        pltpu.VMEM((1, n), F32),
                        pltpu.VMEM((1, n), F32),
                        pltpu.VMEM((VT_ROWS, n), F32)],
        compiler_params=_cparams(("arbitrary", "arbitrary", "arbitrary")),
        name="attention",
    )(q, k_all, vt_all)


def _ssm_tables(lam_re, lam_im, log_dt, b_re, b_im, c_re, c_im, d):
    t = SSM_CHUNK
    g, p = lam_re.shape[1:]
    hdim = b_re.shape[-1]
    lr = lam_re.astype(F32)
    li = lam_im.astype(F32)
    dt = jnp.exp(log_dt.astype(F32))[..., None]
    mag = jnp.exp(lr * dt)
    ar = mag * jnp.cos(li * dt)
    ai = mag * jnp.sin(li * dt)
    den = lr * lr + li * li
    cr = ((ar - 1.0) * lr + ai * li) / den
    ci = (ai * lr - (ar - 1.0) * li) / den
    br = b_re.astype(F32)
    bi = b_im.astype(F32)
    bbr = cr[..., None] * br - ci[..., None] * bi
    bbi = cr[..., None] * bi + ci[..., None] * br
    n = jnp.arange(t + 1, dtype=F32)[:, None, None, None]
    pmag = jnp.exp(n * (lr * dt)[None])
    pr = pmag * jnp.cos(n * (li * dt)[None])
    pi = pmag * jnp.sin(n * (li * dt)[None])
    cre = c_re.astype(F32)
    cim = c_im.astype(F32)

    gs = SSM_LANES // hdim
    n_sg = g // gs

    def selector(r, inner):
        e = np.zeros((gs, r, gs, inner, r, inner), np.float32)
        for grp in range(gs):
            e[grp, :, grp] = np.eye(r * inner, dtype=np.float32).reshape(r, inner, r, inner)
        return jnp.asarray(e.reshape(gs, r * gs * inner, r * inner), dtype=BF16)

    def expand(small):
        _, r1, i1, r2, i2 = small.shape
        blocks = small.reshape(n_sg, gs, r1 * i1, r2 * i2).astype(BF16)
        left = jnp.einsum('grk,sgkl->sgrl', selector(r1, i1), blocks,
                          preferred_element_type=F32).astype(BF16)
        return jnp.einsum('sgrl,gcl->src', left, selector(r2, i2),
                          preferred_element_type=F32).astype(BF16)

    def summ(direction, powers):
        er = pr[powers, direction][..., None] * bbr[direction][None] \
            - pi[powers, direction][..., None] * bbi[direction][None]
        ei = pr[powers, direction][..., None] * bbi[direction][None] \
            + pi[powers, direction][..., None] * bbr[direction][None]
        tr = lambda e: jnp.transpose(e, (1, 0, 3, 2))
        return expand(jnp.stack([tr(er), tr(ei)], axis=3))
    steps = jnp.arange(t)
    p_f = summ(0, t - 1 - steps)
    p_b = summ(1, steps)

    row = lambda v: v.reshape(n_sg, gs * p)
    a_row = jnp.concatenate([row(pr[t, 0]), row(pr[t, 1]), row(pi[t, 0]), row(pi[t, 1])],
                            axis=-1)[:, None, :]

    def kern(direction):
        er = pr[:t, direction][..., None] * bbr[direction][None] \
            - pi[:t, direction][..., None] * bbi[direction][None]
        ei = pr[:t, direction][..., None] * bbi[direction][None] \
            + pi[:t, direction][..., None] * bbr[direction][None]
        return (jnp.einsum('gop,ngpi->ngoi', cre[direction], er)
                - jnp.einsum('gop,ngpi->ngoi', cim[direction], ei))
    kf = kern(0)
    kb = kern(1)
    s_idx = steps[:, None]
    t_idx = steps[None, :]
    lag_f = jnp.clip(t_idx - s_idx, 0, t - 1)
    lag_b = jnp.clip(s_idx - t_idx, 0, t - 1)
    mf = jnp.where((t_idx >= s_idx)[:, :, None, None, None], kf[lag_f], 0.0)
    mb = jnp.where((s_idx >= t_idx)[:, :, None, None, None], kb[lag_b], 0.0)
    dd = d.astype(F32).reshape(g, hdim)
    eye_t = jnp.eye(t, dtype=F32)[:, :, None, None, None]
    eye_c = jnp.eye(hdim, dtype=F32)[None, None, None]
    m5 = mf + mb + eye_t * eye_c * dd[None, None, :, :, None]
    m_mat = expand(jnp.transpose(m5, (2, 0, 4, 1, 3)))

    def state_out(direction, powers):
        prn = pr[powers, direction]
        pin = pi[powers, direction]
        wr = cre[direction][None] * prn[:, :, None, :] - cim[direction][None] * pin[:, :, None, :]
        wi = -cre[direction][None] * pin[:, :, None, :] - cim[direction][None] * prn[:, :, None, :]
        tr = lambda w: jnp.transpose(w, (1, 3, 0, 2))
        return expand(jnp.stack([tr(wr), tr(wi)], axis=1))
    q_f = state_out(0, steps + 1)
    q_b = state_out(1, t - steps)
    return p_f, p_b, m_mat, q_f, q_b, a_row


def _ssm_kernel(tbk, usf_ref, usb_ref, pf_ref, pb_ref, m_ref, qf_ref, qb_ref, a_ref, h0_ref,
                yf_ref, yb_ref, hend_ref, xf_ref, xb_ref, sf_ref, sb_ref, hf_ref, hb_ref, h_ref):
    t = SSM_CHUNK
    lanes = SSM_LANES

    @pl.when(pl.program_id(1) == 0)
    def _init():
        h_ref[...] = h0_ref[0]

    for tt in range(t):
        cols = slice(tt * lanes, (tt + 1) * lanes)
        xf_ref[:, cols] = usf_ref[pl.ds(tt, tbk, stride=t), :].astype(BF16)
        xb_ref[:, cols] = usb_ref[pl.ds(tt, tbk, stride=t), :].astype(BF16)
    sf_ref[...] = jnp.dot(xf_ref[...], pf_ref[0], preferred_element_type=F32)
    sb_ref[...] = jnp.dot(xb_ref[...], pb_ref[0], preferred_element_type=F32)

    w = a_ref.shape[2] // 4
    a = a_ref[0]
    arf, arb, aif, aib = (a[:, i * w:(i + 1) * w] for i in range(4))
    h = h_ref[...]

    def body(k, carry):
        hrf, hrb, hif, hib = carry
        kb = tbk - 1 - k
        hf_ref[pl.ds(k, 1), 0:w] = hrf
        hf_ref[pl.ds(k, 1), w:2 * w] = hif
        hb_ref[pl.ds(kb, 1), 0:w] = hrb
        hb_ref[pl.ds(kb, 1), w:2 * w] = hib
        sf = sf_ref[pl.ds(k, 1), :]
        sb = sb_ref[pl.ds(kb, 1), :]
        return (arf * hrf - aif * hif + sf[:, 0:w], arb * hrb - aib * hib + sb[:, 0:w],
                arf * hif + aif * hrf + sf[:, w:2 * w], arb * hib + aib * hrb + sb[:, w:2 * w])

    carry = lax.fori_loop(0, tbk, body, tuple(h[:, i * w:(i + 1) * w] for i in range(4)),
                          unroll=8)
    h = jnp.concatenate(carry, axis=1)
    h_ref[...] = h
    hend_ref[0] = h

    yf = (jnp.dot(xf_ref[...], m_ref[0], preferred_element_type=F32)
          + jnp.dot(hf_ref[...].astype(BF16), qf_ref[0], preferred_element_type=F32))
    yb = jnp.dot(hb_ref[...].astype(BF16), qb_ref[0], preferred_element_type=F32)
    for tt in range(t):
        cols = slice(tt * lanes, (tt + 1) * lanes)
        yf_ref[pl.ds(tt, tbk, stride=t), :] = yf[:, cols]
        yb_ref[pl.ds(tt, tbk, stride=t), :] = yb[:, cols]


def _ssm_apply(us, tables, h0):
    p_f, p_b, m_mat, q_f, q_b, a_row = tables
    l, w = us.shape
    n_sg = w // SSM_LANES
    n = l // SSM_CHUNK
    tbk = min(256, n)
    nb = n // tbk
    rows = tbk * SSM_CHUNK
    kx = SSM_CHUNK * SSM_LANES
    sw = p_f.shape[2]
    hw = a_row.shape[2]
    per_sg = lambda a: pl.BlockSpec((1,) + a.shape[1:], lambda s, i: (s, 0, 0))
    fwd = pl.BlockSpec((rows, SSM_LANES), lambda s, i: (i, s))
    bwd = pl.BlockSpec((rows, SSM_LANES), lambda s, i: (nb - 1 - i, s))
    return pl.pallas_call(
        functools.partial(_ssm_kernel, tbk),
        grid=(n_sg, nb),
        in_specs=[fwd, bwd, per_sg(p_f), per_sg(p_b), per_sg(m_mat), per_sg(q_f), per_sg(q_b),
                  per_sg(a_row), per_sg(h0)],
        out_specs=[fwd, bwd, per_sg(h0)],
        out_shape=[jax.ShapeDtypeStruct((l, w), F32), jax.ShapeDtypeStruct((l, w), F32),
                   jax.ShapeDtypeStruct(h0.shape, F32)],
        scratch_shapes=[pltpu.VMEM((tbk, kx), BF16), pltpu.VMEM((tbk, kx), BF16),
                        pltpu.VMEM((tbk, sw), F32), pltpu.VMEM((tbk, sw), F32),
                        pltpu.VMEM((tbk, sw), F32), pltpu.VMEM((tbk, sw), F32),
                        pltpu.VMEM((1, hw), F32)],
        compiler_params=_cparams(("arbitrary", "arbitrary")),
        name="ssm_scan",
    )(us, us, p_f, p_b, m_mat, q_f, q_b, a_row, h0)


def _dft_tables(l, n1, hw):
    n2 = l // n1
    scale = 1.0 / math.sqrt(l * hw)
    if n1 == 1:
        t1 = None
    else:
        ang = 2.0 * np.pi * ((np.arange(n1)[:, None] * np.arange(n1)[None, :]) % n1) / n1
        c, s = np.cos(ang), np.sin(ang)
        t1 = jnp.asarray(np.block([[c, -s], [-s, -c]]), dtype=BF16)
    k1 = np.arange(n1)[:, None, None]
    k2 = np.arange(n2)[None, :, None]
    m = np.arange(n2)[None, None, :]
    ang = 2.0 * np.pi * ((m * (n1 * k2 + k1)) % l) / l
    sign = -1.0 if n1 == 1 else 1.0
    g = np.concatenate([np.cos(ang), sign * np.sin(ang)], axis=-1) * scale
    return t1, jnp.asarray(g, dtype=BF16)


def _dft1_kernel(t_ref, x_ref, z_ref):
    z_ref[...] = jnp.dot(t_ref[...], x_ref[...], preferred_element_type=F32).astype(z_ref.dtype)


def _dft1_call(t1, ab2d):
    r, n = ab2d.shape
    tn = min(4096, n)
    return pl.pallas_call(
        _dft1_kernel,
        grid=(n // tn,),
        in_specs=[pl.BlockSpec((r, r), lambda j: (0, 0)),
                  pl.BlockSpec((r, tn), lambda j: (0, j))],
        out_specs=pl.BlockSpec((r, tn), lambda j: (0, j)),
        out_shape=jax.ShapeDtypeStruct((r, n), BF16),
        compiler_params=_cparams(("arbitrary",)),
        name="dft_stage1",
    )(t1, ab2d)


def _dft2_kernel(bt, zr_ref, zi_ref, g_ref, w_ref, o_ref):
    fw = w_ref.shape[1]
    for b in range(bt):
        z = jnp.concatenate([zr_ref[b], zi_ref[b]], axis=0)
        x = jnp.dot(g_ref[b], z, preferred_element_type=F32)
        o_ref[:, b * fw:(b + 1) * fw] = jnp.dot(
            x.astype(BF16), w_ref[0], preferred_element_type=F32).astype(o_ref.dtype)


def _dft2_call(z3, g, fourier_w, layer):
    n1x2, n2, fw = z3.shape
    n1 = n1x2 // 2
    bt = min(8, n1)
    return pl.pallas_call(
        functools.partial(_dft2_kernel, bt),
        grid=(n1 // bt,),
        in_specs=[pl.BlockSpec((bt, n2, fw), lambda b: (b, 0, 0)),
                  pl.BlockSpec((bt, n2, fw), lambda b: (b + n1 // bt, 0, 0)),
                  pl.BlockSpec((bt, n2, 2 * n2), lambda b: (b, 0, 0)),
                  _layer_block(fourier_w, layer)],
        out_specs=pl.BlockSpec((n2, bt * fw), lambda b: (0, b)),
        out_shape=jax.ShapeDtypeStruct((n2, n1 * fw), BF16),
        compiler_params=_cparams(("arbitrary",)),
        name="dft_stage2",
    )(z3, z3, g, fourier_w)


def _fourier_apply(ab, tables, fourier_w, layer, n1):
    t1, g = tables
    _, l, fw = ab.shape
    n2 = l // n1
    if n1 == 1:
        z3 = ab
    else:
        z3 = _dft1_call(t1, ab.reshape(2 * n1, n2 * fw)).reshape(2 * n1, n2, fw)
    return _dft2_call(z3, g, fourier_w, layer).reshape(l, fw)


def _outproj_kernel(attn_ref, yf_ref, yb_ref, four_ref, x_ref, wo_ref, gw_ref, gb_ref, npost_ref,
                    gate_ref, npre_ref, sh_ref, sc_ref, xo_ref, h_ref):
    ys = yf_ref[...] + yb_ref[...]
    gl = 0.5 * ys * (1.0 + jnp.tanh(math.sqrt(2.0 / math.pi) * (ys + 0.044715 * (ys * ys * ys))))
    z = jnp.dot(gl.astype(BF16), gw_ref[0], preferred_element_type=F32) + gb_ref[...]
    ssm = (gl * jax.nn.sigmoid(z)).astype(BF16)
    cat = jnp.concatenate([attn_ref[...], ssm, four_ref[...]], axis=-1)
    y = jnp.dot(cat, wo_ref[0], preferred_element_type=F32)
    xn = x_ref[...] + gate_ref[...] * _rms(y, npost_ref[...])
    xo_ref[...] = xn
    h_ref[...] = (_rms(xn, npre_ref[...]) * (1.0 + sc_ref[...]) + sh_ref[...]).astype(BF16)


def _outproj_call(attn, yf, yb, four, x, w_out, glu_w, layer, glu_b, npost, gate, npre, shift,
                  scale):
    l, d = x.shape
    tm = min(512, l)
    row = lambda i: (i, 0)
    fix = lambda i: (0, 0)
    vec = pl.BlockSpec((1, d), fix)
    return pl.pallas_call(
        _outproj_kernel,
        grid=(l // tm,),
        in_specs=[pl.BlockSpec((tm, attn.shape[1]), row),
                  pl.BlockSpec((tm, yf.shape[1]), row),
                  pl.BlockSpec((tm, yb.shape[1]), row),
                  pl.BlockSpec((tm, four.shape[1]), row),
                  pl.BlockSpec((tm, d), row),
                  _layer_block(w_out, layer),
                  _layer_block(glu_w, layer),
                  pl.BlockSpec((1, glu_w.shape[2]), fix),
                  vec, vec, vec, vec, vec],
        out_specs=[pl.BlockSpec((tm, d), row), pl.BlockSpec((tm, d), row)],
        out_shape=[jax.ShapeDtypeStruct((l, d), F32), jax.ShapeDtypeStruct((l, d), BF16)],
        compiler_params=_cparams(("arbitrary",)),
        name="out_proj",
    )(attn, yf, yb, four, x, w_out, glu_w, glu_b, npost, gate, npre, shift, scale)


def _ffn_kernel(nj1, nj2, th, tn, h_ref, x_ref, wg_ref, wu_ref, wd_ref, npost_ref, gate_ref, o_ref,
                act_ref, tmp_ref):
    j = pl.program_id(1)

    @pl.when(j < nj1)
    def _up():
        h = h_ref[...]
        a = jnp.dot(h, wg_ref[0], preferred_element_type=F32)
        u = jnp.dot(h, wu_ref[0], preferred_element_type=F32)
        tmp_ref[...] = (a * jax.nn.sigmoid(a) * u).astype(BF16)

    for c in range(nj1):
        @pl.when(j == c)
        def _place(c=c):
            act_ref[:, c * th:(c + 1) * th] = tmp_ref[...]

    for c in range(nj2):
        @pl.when(j == nj1 + c)
        def _down(c=c):
            o_ref[:, c * tn:(c + 1) * tn] = jnp.dot(act_ref[...], wd_ref[0],
                                                    preferred_element_type=F32)

    @pl.when(j == nj1 + nj2 - 1)
    def _finish():
        o_ref[...] = x_ref[...] + gate_ref[...] * _rms(o_ref[...], npost_ref[...])


def _ffn_call(h, x, w_gate, w_up, w_down, layer, npost, gate):
    l, d = x.shape
    fh = w_gate.shape[2]
    th, tn = FFN_TH, FFN_TN
    nj1 = fh // th
    nj2 = d // tn
    tm = min(FFN_TM, l)
    vec = pl.BlockSpec((1, d), lambda i, j: (0, 0))
    up = pl.BlockSpec((1, d, th), lambda i, j: (layer, 0, jnp.minimum(j, nj1 - 1)))
    once = pl.Buffered(1)
    return pl.pallas_call(
        functools.partial(_ffn_kernel, nj1, nj2, th, tn),
        grid=(l // tm, nj1 + nj2),
        in_specs=[pl.BlockSpec((tm, d), lambda i, j: (i, 0)),
                  pl.BlockSpec((tm, d), lambda i, j: (i, 0), pipeline_mode=once),
                  up, up,
                  pl.BlockSpec((1, fh, tn), lambda i, j: (layer, 0, jnp.maximum(j - nj1, 0))),
                  vec, vec],
        out_specs=pl.BlockSpec((tm, d), lambda i, j: (i, 0), pipeline_mode=once),
        out_shape=jax.ShapeDtypeStruct((l, d), F32),
        scratch_shapes=[pltpu.VMEM((tm, fh), BF16), pltpu.VMEM((tm, th), BF16)],
        compiler_params=_cparams(("arbitrary", "arbitrary")),
        name="ffn",
    )(h, x, w_gate, w_up, w_down, npost, gate)


def _rope_tables(l):
    t = np.arange(l)
    row = (t // GRID_W).astype(np.float32)
    col = (t % GRID_W).astype(np.float32)
    freqs = np.float32(ROPE_THETA) ** (-np.arange(ROPE_PAIRS, dtype=np.float32) / np.float32(ROPE_PAIRS))
    ang_r = (row[:, None] * freqs).astype(np.float32)
    ang_c = (col[:, None] * freqs).astype(np.float32)
    cos = np.concatenate([np.cos(ang_r)] * 2 + [np.cos(ang_c)] * 2, axis=-1)
    sin = np.concatenate([-np.sin(ang_r), np.sin(ang_r), -np.sin(ang_c), np.sin(ang_c)], axis=-1)
    return jnp.asarray(cos, dtype=F32), jnp.asarray(sin, dtype=F32)


def _channel_dft_table(fw):
    hw = FOURIER_HEAD_DIM
    ang = 2.0 * np.pi * ((np.arange(hw)[:, None] * np.arange(hw)[None, :]) % hw) / hw
    eye = np.eye(fw // hw)
    return jnp.asarray(np.concatenate([np.kron(eye, np.cos(ang)), np.kron(eye, np.sin(ang))], axis=1),
                       dtype=BF16)


def _dft_split(l):
    n1 = 1
    while n1 * n1 < l:
        n1 *= 2
    return n1 if (l >= 1024 and n1 * n1 == l) else 1


def kernel(x, c, ctx, c_ctx, ada_w, ada_b, norm_mix_pre, norm_mix_post, norm_ffn_pre, norm_ffn_post, w_in, q_norm, k_norm, ssm_lam_re, ssm_lam_im, ssm_log_dt, ssm_b_re, ssm_b_im, ssm_c_re, ssm_c_im, ssm_d, ssm_glu_w, ssm_glu_b, fourier_w, w_out, ffn_w_gate, ffn_w_up, ffn_w_down):
    depth = ada_w.shape[0]
    _, l, d = x.shape
    n_ctx = ctx.shape[1]
    fw = fourier_w.shape[1]
    sw = ssm_d.shape[1]

    cond = jnp.zeros((8, d), F32).at[0].set(c[0]).at[1].set(c_ctx)
    mod = _ada_call(cond, ada_w, ada_b)

    cos, sin = _rope_tables(l)
    zero_tab = jnp.zeros((n_ctx, HEAD_DIM), F32)
    cs = _channel_dft_table(fw)
    n1 = _dft_split(l)
    n1c = _dft_split(n_ctx)
    dft_x = _dft_tables(l, n1, FOURIER_HEAD_DIM)
    dft_c = _dft_tables(n_ctx, n1c, FOURIER_HEAD_DIM)
    h0 = jnp.zeros((sw // SSM_LANES, 1, 4 * (SSM_LANES // SSM_GROUP) * SSM_STATE), F32)

    w_in_b = w_in.astype(BF16)
    w_out_b = w_out.astype(BF16)
    glu_w_b = ssm_glu_w.astype(BF16)
    four_w_b = fourier_w.astype(BF16)
    wg_b = ffn_w_gate.astype(BF16)
    wu_b = ffn_w_up.astype(BF16)
    wd_b = ffn_w_down.astype(BF16)

    xs = x[0]
    xc = ctx[0]
    for layer in range(depth):
        need_ctx = layer < depth - 1
        vecs = lambda r: [mod[layer, r:r + 1, i * d:(i + 1) * d] for i in range(6)]
        sh_m, sc_m, g_m, sh_f, sc_f, g_f = vecs(0)
        shc_m, scc_m, gc_m, shc_f, scc_f, gc_f = vecs(1)
        row = lambda a: a[layer].reshape(1, -1)
        tables = _ssm_tables(ssm_lam_re[layer], ssm_lam_im[layer], ssm_log_dt[layer],
                             ssm_b_re[layer], ssm_b_im[layer], ssm_c_re[layer], ssm_c_im[layer],
                             ssm_d[layer])

        qc, kc, vtc, usc, abc = _inproj_call(
            xc, row(norm_mix_pre), shc_m, scc_m, w_in_b, layer, row(q_norm), row(k_norm),
            zero_tab, zero_tab, cs, False)
        yfc, ybc, hend_c = _ssm_apply(usc, tables, h0)

        q, k, vt, us, ab = _inproj_call(xs, row(norm_mix_pre), sh_m, sc_m, w_in_b, layer,
                                        row(q_norm), row(k_norm), cos, sin, cs, True)
        attn = _attn_dispatch(q, k, vt, kc, vtc, q_norm[layer], k_norm[layer])
        yf, yb, _ = _ssm_apply(us, tables, hend_c)
        four = _fourier_apply(ab, dft_x, four_w_b, layer, n1)
        xs, hs = _outproj_call(attn, yf, yb, four, xs, w_out_b, glu_w_b, layer, row(ssm_glu_b),
                               row(norm_mix_post), g_m, row(norm_ffn_pre), sh_f, sc_f)
        xs = _ffn_call(hs, xs, wg_b, wu_b, wd_b, layer, row(norm_ffn_post), g_f)

        if need_ctx:
            attn_c = _attn_call(qc, kc, vtc)
            four_c = _fourier_apply(abc, dft_c, four_w_b, layer, n1c)
            xc, hc = _outproj_call(attn_c, yfc, ybc, four_c, xc, w_out_b, glu_w_b, layer,
                                   row(ssm_glu_b), row(norm_mix_post), gc_m, row(norm_ffn_pre),
                                   shc_f, scc_f)
            xc = _ffn_call(hc, xc, wg_b, wu_b, wd_b, layer, row(norm_ffn_post), gc_f)
    return xs[None]
```

```python
import functools
import math

import numpy as np
import jax
import jax.numpy as jnp
from jax import lax
from jax.experimental import pallas as pl
from jax.experimental.pallas import tpu as pltpu

F32 = jnp.float32
BF16 = jnp.bfloat16

HEAD_DIM = 128
N_KV_HEADS = 2
Q_PER_KV = 4
GRID_W = 64
ROPE_THETA = 10000.0
ROPE_PAIRS = HEAD_DIM // 4
SSM_GROUP = 16
SSM_STATE = 64
FOURIER_HEAD_DIM = 128
NORM_EPS = 1e-6
SSM_CHUNK = 8
SSM_LANES = 128
LOG2E = 1.4426950408889634
ATTN_TQ = 512
ATTN_TK = 1280
Q_SCALE = HEAD_DIM ** -0.5 * LOG2E
NORM_SLACK = 1.01
ATTN_MAX_GAP = 100.0
ATTN_SUB = 256
FFN_TM = 1024
FFN_TH = 512
FFN_TN = 256
FFN_VMEM_MB = 60
VT_PAD = 16
VT_ROWS = HEAD_DIM + VT_PAD

VMEM_LIMIT_MB = 56


def _cparams(semantics, vmem_mb=VMEM_LIMIT_MB):
    return pltpu.CompilerParams(dimension_semantics=semantics,
                                vmem_limit_bytes=vmem_mb * 2 ** 20)


def _rms(x, gain):
    return x * lax.rsqrt(jnp.mean(x * x, axis=-1, keepdims=True) + NORM_EPS) * gain


def _ada_kernel(s_ref, w_ref, b_ref, o_ref):
    s = s_ref[...]
    act = (s * jax.nn.sigmoid(s)).astype(BF16)
    o_ref[0] = jnp.dot(act, w_ref[0].astype(BF16), preferred_element_type=F32) + b_ref[0]


def _ada_call(cond, ada_w, ada_b):
    depth, d, n = ada_w.shape
    tn = 1024
    return pl.pallas_call(
        _ada_kernel,
        grid=(depth, n // tn),
        in_specs=[pl.BlockSpec((8, d), lambda l, j: (0, 0)),
                  pl.BlockSpec((1, d, tn), lambda l, j: (l, 0, j)),
                  pl.BlockSpec((1, 1, tn), lambda l, j: (l, 0, j))],
        out_specs=pl.BlockSpec((1, 8, tn), lambda l, j: (l, 0, j)),
        out_shape=jax.ShapeDtypeStruct((depth, 8, n), F32),
        compiler_params=_cparams(("arbitrary", "arbitrary")),
        name="ada_ln",
    )(cond, ada_w, ada_b.reshape(depth, 1, n))


def _inproj_kernel(rope, x_ref, g_ref, sh_ref, sc_ref, w_ref, qn_ref, kn_ref, cos_ref, sin_ref,
                   cs_ref, q_ref, k_ref, vt_ref, us_ref, ab_ref):
    x = x_ref[...]
    h = (_rms(x, g_ref[...]) * (1.0 + sc_ref[...]) + sh_ref[...]).astype(BF16)
    n_q = q_ref.shape[1] // HEAD_DIM
    n_k = k_ref.shape[1] // HEAD_DIM
    q_end = n_q * HEAD_DIM
    k_end = q_end + n_k * HEAD_DIM
    v_end = k_end + n_k * HEAD_DIM
    s_end = v_end + us_ref.shape[1]
    f_end = s_end + ab_ref.shape[2]

    if rope:
        cos = cos_ref[...]
        sin = sin_ref[...]
        lane = lax.broadcasted_iota(jnp.int32, cos.shape, 1)
        low = (lane % 64) < 32

    def head(t, gain, scale):
        t = _rms(t, gain)
        if rope:
            sw = jnp.where(low, pltpu.roll(t, HEAD_DIM - 32, 1), pltpu.roll(t, 32, 1))
            t = t * cos + sw * sin
        if scale != 1.0:
            t = t * scale
        return t.astype(BF16)

    qk = jnp.dot(h, w_ref[0, :, 0:k_end], preferred_element_type=F32)
    for i in range(n_q):
        q_ref[:, i * HEAD_DIM:(i + 1) * HEAD_DIM] = head(
            qk[:, i * HEAD_DIM:(i + 1) * HEAD_DIM], qn_ref[...], Q_SCALE)
    for i in range(n_k):
        k_ref[:, i * HEAD_DIM:(i + 1) * HEAD_DIM] = head(
            qk[:, q_end + i * HEAD_DIM:q_end + (i + 1) * HEAD_DIM], kn_ref[...], 1.0)
    rest = jnp.dot(h, w_ref[0, :, k_end:f_end], preferred_element_type=F32)
    ones_rows = (lax.broadcasted_iota(jnp.int32, (VT_PAD, ATTN_SUB), 0) == 0).astype(BF16)
    for i in range(n_k):
        vt = rest[:, i * HEAD_DIM:(i + 1) * HEAD_DIM].T.astype(BF16)
        for b in range(vt_ref.shape[1]):
            vt_ref[i, b, 0:HEAD_DIM, :] = vt[:, b * ATTN_SUB:(b + 1) * ATTN_SUB]
            vt_ref[i, b, HEAD_DIM:VT_ROWS, :] = ones_rows
    us_ref[...] = rest[:, v_end - k_end:s_end - k_end]
    uf = rest[:, s_end - k_end:f_end - k_end].astype(BF16)
    ab = jnp.dot(uf, cs_ref[...], preferred_element_type=F32)
    fw = ab_ref.shape[2]
    ab_ref[0] = ab[:, 0:fw].astype(BF16)
    ab_ref[1] = ab[:, fw:2 * fw].astype(BF16)


def _layer_block(w, layer):
    return pl.BlockSpec((1,) + w.shape[1:], lambda *_: (layer, 0, 0))


def _inproj_call(x, gain, shift, scale, w_in, layer, qn, kn, cos, sin, cs, rope):
    l, d = x.shape
    tm = min(512, l)
    qw = Q_PER_KV * N_KV_HEADS * HEAD_DIM
    kw = N_KV_HEADS * HEAD_DIM
    fw = cs.shape[0]
    sw = w_in.shape[2] - qw - 2 * kw - fw
    row = lambda i: (i, 0)
    fix = lambda i: (0, 0)
    return pl.pallas_call(
        functools.partial(_inproj_kernel, rope),
        grid=(l // tm,),
        in_specs=[pl.BlockSpec((tm, d), row),
                  pl.BlockSpec((1, d), fix), pl.BlockSpec((1, d), fix), pl.BlockSpec((1, d), fix),
                  _layer_block(w_in, layer),
                  pl.BlockSpec((1, HEAD_DIM), fix), pl.BlockSpec((1, HEAD_DIM), fix),
                  pl.BlockSpec((tm, HEAD_DIM), row), pl.BlockSpec((tm, HEAD_DIM), row),
                  pl.BlockSpec(cs.shape, fix)],
        out_specs=[pl.BlockSpec((tm, qw), row),
                   pl.BlockSpec((tm, kw), row),
                   pl.BlockSpec((N_KV_HEADS, tm // ATTN_SUB, VT_ROWS, ATTN_SUB),
                                lambda i: (0, i, 0, 0)),
                   pl.BlockSpec((tm, sw), row),
                   pl.BlockSpec((2, tm, fw), lambda i: (0, i, 0))],
        out_shape=[jax.ShapeDtypeStruct((l, qw), BF16),
                   jax.ShapeDtypeStruct((l, kw), BF16),
                   jax.ShapeDtypeStruct((N_KV_HEADS, l // ATTN_SUB, VT_ROWS, ATTN_SUB), BF16),
                   jax.ShapeDtypeStruct((l, sw), F32),
                   jax.ShapeDtypeStruct((2, l, fw), BF16)],
        compiler_params=_cparams(("arbitrary",)),
        name="in_proj",
    )(x, gain, shift, scale, w_in, qn, kn, cos, sin, cs)


def _attn_kernel(nkb, tq, q_ref, k_ref, vt_ref, o_ref, qs_ref, s0_ref, s1_ref, bm0_ref, bm1_ref,
                 m_ref, acc_ref):
    j = pl.program_id(2)
    s_refs = (s0_ref, s1_ref)
    bm_refs = (bm0_ref, bm1_ref)

    tk = k_ref.shape[0]
    sub = vt_ref.shape[3]

    def step(score_slot, absorb_slot):
        if absorb_slot is not None:
            m_prev = m_ref[...]
            m_new = jnp.maximum(m_prev, bm_refs[absorb_slot][...])
            alpha = jnp.exp2(m_prev - m_new)
            pv = jnp.zeros(acc_ref.shape, F32)
        bmax = None
        for c in range(tk // sub):
            rows = slice(c * sub, (c + 1) * sub)
            if score_slot is not None:
                s = lax.dot_general(k_ref[rows, :], qs_ref[...], (((1,), (1,)), ((), ())),
                                    preferred_element_type=F32)
                s_refs[score_slot][rows, :] = s
                cmax = jnp.max(s, axis=0, keepdims=True)
                bmax = cmax if bmax is None else jnp.maximum(bmax, cmax)
            if absorb_slot is not None:
                p = jnp.exp2((s_refs[absorb_slot][rows, :] - m_new).astype(BF16))
                pv = pv + jnp.dot(vt_ref[0, c], p, preferred_element_type=F32)
        if score_slot is not None:
            bm_refs[score_slot][...] = bmax
        if absorb_slot is not None:
            acc_ref[...] = alpha * acc_ref[...] + pv
            m_ref[...] = m_new

    @pl.when(j == 0)
    def _first():
        for h in range(Q_PER_KV):
            qs_ref[h * tq:(h + 1) * tq, :] = q_ref[:, h * HEAD_DIM:(h + 1) * HEAD_DIM]
        m_ref[...] = jnp.full(m_ref.shape, -jnp.inf, F32)
        acc_ref[...] = jnp.zeros(acc_ref.shape, F32)
        step(0, None)

    for parity in (0, 1):
        @pl.when((j > 0) & (j < nkb) & (j % 2 == parity))
        def _mid(parity=parity):
            step(parity, 1 - parity)

    @pl.when(j == nkb)
    def _last():
        step(None, (nkb - 1) % 2)
        o = acc_ref[0:HEAD_DIM, :] / acc_ref[HEAD_DIM:HEAD_DIM + 1, :]
        for h in range(Q_PER_KV):
            o_ref[:, h * HEAD_DIM:(h + 1) * HEAD_DIM] = o[:, h * tq:(h + 1) * tq].T.astype(BF16)


def _pick_block(total, unit, cap):
    best = unit
    for b in range(unit, cap + 1, unit):
        if total % b == 0:
            best = b
    return best


def _attn_bounded_kernel(unroll, tq, kn_ref, q_ref, k_ref, vt_ref, kc_ref, vtc_ref, o_ref, qs_ref,
                         acc_ref):
    sub = vt_ref.shape[3]
    for h in range(Q_PER_KV):
        qs_ref[h * tq:(h + 1) * tq, :] = q_ref[:, h * HEAD_DIM:(h + 1) * HEAD_DIM]
    qf = qs_ref[...].astype(F32)
    norm2 = lax.dot_general(jnp.ones((8, HEAD_DIM), F32), qf * qf, (((1,), (1,)), ((), ())),
                            preferred_element_type=F32)[0:1, :]
    shift = jnp.sqrt(norm2) * kn_ref[pl.program_id(0)]

    def weighted_values(keys, vt):
        s = lax.dot_general(keys, qs_ref[...], (((1,), (1,)), ((), ())),
                            preferred_element_type=F32)
        return jnp.dot(vt, jnp.exp2(s - shift).astype(BF16), preferred_element_type=F32)

    acc = None
    for c in range(vtc_ref.shape[1]):
        d = weighted_values(kc_ref[c * sub:(c + 1) * sub, :], vtc_ref[0, c])
        acc = d if acc is None else acc + d
    acc_ref[...] = acc

    def body(t, carry):
        pv = None
        for u in range(unroll):
            c = t * unroll + u
            start = pl.multiple_of(c * sub, sub)
            d = weighted_values(k_ref[pl.ds(start, sub), :], vt_ref[0, c])
            pv = d if pv is None else pv + d
        acc_ref[...] += pv
        return carry

    lax.fori_loop(0, vt_ref.shape[1] // unroll, body, 0)
    o = acc_ref[0:HEAD_DIM, :] / acc_ref[HEAD_DIM:HEAD_DIM + 1, :]
    for h in range(Q_PER_KV):
        o_ref[:, h * HEAD_DIM:(h + 1) * HEAD_DIM] = o[:, h * tq:(h + 1) * tq].T.astype(BF16)


def _attn_bounded_call(q, k, vt, kc, vtc, k_norm_max):
    l = q.shape[0]
    nblk = vt.shape[1]
    tq = min(ATTN_TQ, l)
    gw = Q_PER_KV * HEAD_DIM
    n = Q_PER_KV * tq
    unroll = max(u for u in (8, 5, 4, 3, 2, 1) if nblk % u == 0)
    head_rows = lambda a: pl.BlockSpec((a.shape[0], HEAD_DIM), lambda g, i: (0, g))
    head_blocks = lambda a: pl.BlockSpec((1,) + a.shape[1:], lambda g, i: (g, 0, 0, 0))
    return pl.pallas_call(
        functools.partial(_attn_bounded_kernel, unroll, tq),
        grid=(N_KV_HEADS, l // tq),
        in_specs=[pl.BlockSpec(memory_space=pltpu.SMEM),
                  pl.BlockSpec((tq, gw), lambda g, i: (i, g)),
                  head_rows(k), head_blocks(vt), head_rows(kc), head_blocks(vtc)],
        out_specs=pl.BlockSpec((tq, gw), lambda g, i: (i, g)),
        out_shape=jax.ShapeDtypeStruct(q.shape, BF16),
        scratch_shapes=[pltpu.VMEM((n, HEAD_DIM), BF16), pltpu.VMEM((VT_ROWS, n), F32)],
        compiler_params=_cparams(("arbitrary", "arbitrary")),
        name="attention_bounded",
    )(k_norm_max, q, k, vt, kc, vtc)


def _attn_dispatch(q, k, vt, kc, vtc, q_gain, k_gain):
    bound = lambda gain: NORM_SLACK * math.sqrt(HEAD_DIM) * jnp.max(jnp.abs(gain.astype(F32)))
    qn = Q_SCALE * bound(q_gain)
    kn = jnp.full((N_KV_HEADS,), bound(k_gain), F32)
    safe = 2.0 * qn * jnp.max(kn) <= ATTN_MAX_GAP
    return lax.cond(safe,
                    lambda: _attn_bounded_call(q, k, vt, kc, vtc, kn),
                    lambda: _attn_call(q, jnp.concatenate([k, kc], axis=0),
                                       jnp.concatenate([vt, vtc], axis=1)))


def _attn_call(q, k_all, vt_all):
    l = q.shape[0]
    s_len = k_all.shape[0]
    sub = vt_all.shape[3]
    tq = min(ATTN_TQ, l)
    tk = _pick_block(s_len, sub, ATTN_TK)
    nkb = s_len // tk
    gw = Q_PER_KV * HEAD_DIM
    n = Q_PER_KV * tq
    return pl.pallas_call(
        functools.partial(_attn_kernel, nkb, tq),
        grid=(N_KV_HEADS, l // tq, nkb + 1),
        in_specs=[pl.BlockSpec((tq, gw), lambda g, i, j: (i, g)),
                  pl.BlockSpec((tk, HEAD_DIM), lambda g, i, j: (jnp.minimum(j, nkb - 1), g)),
                  pl.BlockSpec((1, tk // sub, VT_ROWS, sub),
                               lambda g, i, j: (g, jnp.maximum(j - 1, 0), 0, 0))],
        out_specs=pl.BlockSpec((tq, gw), lambda g, i, j: (i, g)),
        out_shape=jax.ShapeDtypeStruct(q.shape, BF16),
        scratch_shapes=[pltpu.VMEM((n, HEAD_DIM), BF16),
                        pltpu.VMEM((tk, n), F32),
                        pltpu.VMEM((tk, n), F32),
                        pltpu.VMEM((1, n), F32),
                        pltpu.VMEM((1, n), F32),
                        pltpu.VMEM((1, n), F32),
                        pltpu.VMEM((VT_ROWS, n), F32)],
        compiler_params=_cparams(("arbitrary", "arbitrary", "arbitrary")),
        name="attention",
    )(q, k_all, vt_all)


def _ssm_tables(lam_re, lam_im, log_dt, b_re, b_im, c_re, c_im, d):
    t = SSM_CHUNK
    g, p = lam_re.shape[1:]
    hdim = b_re.shape[-1]
    lr = lam_re.astype(F32)
    li = lam_im.astype(F32)
    dt = jnp.exp(log_dt.astype(F32))[..., None]
    mag = jnp.exp(lr * dt)
    ar = mag * jnp.cos(li * dt)
    ai = mag * jnp.sin(li * dt)
    den = lr * lr + li * li
    cr = ((ar - 1.0) * lr + ai * li) / den
    ci = (ai * lr - (ar - 1.0) * li) / den
    br = b_re.astype(F32)
    bi = b_im.astype(F32)
    bbr = cr[..., None] * br - ci[..., None] * bi
    bbi = cr[..., None] * bi + ci[..., None] * br
    n = jnp.arange(t + 1, dtype=F32)[:, None, None, None]
    pmag = jnp.exp(n * (lr * dt)[None])
    pr = pmag * jnp.cos(n * (li * dt)[None])
    pi = pmag * jnp.sin(n * (li * dt)[None])
    cre = c_re.astype(F32)
    cim = c_im.astype(F32)

    gs = SSM_LANES // hdim
    n_sg = g // gs

    def selector(r, inner):
        e = np.zeros((gs, r, gs, inner, r, inner), np.float32)
        for grp in range(gs):
            e[grp, :, grp] = np.eye(r * inner, dtype=np.float32).reshape(r, inner, r, inner)
        return jnp.asarray(e.reshape(gs, r * gs * inner, r * inner), dtype=BF16)

    def expand(small):
        _, r1, i1, r2, i2 = small.shape
        blocks = small.reshape(n_sg, gs, r1 * i1, r2 * i2).astype(BF16)
        wide = jnp.einsum('sgkl,gcl->sgkc', blocks, selector(r2, i2),
                          preferred_element_type=F32).astype(BF16)
        wide = wide.reshape(n_sg, gs, r1, i1, r2 * gs * i2)
        return jnp.transpose(wide, (0, 2, 1, 3, 4)).reshape(n_sg, r1 * gs * i1, r2 * gs * i2)

    def summ(direction, powers):
        er = pr[powers, direction][..., None] * bbr[direction][None] \
            - pi[powers, direction][..., None] * bbi[direction][None]
        ei = pr[powers, direction][..., None] * bbi[direction][None] \
            + pi[powers, direction][..., None] * bbr[direction][None]
        tr = lambda e: jnp.transpose(e, (1, 0, 3, 2))
        return expand(jnp.stack([tr(er), tr(ei)], axis=3))
    steps = jnp.arange(t)
    p_f = summ(0, t - 1 - steps)
    p_b = summ(1, steps)

    row = lambda v: v.reshape(n_sg, gs * p)
    a_row = jnp.concatenate([row(pr[t, 0]), row(pr[t, 1]), row(pi[t, 0]), row(pi[t, 1])],
                            axis=-1)[:, None, :]

    def kern(direction):
        er = pr[:t, direction][..., None] * bbr[direction][None] \
            - pi[:t, direction][..., None] * bbi[direction][None]
        ei = pr[:t, direction][..., None] * bbi[direction][None] \
            + pi[:t, direction][..., None] * bbr[direction][None]
        return (jnp.einsum('gop,ngpi->ngoi', cre[direction], er)
                - jnp.einsum('gop,ngpi->ngoi', cim[direction], ei))
    kf = kern(0)
    kb = kern(1)
    s_idx = steps[:, None]
    t_idx = steps[None, :]
    lag_f = jnp.clip(t_idx - s_idx, 0, t - 1)
    lag_b = jnp.clip(s_idx - t_idx, 0, t - 1)
    mf = jnp.where((t_idx >= s_idx)[:, :, None, None, None], kf[lag_f], 0.0)
    mb = jnp.where((s_idx >= t_idx)[:, :, None, None, None], kb[lag_b], 0.0)
    dd = d.astype(F32).reshape(g, hdim)
    eye_t = jnp.eye(t, dtype=F32)[:, :, None, None, None]
    eye_c = jnp.eye(hdim, dtype=F32)[None, None, None]
    m5 = mf + mb + eye_t * eye_c * dd[None, None, :, :, None]
    m_mat = expand(jnp.transpose(m5, (2, 0, 4, 1, 3)))

    def state_out(direction, powers):
        prn = pr[powers, direction]
        pin = pi[powers, direction]
        wr = cre[direction][None] * prn[:, :, None, :] - cim[direction][None] * pin[:, :, None, :]
        wi = -cre[direction][None] * pin[:, :, None, :] - cim[direction][None] * prn[:, :, None, :]
        tr = lambda w: jnp.transpose(w, (1, 3, 0, 2))
        return expand(jnp.stack([tr(wr), tr(wi)], axis=1))
    q_f = state_out(0, steps + 1)
    q_b = state_out(1, t - steps)
    return p_f, p_b, m_mat, q_f, q_b, a_row


def _ssm_kernel(tbk, usf_ref, usb_ref, pf_ref, pb_ref, m_ref, qf_ref, qb_ref, a_ref, h0_ref,
                yf_ref, yb_ref, hend_ref, xf_ref, xb_ref, sf_ref, sb_ref, hf_ref, hb_ref, h_ref):
    t = SSM_CHUNK
    lanes = SSM_LANES

    @pl.when(pl.program_id(1) == 0)
    def _init():
        h_ref[...] = h0_ref[0]

    for tt in range(t):
        cols = slice(tt * lanes, (tt + 1) * lanes)
        xf_ref[:, cols] = usf_ref[pl.ds(tt, tbk, stride=t), :].astype(BF16)
        xb_ref[:, cols] = usb_ref[pl.ds(tt, tbk, stride=t), :].astype(BF16)
    sf_ref[...] = jnp.dot(xf_ref[...], pf_ref[0], preferred_element_type=F32)
    sb_ref[...] = jnp.dot(xb_ref[...], pb_ref[0], preferred_element_type=F32)

    w = a_ref.shape[2] // 4
    a = a_ref[0]
    arf, arb, aif, aib = (a[:, i * w:(i + 1) * w] for i in range(4))
    h = h_ref[...]

    def body(k, carry):
        hrf, hrb, hif, hib = carry
        kb = tbk - 1 - k
        hf_ref[pl.ds(k, 1), 0:w] = hrf
        hf_ref[pl.ds(k, 1), w:2 * w] = hif
        hb_ref[pl.ds(kb, 1), 0:w] = hrb
        hb_ref[pl.ds(kb, 1), w:2 * w] = hib
        sf = sf_ref[pl.ds(k, 1), :]
        sb = sb_ref[pl.ds(kb, 1), :]
        return (arf * hrf - aif * hif + sf[:, 0:w], arb * hrb - aib * hib + sb[:, 0:w],
                arf * hif + aif * hrf + sf[:, w:2 * w], arb * hib + aib * hrb + sb[:, w:2 * w])

    carry = lax.fori_loop(0, tbk, body, tuple(h[:, i * w:(i + 1) * w] for i in range(4)),
                          unroll=8)
    h = jnp.concatenate(carry, axis=1)
    h_ref[...] = h
    hend_ref[0] = h

    yf = (jnp.dot(xf_ref[...], m_ref[0], preferred_element_type=F32)
          + jnp.dot(hf_ref[...].astype(BF16), qf_ref[0], preferred_element_type=F32))
    yb = jnp.dot(hb_ref[...].astype(BF16), qb_ref[0], preferred_element_type=F32)
    for tt in range(t):
        cols = slice(tt * lanes, (tt + 1) * lanes)
        yf_ref[pl.ds(tt, tbk, stride=t), :] = yf[:, cols]
        yb_ref[pl.ds(tt, tbk, stride=t), :] = yb[:, cols]


def _ssm_apply(us, tables, h0):
    p_f, p_b, m_mat, q_f, q_b, a_row = tables
    l, w = us.shape
    n_sg = w // SSM_LANES
    n = l // SSM_CHUNK
    tbk = min(256, n)
    nb = n // tbk
    rows = tbk * SSM_CHUNK
    kx = SSM_CHUNK * SSM_LANES
    sw = p_f.shape[2]
    hw = a_row.shape[2]
    per_sg = lambda a: pl.BlockSpec((1,) + a.shape[1:], lambda s, i: (s, 0, 0))
    fwd = pl.BlockSpec((rows, SSM_LANES), lambda s, i: (i, s))
    bwd = pl.BlockSpec((rows, SSM_LANES), lambda s, i: (nb - 1 - i, s))
    return pl.pallas_call(
        functools.partial(_ssm_kernel, tbk),
        grid=(n_sg, nb),
        in_specs=[fwd, bwd, per_sg(p_f), per_sg(p_b), per_sg(m_mat), per_sg(q_f), per_sg(q_b),
                  per_sg(a_row), per_sg(h0)],
        out_specs=[fwd, bwd, per_sg(h0)],
        out_shape=[jax.ShapeDtypeStruct((l, w), F32), jax.ShapeDtypeStruct((l, w), F32),
                   jax.ShapeDtypeStruct(h0.shape, F32)],
        scratch_shapes=[pltpu.VMEM((tbk, kx), BF16), pltpu.VMEM((tbk, kx), BF16),
                        pltpu.VMEM((tbk, sw), F32), pltpu.VMEM((tbk, sw), F32),
                        pltpu.VMEM((tbk, sw), F32), pltpu.VMEM((tbk, sw), F32),
                        pltpu.VMEM((1, hw), F32)],
        compiler_params=_cparams(("arbitrary", "arbitrary")),
        name="ssm_scan",
    )(us, us, p_f, p_b, m_mat, q_f, q_b, a_row, h0)


def _dft_tables(l, n1, hw):
    n2 = l // n1
    scale = 1.0 / math.sqrt(l * hw)
    if n1 == 1:
        t1 = None
    else:
        ang = 2.0 * np.pi * ((np.arange(n1)[:, None] * np.arange(n1)[None, :]) % n1) / n1
        c, s = np.cos(ang), np.sin(ang)
        t1 = jnp.asarray(np.block([[c, -s], [-s, -c]]), dtype=BF16)
    k1 = np.arange(n1)[:, None, None]
    k2 = np.arange(n2)[None, :, None]
    m = np.arange(n2)[None, None, :]
    ang = 2.0 * np.pi * ((m * (n1 * k2 + k1)) % l) / l
    sign = -1.0 if n1 == 1 else 1.0
    g = np.concatenate([np.cos(ang), sign * np.sin(ang)], axis=-1) * scale
    return t1, jnp.asarray(g, dtype=BF16)


def _dft1_kernel(t_ref, x_ref, z_ref):
    z_ref[...] = jnp.dot(t_ref[...], x_ref[...], preferred_element_type=F32).astype(z_ref.dtype)


def _dft1_call(t1, ab2d):
    r, n = ab2d.shape
    tn = min(4096, n)
    return pl.pallas_call(
        _dft1_kernel,
        grid=(n // tn,),
        in_specs=[pl.BlockSpec((r, r), lambda j: (0, 0)),
                  pl.BlockSpec((r, tn), lambda j: (0, j))],
        out_specs=pl.BlockSpec((r, tn), lambda j: (0, j)),
        out_shape=jax.ShapeDtypeStruct((r, n), BF16),
        compiler_params=_cparams(("arbitrary",)),
        name="dft_stage1",
    )(t1, ab2d)


def _dft2_kernel(bt, zr_ref, zi_ref, g_ref, w_ref, o_ref):
    fw = w_ref.shape[1]
    for b in range(bt):
        z = jnp.concatenate([zr_ref[b], zi_ref[b]], axis=0)
        x = jnp.dot(g_ref[b], z, preferred_element_type=F32)
        o_ref[:, b * fw:(b + 1) * fw] = jnp.dot(
            x.astype(BF16), w_ref[0], preferred_element_type=F32).astype(o_ref.dtype)


def _dft2_call(z3, g, fourier_w, layer):
    n1x2, n2, fw = z3.shape
    n1 = n1x2 // 2
    bt = min(8, n1)
    return pl.pallas_call(
        functools.partial(_dft2_kernel, bt),
        grid=(n1 // bt,),
        in_specs=[pl.BlockSpec((bt, n2, fw), lambda b: (b, 0, 0)),
                  pl.BlockSpec((bt, n2, fw), lambda b: (b + n1 // bt, 0, 0)),
                  pl.BlockSpec((bt, n2, 2 * n2), lambda b: (b, 0, 0)),
                  _layer_block(fourier_w, layer)],
        out_specs=pl.BlockSpec((n2, bt * fw), lambda b: (0, b)),
        out_shape=jax.ShapeDtypeStruct((n2, n1 * fw), BF16),
        compiler_params=_cparams(("arbitrary",)),
        name="dft_stage2",
    )(z3, z3, g, fourier_w)


def _fourier_apply(ab, tables, fourier_w, layer, n1):
    t1, g = tables
    _, l, fw = ab.shape
    n2 = l // n1
    if n1 == 1:
        z3 = ab
    else:
        z3 = _dft1_call(t1, ab.reshape(2 * n1, n2 * fw)).reshape(2 * n1, n2, fw)
    return _dft2_call(z3, g, fourier_w, layer).reshape(l, fw)


def _outproj_kernel(attn_ref, yf_ref, yb_ref, four_ref, x_ref, wo_ref, gw_ref, gb_ref, npost_ref,
                    gate_ref, npre_ref, sh_ref, sc_ref, xo_ref, h_ref):
    ys = yf_ref[...] + yb_ref[...]
    gl = 0.5 * ys * (1.0 + jnp.tanh(math.sqrt(2.0 / math.pi) * (ys + 0.044715 * (ys * ys * ys))))
    z = jnp.dot(gl.astype(BF16), gw_ref[0], preferred_element_type=F32) + gb_ref[...]
    ssm = (gl * jax.nn.sigmoid(z)).astype(BF16)
    cat = jnp.concatenate([attn_ref[...], ssm, four_ref[...]], axis=-1)
    y = jnp.dot(cat, wo_ref[0], preferred_element_type=F32)
    xn = x_ref[...] + gate_ref[...] * _rms(y, npost_ref[...])
    xo_ref[...] = xn
    h_ref[...] = (_rms(xn, npre_ref[...]) * (1.0 + sc_ref[...]) + sh_ref[...]).astype(BF16)


def _outproj_call(attn, yf, yb, four, x, w_out, glu_w, layer, glu_b, npost, gate, npre, shift,
                  scale):
    l, d = x.shape
    tm = min(512, l)
    row = lambda i: (i, 0)
    fix = lambda i: (0, 0)
    vec = pl.BlockSpec((1, d), fix)
    return pl.pallas_call(
        _outproj_kernel,
        grid=(l // tm,),
        in_specs=[pl.BlockSpec((tm, attn.shape[1]), row),
                  pl.BlockSpec((tm, yf.shape[1]), row),
                  pl.BlockSpec((tm, yb.shape[1]), row),
                  pl.BlockSpec((tm, four.shape[1]), row),
                  pl.BlockSpec((tm, d), row),
                  _layer_block(w_out, layer),
                  _layer_block(glu_w, layer),
                  pl.BlockSpec((1, glu_w.shape[2]), fix),
                  vec, vec, vec, vec, vec],
        out_specs=[pl.BlockSpec((tm, d), row), pl.BlockSpec((tm, d), row)],
        out_shape=[jax.ShapeDtypeStruct((l, d), F32), jax.ShapeDtypeStruct((l, d), BF16)],
        compiler_params=_cparams(("arbitrary",)),
        name="out_proj",
    )(attn, yf, yb, four, x, w_out, glu_w, glu_b, npost, gate, npre, shift, scale)


def _ffn_kernel(nj1, nj2, th, tn, h_ref, x_ref, wg_ref, wu_ref, wd_ref, npost_ref, gate_ref, o_ref,
                act_ref, tmp_ref, ytmp_ref):
    j = pl.program_id(1)

    @pl.when(j < nj1)
    def _up():
        h = h_ref[...]
        a = jnp.dot(h, wg_ref[0], preferred_element_type=F32)
        u = jnp.dot(h, wu_ref[0], preferred_element_type=F32)
        tmp_ref[...] = (a * jax.nn.sigmoid(a) * u).astype(BF16)

    for c in range(nj1):
        @pl.when(j == c)
        def _place_up(c=c):
            act_ref[:, c * th:(c + 1) * th] = tmp_ref[...]

    @pl.when(j >= nj1)
    def _down():
        ytmp_ref[...] = jnp.dot(act_ref[...], wd_ref[0], preferred_element_type=F32)

    for c in range(nj2):
        @pl.when(j == nj1 + c)
        def _place_down(c=c):
            o_ref[:, c * tn:(c + 1) * tn] = ytmp_ref[...]

    @pl.when(j == nj1 + nj2 - 1)
    def _finish():
        o_ref[...] = x_ref[...] + gate_ref[...] * _rms(o_ref[...], npost_ref[...])


def _ffn_call(h, x, w_gate, w_up, w_down, layer, npost, gate):
    l, d = x.shape
    fh = w_gate.shape[2]
    th, tn = FFN_TH, FFN_TN
    nj1 = fh // th
    nj2 = d // tn
    tm = min(FFN_TM, l)
    vec = pl.BlockSpec((1, d), lambda i, j: (0, 0))
    up = pl.BlockSpec((1, d, th), lambda i, j: (layer, 0, jnp.minimum(j, nj1 - 1)))
    once = pl.Buffered(1)
    return pl.pallas_call(
        functools.partial(_ffn_kernel, nj1, nj2, th, tn),
        grid=(l // tm, nj1 + nj2),
        in_specs=[pl.BlockSpec((tm, d), lambda i, j: (i, 0)),
                  pl.BlockSpec((tm, d), lambda i, j: (i, 0), pipeline_mode=once),
                  up, up,
                  pl.BlockSpec((1, fh, tn), lambda i, j: (layer, 0, jnp.maximum(j - nj1, 0))),
                  vec, vec],
        out_specs=pl.BlockSpec((tm, d), lambda i, j: (i, 0), pipeline_mode=once),
        out_shape=jax.ShapeDtypeStruct((l, d), F32),
        scratch_shapes=[pltpu.VMEM((tm, fh), BF16), pltpu.VMEM((tm, th), BF16),
                        pltpu.VMEM((tm, tn), F32)],
        compiler_params=_cparams(("arbitrary", "arbitrary"), FFN_VMEM_MB),
        name="ffn",
    )(h, x, w_gate, w_up, w_down, npost, gate)


def _rope_tables(l):
    t = np.arange(l)
    row = (t // GRID_W).astype(np.float32)
    col = (t % GRID_W).astype(np.float32)
    freqs = np.float32(ROPE_THETA) ** (-np.arange(ROPE_PAIRS, dtype=np.float32) / np.float32(ROPE_PAIRS))
    ang_r = (row[:, None] * freqs).astype(np.float32)
    ang_c = (col[:, None] * freqs).astype(np.float32)
    cos = np.concatenate([np.cos(ang_r)] * 2 + [np.cos(ang_c)] * 2, axis=-1)
    sin = np.concatenate([-np.sin(ang_r), np.sin(ang_r), -np.sin(ang_c), np.sin(ang_c)], axis=-1)
    return jnp.asarray(cos, dtype=F32), jnp.asarray(sin, dtype=F32)


def _channel_dft_table(fw):
    hw = FOURIER_HEAD_DIM
    ang = 2.0 * np.pi * ((np.arange(hw)[:, None] * np.arange(hw)[None, :]) % hw) / hw
    eye = np.eye(fw // hw)
    return jnp.asarray(np.concatenate([np.kron(eye, np.cos(ang)), np.kron(eye, np.sin(ang))], axis=1),
                       dtype=BF16)


def _dft_split(l):
    n1 = 1
    while n1 * n1 < l:
        n1 *= 2
    return n1 if (l >= 1024 and n1 * n1 == l) else 1


def kernel(x, c, ctx, c_ctx, ada_w, ada_b, norm_mix_pre, norm_mix_post, norm_ffn_pre, norm_ffn_post, w_in, q_norm, k_norm, ssm_lam_re, ssm_lam_im, ssm_log_dt, ssm_b_re, ssm_b_im, ssm_c_re, ssm_c_im, ssm_d, ssm_glu_w, ssm_glu_b, fourier_w, w_out, ffn_w_gate, ffn_w_up, ffn_w_down):
    depth = ada_w.shape[0]
    _, l, d = x.shape
    n_ctx = ctx.shape[1]
    fw = fourier_w.shape[1]
    sw = ssm_d.shape[1]

    cond = jnp.zeros((8, d), F32).at[0].set(c[0]).at[1].set(c_ctx)
    mod = _ada_call(cond, ada_w, ada_b)

    cos, sin = _rope_tables(l)
    zero_tab = jnp.zeros((n_ctx, HEAD_DIM), F32)
    cs = _channel_dft_table(fw)
    n1 = _dft_split(l)
    n1c = _dft_split(n_ctx)
    dft_x = _dft_tables(l, n1, FOURIER_HEAD_DIM)
    dft_c = _dft_tables(n_ctx, n1c, FOURIER_HEAD_DIM)
    h0 = jnp.zeros((sw // SSM_LANES, 1, 4 * (SSM_LANES // SSM_GROUP) * SSM_STATE), F32)

    w_in_b = w_in.astype(BF16)
    w_out_b = w_out.astype(BF16)
    glu_w_b = ssm_glu_w.astype(BF16)
    four_w_b = fourier_w.astype(BF16)
    wg_b = ffn_w_gate.astype(BF16)
    wu_b = ffn_w_up.astype(BF16)
    wd_b = ffn_w_down.astype(BF16)

    xs = x[0]
    xc = ctx[0]
    for layer in range(depth):
        need_ctx = layer < depth - 1
        vecs = lambda r: [mod[layer, r:r + 1, i * d:(i + 1) * d] for i in range(6)]
        sh_m, sc_m, g_m, sh_f, sc_f, g_f = vecs(0)
        shc_m, scc_m, gc_m, shc_f, scc_f, gc_f = vecs(1)
        row = lambda a: a[layer].reshape(1, -1)
        tables = _ssm_tables(ssm_lam_re[layer], ssm_lam_im[layer], ssm_log_dt[layer],
                             ssm_b_re[layer], ssm_b_im[layer], ssm_c_re[layer], ssm_c_im[layer],
                             ssm_d[layer])

        qc, kc, vtc, usc, abc = _inproj_call(
            xc, row(norm_mix_pre), shc_m, scc_m, w_in_b, layer, row(q_norm), row(k_norm),
            zero_tab, zero_tab, cs, False)
        yfc, ybc, hend_c = _ssm_apply(usc, tables, h0)

        q, k, vt, us, ab = _inproj_call(xs, row(norm_mix_pre), sh_m, sc_m, w_in_b, layer,
                                        row(q_norm), row(k_norm), cos, sin, cs, True)
        attn = _attn_dispatch(q, k, vt, kc, vtc, q_norm[layer], k_norm[layer])
        yf, yb, _ = _ssm_apply(us, tables, hend_c)
        four = _fourier_apply(ab, dft_x, four_w_b, layer, n1)
        xs, hs = _outproj_call(attn, yf, yb, four, xs, w_out_b, glu_w_b, layer, row(ssm_glu_b),
                               row(norm_mix_post), g_m, row(norm_ffn_pre), sh_f, sc_f)
        xs = _ffn_call(hs, xs, wg_b, wu_b, wd_b, layer, row(norm_ffn_post), g_f)

        if need_ctx:
            attn_c = _attn_call(qc, kc, vtc)
            four_c = _fourier_apply(abc, dft_c, four_w_b, layer, n1c)
            xc, hc = _outproj_call(attn_c, yfc, ybc, four_c, xc, w_out_b, glu_w_b, layer,
                                   row(ssm_glu_b), row(norm_mix_post), gc_m, row(norm_ffn_pre),
                                   shc_f, scc_f)
            xc = _ffn_call(hc, xc, wg_b, wu_b, wd_b, layer, row(norm_ffn_post), gc_f)
    return xs[None]
```

```python
import functools
import math

import numpy as np
import jax
import jax.numpy as jnp
from jax import lax
from jax.experimental import pallas as pl
from jax.experimental.pallas import tpu as pltpu

F32 = jnp.float32
BF16 = jnp.bfloat16

HEAD_DIM = 128
N_KV_HEADS = 2
Q_PER_KV = 4
GRID_W = 64
ROPE_THETA = 10000.0
ROPE_PAIRS = HEAD_DIM // 4
SSM_GROUP = 16
SSM_STATE = 64
FOURIER_HEAD_DIM = 128
NORM_EPS = 1e-6
SSM_CHUNK = 8
SSM_LANES = 128
LOG2E = 1.4426950408889634
ATTN_TQ = 512
ATTN_TK = 1280
Q_SCALE = HEAD_DIM ** -0.5 * LOG2E
NORM_SLACK = 1.01
ATTN_MAX_GAP = 100.0
ATTN_SUB = 256
FFN_TM = 1024
FFN_TH = 512
FFN_TN = 256
VT_PAD = 16
VT_ROWS = HEAD_DIM + VT_PAD

VMEM_LIMIT_MB = 56


def _cparams(semantics, vmem_mb=VMEM_LIMIT_MB):
    return pltpu.CompilerParams(dimension_semantics=semantics,
                                vmem_limit_bytes=vmem_mb * 2 ** 20)


def _rms(x, gain):
    return x * lax.rsqrt(jnp.mean(x * x, axis=-1, keepdims=True) + NORM_EPS) * gain


def _ada_kernel(s_ref, w_ref, b_ref, o_ref):
    s = s_ref[...]
    act = (s * jax.nn.sigmoid(s)).astype(BF16)
    o_ref[0] = jnp.dot(act, w_ref[0].astype(BF16), preferred_element_type=F32) + b_ref[0]


def _ada_call(cond, ada_w, ada_b):
    depth, d, n = ada_w.shape
    tn = 1024
    return pl.pallas_call(
        _ada_kernel,
        grid=(depth, n // tn),
        in_specs=[pl.BlockSpec((8, d), lambda l, j: (0, 0)),
                  pl.BlockSpec((1, d, tn), lambda l, j: (l, 0, j)),
                  pl.BlockSpec((1, 1, tn), lambda l, j: (l, 0, j))],
        out_specs=pl.BlockSpec((1, 8, tn), lambda l, j: (l, 0, j)),
        out_shape=jax.ShapeDtypeStruct((depth, 8, n), F32),
        compiler_params=_cparams(("arbitrary", "arbitrary")),
        name="ada_ln",
    )(cond, ada_w, ada_b.reshape(depth, 1, n))


def _inproj_kernel(rope, x_ref, g_ref, sh_ref, sc_ref, w_ref, qn_ref, kn_ref, cos_ref, sin_ref,
                   cs_ref, q_ref, k_ref, vt_ref, us_ref, ab_ref):
    x = x_ref[...]
    h = (_rms(x, g_ref[...]) * (1.0 + sc_ref[...]) + sh_ref[...]).astype(BF16)
    n_q = q_ref.shape[1] // HEAD_DIM
    n_k = k_ref.shape[1] // HEAD_DIM
    q_end = n_q * HEAD_DIM
    k_end = q_end + n_k * HEAD_DIM
    v_end = k_end + n_k * HEAD_DIM
    s_end = v_end + us_ref.shape[1]
    f_end = s_end + ab_ref.shape[2]

    if rope:
        cos = cos_ref[...]
        sin = sin_ref[...]
        lane = lax.broadcasted_iota(jnp.int32, cos.shape, 1)
        low = (lane % 64) < 32

    def head(t, gain, scale):
        t = _rms(t, gain)
        if rope:
            sw = jnp.where(low, pltpu.roll(t, HEAD_DIM - 32, 1), pltpu.roll(t, 32, 1))
            t = t * cos + sw * sin
        if scale != 1.0:
            t = t * scale
        return t.astype(BF16)

    qk = jnp.dot(h, w_ref[0, :, 0:k_end], preferred_element_type=F32)
    for i in range(n_q):
        q_ref[:, i * HEAD_DIM:(i + 1) * HEAD_DIM] = head(
            qk[:, i * HEAD_DIM:(i + 1) * HEAD_DIM], qn_ref[...], Q_SCALE)
    for i in range(n_k):
        k_ref[:, i * HEAD_DIM:(i + 1) * HEAD_DIM] = head(
            qk[:, q_end + i * HEAD_DIM:q_end + (i + 1) * HEAD_DIM], kn_ref[...], 1.0)
    rest = jnp.dot(h, w_ref[0, :, k_end:f_end], preferred_element_type=F32)
    ones_rows = (lax.broadcasted_iota(jnp.int32, (VT_PAD, ATTN_SUB), 0) == 0).astype(BF16)
    for i in range(n_k):
        vt = rest[:, i * HEAD_DIM:(i + 1) * HEAD_DIM].T.astype(BF16)
        for b in range(vt_ref.shape[1]):
            vt_ref[i, b, 0:HEAD_DIM, :] = vt[:, b * ATTN_SUB:(b + 1) * ATTN_SUB]
            vt_ref[i, b, HEAD_DIM:VT_ROWS, :] = ones_rows
    us_ref[...] = rest[:, v_end - k_end:s_end - k_end]
    uf = rest[:, s_end - k_end:f_end - k_end].astype(BF16)
    ab = jnp.dot(uf, cs_ref[...], preferred_element_type=F32)
    fw = ab_ref.shape[2]
    ab_ref[0] = ab[:, 0:fw].astype(BF16)
    ab_ref[1] = ab[:, fw:2 * fw].astype(BF16)


def _layer_block(w, layer):
    return pl.BlockSpec((1,) + w.shape[1:], lambda *_: (layer, 0, 0))


def _inproj_call(x, gain, shift, scale, w_in, layer, qn, kn, cos, sin, cs, rope):
    l, d = x.shape
    tm = min(512, l)
    qw = Q_PER_KV * N_KV_HEADS * HEAD_DIM
    kw = N_KV_HEADS * HEAD_DIM
    fw = cs.shape[0]
    sw = w_in.shape[2] - qw - 2 * kw - fw
    row = lambda i: (i, 0)
    fix = lambda i: (0, 0)
    return pl.pallas_call(
        functools.partial(_inproj_kernel, rope),
        grid=(l // tm,),
        in_specs=[pl.BlockSpec((tm, d), row),
                  pl.BlockSpec((1, d), fix), pl.BlockSpec((1, d), fix), pl.BlockSpec((1, d), fix),
                  _layer_block(w_in, layer),
                  pl.BlockSpec((1, HEAD_DIM), fix), pl.BlockSpec((1, HEAD_DIM), fix),
                  pl.BlockSpec((tm, HEAD_DIM), row), pl.BlockSpec((tm, HEAD_DIM), row),
                  pl.BlockSpec(cs.shape, fix)],
        out_specs=[pl.BlockSpec((tm, qw), row),
                   pl.BlockSpec((tm, kw), row),
                   pl.BlockSpec((N_KV_HEADS, tm // ATTN_SUB, VT_ROWS, ATTN_SUB),
                                lambda i: (0, i, 0, 0)),
                   pl.BlockSpec((tm, sw), row),
                   pl.BlockSpec((2, tm, fw), lambda i: (0, i, 0))],
        out_shape=[jax.ShapeDtypeStruct((l, qw), BF16),
                   jax.ShapeDtypeStruct((l, kw), BF16),
                   jax.ShapeDtypeStruct((N_KV_HEADS, l // ATTN_SUB, VT_ROWS, ATTN_SUB), BF16),
                   jax.ShapeDtypeStruct((l, sw), F32),
                   jax.ShapeDtypeStruct((2, l, fw), BF16)],
        compiler_params=_cparams(("arbitrary",)),
        name="in_proj",
    )(x, gain, shift, scale, w_in, qn, kn, cos, sin, cs)


def _attn_kernel(nkb, tq, q_ref, k_ref, vt_ref, o_ref, qs_ref, s0_ref, s1_ref, bm0_ref, bm1_ref,
                 m_ref, acc_ref):
    j = pl.program_id(2)
    s_refs = (s0_ref, s1_ref)
    bm_refs = (bm0_ref, bm1_ref)

    tk = k_ref.shape[0]
    sub = vt_ref.shape[3]

    def step(score_slot, absorb_slot):
        if absorb_slot is not None:
            m_prev = m_ref[...]
            m_new = jnp.maximum(m_prev, bm_refs[absorb_slot][...])
            alpha = jnp.exp2(m_prev - m_new)
            pv = jnp.zeros(acc_ref.shape, F32)
        bmax = None
        for c in range(tk // sub):
            rows = slice(c * sub, (c + 1) * sub)
            if score_slot is not None:
                s = lax.dot_general(k_ref[rows, :], qs_ref[...], (((1,), (1,)), ((), ())),
                                    preferred_element_type=F32)
                s_refs[score_slot][rows, :] = s
                cmax = jnp.max(s, axis=0, keepdims=True)
                bmax = cmax if bmax is None else jnp.maximum(bmax, cmax)
            if absorb_slot is not None:
                p = jnp.exp2((s_refs[absorb_slot][rows, :] - m_new).astype(BF16))
                pv = pv + jnp.dot(vt_ref[0, c], p, preferred_element_type=F32)
        if score_slot is not None:
            bm_refs[score_slot][...] = bmax
        if absorb_slot is not None:
            acc_ref[...] = alpha * acc_ref[...] + pv
            m_ref[...] = m_new

    @pl.when(j == 0)
    def _first():
        for h in range(Q_PER_KV):
            qs_ref[h * tq:(h + 1) * tq, :] = q_ref[:, h * HEAD_DIM:(h + 1) * HEAD_DIM]
        m_ref[...] = jnp.full(m_ref.shape, -jnp.inf, F32)
        acc_ref[...] = jnp.zeros(acc_ref.shape, F32)
        step(0, None)

    for parity in (0, 1):
        @pl.when((j > 0) & (j < nkb) & (j % 2 == parity))
        def _mid(parity=parity):
            step(parity, 1 - parity)

    @pl.when(j == nkb)
    def _last():
        step(None, (nkb - 1) % 2)
        o = acc_ref[0:HEAD_DIM, :] / acc_ref[HEAD_DIM:HEAD_DIM + 1, :]
        for h in range(Q_PER_KV):
            o_ref[:, h * HEAD_DIM:(h + 1) * HEAD_DIM] = o[:, h * tq:(h + 1) * tq].T.astype(BF16)


def _pick_block(total, unit, cap):
    best = unit
    for b in range(unit, cap + 1, unit):
        if total % b == 0:
            best = b
    return best


def _attn_bounded_kernel(unroll, tq, kn_ref, q_ref, k_ref, vt_ref, kc_ref, vtc_ref, o_ref, qs_ref,
                         acc_ref):
    sub = vt_ref.shape[3]
    for h in range(Q_PER_KV):
        qs_ref[h * tq:(h + 1) * tq, :] = q_ref[:, h * HEAD_DIM:(h + 1) * HEAD_DIM]
    qf = qs_ref[...].astype(F32)
    norm2 = lax.dot_general(jnp.ones((8, HEAD_DIM), F32), qf * qf, (((1,), (1,)), ((), ())),
                            preferred_element_type=F32)[0:1, :]
    shift = jnp.sqrt(norm2) * kn_ref[pl.program_id(0)]

    def weighted_values(keys, vt):
        s = lax.dot_general(keys, qs_ref[...], (((1,), (1,)), ((), ())),
                            preferred_element_type=F32)
        return jnp.dot(vt, jnp.exp2(s - shift).astype(BF16), preferred_element_type=F32)

    acc = None
    for c in range(vtc_ref.shape[1]):
        d = weighted_values(kc_ref[c * sub:(c + 1) * sub, :], vtc_ref[0, c])
        acc = d if acc is None else acc + d
    acc_ref[...] = acc

    def body(t, carry):
        pv = None
        for u in range(unroll):
            c = t * unroll + u
            start = pl.multiple_of(c * sub, sub)
            d = weighted_values(k_ref[pl.ds(start, sub), :], vt_ref[0, c])
            pv = d if pv is None else pv + d
        acc_ref[...] += pv
        return carry

    lax.fori_loop(0, vt_ref.shape[1] // unroll, body, 0)
    o = acc_ref[0:HEAD_DIM, :] / acc_ref[HEAD_DIM:HEAD_DIM + 1, :]
    for h in range(Q_PER_KV):
        o_ref[:, h * HEAD_DIM:(h + 1) * HEAD_DIM] = o[:, h * tq:(h + 1) * tq].T.astype(BF16)


def _attn_bounded_call(q, k, vt, kc, vtc, k_norm_max):
    l = q.shape[0]
    nblk = vt.shape[1]
    tq = min(ATTN_TQ, l)
    gw = Q_PER_KV * HEAD_DIM
    n = Q_PER_KV * tq
    unroll = max(u for u in (8, 5, 4, 3, 2, 1) if nblk % u == 0)
    head_rows = lambda a: pl.BlockSpec((a.shape[0], HEAD_DIM), lambda g, i: (0, g))
    head_blocks = lambda a: pl.BlockSpec((1,) + a.shape[1:], lambda g, i: (g, 0, 0, 0))
    return pl.pallas_call(
        functools.partial(_attn_bounded_kernel, unroll, tq),
        grid=(N_KV_HEADS, l // tq),
        in_specs=[pl.BlockSpec(memory_space=pltpu.SMEM),
                  pl.BlockSpec((tq, gw), lambda g, i: (i, g)),
                  head_rows(k), head_blocks(vt), head_rows(kc), head_blocks(vtc)],
        out_specs=pl.BlockSpec((tq, gw), lambda g, i: (i, g)),
        out_shape=jax.ShapeDtypeStruct(q.shape, BF16),
        scratch_shapes=[pltpu.VMEM((n, HEAD_DIM), BF16), pltpu.VMEM((VT_ROWS, n), F32)],
        compiler_params=_cparams(("arbitrary", "arbitrary")),
        name="attention_bounded",
    )(k_norm_max, q, k, vt, kc, vtc)


def _attn_dispatch(q, k, vt, kc, vtc, q_gain, k_gain):
    bound = lambda gain: NORM_SLACK * math.sqrt(HEAD_DIM) * jnp.max(jnp.abs(gain.astype(F32)))
    qn = Q_SCALE * bound(q_gain)
    kn = jnp.full((N_KV_HEADS,), bound(k_gain), F32)
    safe = 2.0 * qn * jnp.max(kn) <= ATTN_MAX_GAP
    return lax.cond(safe,
                    lambda: _attn_bounded_call(q, k, vt, kc, vtc, kn),
                    lambda: _attn_call(q, jnp.concatenate([k, kc], axis=0),
                                       jnp.concatenate([vt, vtc], axis=1)))


def _attn_call(q, k_all, vt_all):
    l = q.shape[0]
    s_len = k_all.shape[0]
    sub = vt_all.shape[3]
    tq = min(ATTN_TQ, l)
    tk = _pick_block(s_len, sub, ATTN_TK)
    nkb = s_len // tk
    gw = Q_PER_KV * HEAD_DIM
    n = Q_PER_KV * tq
    return pl.pallas_call(
        functools.partial(_attn_kernel, nkb, tq),
        grid=(N_KV_HEADS, l // tq, nkb + 1),
        in_specs=[pl.BlockSpec((tq, gw), lambda g, i, j: (i, g)),
                  pl.BlockSpec((tk, HEAD_DIM), lambda g, i, j: (jnp.minimum(j, nkb - 1), g)),
                  pl.BlockSpec((1, tk // sub, VT_ROWS, sub),
                               lambda g, i, j: (g, jnp.maximum(j - 1, 0), 0, 0))],
        out_specs=pl.BlockSpec((tq, gw), lambda g, i, j: (i, g)),
        out_shape=jax.ShapeDtypeStruct(q.shape, BF16),
        scratch_shapes=[pltpu.VMEM((n, HEAD_DIM), BF16),
                        pltpu.VMEM((tk, n), F32),
                        pltpu.VMEM((tk, n), F32),
                        pltpu.VMEM((1, n), F32),
                        pltpu.VMEM((1, n), F32),
                        pltpu.VMEM((1, n), F32),
                        pltpu.VMEM((VT_ROWS, n), F32)],
        compiler_params=_cparams(("arbitrary", "arbitrary", "arbitrary")),
        name="attention",
    )(q, k_all, vt_all)


def _ssm_tables(lam_re, lam_im, log_dt, b_re, b_im, c_re, c_im, d):
    t = SSM_CHUNK
    g, p = lam_re.shape[1:]
    hdim = b_re.shape[-1]
    lr = lam_re.astype(F32)
    li = lam_im.astype(F32)
    dt = jnp.exp(log_dt.astype(F32))[..., None]
    mag = jnp.exp(lr * dt)
    ar = mag * jnp.cos(li * dt)
    ai = mag * jnp.sin(li * dt)
    den = lr * lr + li * li
    cr = ((ar - 1.0) * lr + ai * li) / den
    ci = (ai * lr - (ar - 1.0) * li) / den
    br = b_re.astype(F32)
    bi = b_im.astype(F32)
    bbr = cr[..., None] * br - ci[..., None] * bi
    bbi = cr[..., None] * bi + ci[..., None] * br
    n = jnp.arange(t + 1, dtype=F32)[:, None, None, None]
    pmag = jnp.exp(n * (lr * dt)[None])
    pr = pmag * jnp.cos(n * (li * dt)[None])
    pi = pmag * jnp.sin(n * (li * dt)[None])
    cre = c_re.astype(F32)
    cim = c_im.astype(F32)

    gs = SSM_LANES // hdim
    n_sg = g // gs

    def selector(r, inner):
        e = np.zeros((gs, r, gs, inner, r, inner), np.float32)
        for grp in range(gs):
            e[grp, :, grp] = np.eye(r * inner, dtype=np.float32).reshape(r, inner, r, inner)
        return jnp.asarray(e.reshape(gs, r * gs * inner, r * inner), dtype=BF16)

    def expand(small):
        _, r1, i1, r2, i2 = small.shape
        blocks = small.reshape(n_sg, gs, r1 * i1, r2 * i2).astype(BF16)
        wide = jnp.einsum('sgkl,gcl->sgkc', blocks, selector(r2, i2),
                          preferred_element_type=F32).astype(BF16)
        wide = wide.reshape(n_sg, gs, r1, i1, r2 * gs * i2)
        return jnp.transpose(wide, (0, 2, 1, 3, 4)).reshape(n_sg, r1 * gs * i1, r2 * gs * i2)

    def summ(direction, powers):
        er = pr[powers, direction][..., None] * bbr[direction][None] \
            - pi[powers, direction][..., None] * bbi[direction][None]
        ei = pr[powers, direction][..., None] * bbi[direction][None] \
            + pi[powers, direction][..., None] * bbr[direction][None]
        tr = lambda e: jnp.transpose(e, (1, 0, 3, 2))
        return expand(jnp.stack([tr(er), tr(ei)], axis=3))
    steps = jnp.arange(t)
    p_f = summ(0, t - 1 - steps)
    p_b = summ(1, steps)

    row = lambda v: v.reshape(n_sg, gs * p)
    a_row = jnp.concatenate([row(pr[t, 0]), row(pr[t, 1]), row(pi[t, 0]), row(pi[t, 1])],
                            axis=-1)[:, None, :]

    def kern(direction):
        er = pr[:t, direction][..., None] * bbr[direction][None] \
            - pi[:t, direction][..., None] * bbi[direction][None]
        ei = pr[:t, direction][..., None] * bbi[direction][None] \
            + pi[:t, direction][..., None] * bbr[direction][None]
        return (jnp.einsum('gop,ngpi->ngoi', cre[direction], er)
                - jnp.einsum('gop,ngpi->ngoi', cim[direction], ei))
    kf = kern(0)
    kb = kern(1)
    s_idx = steps[:, None]
    t_idx = steps[None, :]
    lag_f = jnp.clip(t_idx - s_idx, 0, t - 1)
    lag_b = jnp.clip(s_idx - t_idx, 0, t - 1)
    mf = jnp.where((t_idx >= s_idx)[:, :, None, None, None], kf[lag_f], 0.0)
    mb = jnp.where((s_idx >= t_idx)[:, :, None, None, None], kb[lag_b], 0.0)
    dd = d.astype(F32).reshape(g, hdim)
    eye_t = jnp.eye(t, dtype=F32)[:, :, None, None, None]
    eye_c = jnp.eye(hdim, dtype=F32)[None, None, None]
    m5 = mf + mb + eye_t * eye_c * dd[None, None, :, :, None]
    m_mat = expand(jnp.transpose(m5, (2, 0, 4, 1, 3)))

    def state_out(direction, powers):
        prn = pr[powers, direction]
        pin = pi[powers, direction]
        wr = cre[direction][None] * prn[:, :, None, :] - cim[direction][None] * pin[:, :, None, :]
        wi = -cre[direction][None] * pin[:, :, None, :] - cim[direction][None] * prn[:, :, None, :]
        tr = lambda w: jnp.transpose(w, (1, 3, 0, 2))
        return expand(jnp.stack([tr(wr), tr(wi)], axis=1))
    q_f = state_out(0, steps + 1)
    q_b = state_out(1, t - steps)
    return p_f, p_b, m_mat, q_f, q_b, a_row


def _ssm_kernel(tbk, usf_ref, usb_ref, pf_ref, pb_ref, m_ref, qf_ref, qb_ref, a_ref, h0_ref,
                yf_ref, yb_ref, hend_ref, xf_ref, xb_ref, sf_ref, sb_ref, hf_ref, hb_ref, h_ref):
    t = SSM_CHUNK
    lanes = SSM_LANES

    @pl.when(pl.program_id(1) == 0)
    def _init():
        h_ref[...] = h0_ref[0]

    for tt in range(t):
        cols = slice(tt * lanes, (tt + 1) * lanes)
        xf_ref[:, cols] = usf_ref[pl.ds(tt, tbk, stride=t), :].astype(BF16)
        xb_ref[:, cols] = usb_ref[pl.ds(tt, tbk, stride=t), :].astype(BF16)
    sf_ref[...] = jnp.dot(xf_ref[...], pf_ref[0], preferred_element_type=F32)
    sb_ref[...] = jnp.dot(xb_ref[...], pb_ref[0], preferred_element_type=F32)

    w = a_ref.shape[2] // 4
    a = a_ref[0]
    arf, arb, aif, aib = (a[:, i * w:(i + 1) * w] for i in range(4))
    h = h_ref[...]

    def body(k, carry):
        hrf, hrb, hif, hib = carry
        kb = tbk - 1 - k
        hf_ref[pl.ds(k, 1), 0:w] = hrf
        hf_ref[pl.ds(k, 1), w:2 * w] = hif
        hb_ref[pl.ds(kb, 1), 0:w] = hrb
        hb_ref[pl.ds(kb, 1), w:2 * w] = hib
        sf = sf_ref[pl.ds(k, 1), :]
        sb = sb_ref[pl.ds(kb, 1), :]
        return (arf * hrf - aif * hif + sf[:, 0:w], arb * hrb - aib * hib + sb[:, 0:w],
                arf * hif + aif * hrf + sf[:, w:2 * w], arb * hib + aib * hrb + sb[:, w:2 * w])

    carry = lax.fori_loop(0, tbk, body, tuple(h[:, i * w:(i + 1) * w] for i in range(4)),
                          unroll=8)
    h = jnp.concatenate(carry, axis=1)
    h_ref[...] = h
    hend_ref[0] = h

    yf = (jnp.dot(xf_ref[...], m_ref[0], preferred_element_type=F32)
          + jnp.dot(hf_ref[...].astype(BF16), qf_ref[0], preferred_element_type=F32))
    yb = jnp.dot(hb_ref[...].astype(BF16), qb_ref[0], preferred_element_type=F32)
    for tt in range(t):
        cols = slice(tt * lanes, (tt + 1) * lanes)
        yf_ref[pl.ds(tt, tbk, stride=t), :] = yf[:, cols]
        yb_ref[pl.ds(tt, tbk, stride=t), :] = yb[:, cols]


def _ssm_apply(us, tables, h0):
    p_f, p_b, m_mat, q_f, q_b, a_row = tables
    l, w = us.shape
    n_sg = w // SSM_LANES
    n = l // SSM_CHUNK
    tbk = min(256, n)
    nb = n // tbk
    rows = tbk * SSM_CHUNK
    kx = SSM_CHUNK * SSM_LANES
    sw = p_f.shape[2]
    hw = a_row.shape[2]
    per_sg = lambda a: pl.BlockSpec((1,) + a.shape[1:], lambda s, i: (s, 0, 0))
    fwd = pl.BlockSpec((rows, SSM_LANES), lambda s, i: (i, s))
    bwd = pl.BlockSpec((rows, SSM_LANES), lambda s, i: (nb - 1 - i, s))
    return pl.pallas_call(
        functools.partial(_ssm_kernel, tbk),
        grid=(n_sg, nb),
        in_specs=[fwd, bwd, per_sg(p_f), per_sg(p_b), per_sg(m_mat), per_sg(q_f), per_sg(q_b),
                  per_sg(a_row), per_sg(h0)],
        out_specs=[fwd, bwd, per_sg(h0)],
        out_shape=[jax.ShapeDtypeStruct((l, w), F32), jax.ShapeDtypeStruct((l, w), F32),
                   jax.ShapeDtypeStruct(h0.shape, F32)],
        scratch_shapes=[pltpu.VMEM((tbk, kx), BF16), pltpu.VMEM((tbk, kx), BF16),
                        pltpu.VMEM((tbk, sw), F32), pltpu.VMEM((tbk, sw), F32),
                        pltpu.VMEM((tbk, sw), F32), pltpu.VMEM((tbk, sw), F32),
                        pltpu.VMEM((1, hw), F32)],
        compiler_params=_cparams(("arbitrary", "arbitrary")),
        name="ssm_scan",
    )(us, us, p_f, p_b, m_mat, q_f, q_b, a_row, h0)


def _dft_tables(l, n1, hw):
    n2 = l // n1
    scale = 1.0 / math.sqrt(l * hw)
    if n1 == 1:
        t1 = None
    else:
        ang = 2.0 * np.pi * ((np.arange(n1)[:, None] * np.arange(n1)[None, :]) % n1) / n1
        c, s = np.cos(ang), np.sin(ang)
        t1 = jnp.asarray(np.block([[c, -s], [-s, -c]]), dtype=BF16)
    k1 = np.arange(n1)[:, None, None]
    k2 = np.arange(n2)[None, :, None]
    m = np.arange(n2)[None, None, :]
    ang = 2.0 * np.pi * ((m * (n1 * k2 + k1)) % l) / l
    sign = -1.0 if n1 == 1 else 1.0
    g = np.concatenate([np.cos(ang), sign * np.sin(ang)], axis=-1) * scale
    return t1, jnp.asarray(g, dtype=BF16)


def _dft1_kernel(t_ref, x_ref, z_ref):
    z_ref[...] = jnp.dot(t_ref[...], x_ref[...], preferred_element_type=F32).astype(z_ref.dtype)


def _dft1_call(t1, ab2d):
    r, n = ab2d.shape
    tn = min(4096, n)
    return pl.pallas_call(
        _dft1_kernel,
        grid=(n // tn,),
        in_specs=[pl.BlockSpec((r, r), lambda j: (0, 0)),
                  pl.BlockSpec((r, tn), lambda j: (0, j))],
        out_specs=pl.BlockSpec((r, tn), lambda j: (0, j)),
        out_shape=jax.ShapeDtypeStruct((r, n), BF16),
        compiler_params=_cparams(("arbitrary",)),
        name="dft_stage1",
    )(t1, ab2d)


def _dft2_kernel(bt, zr_ref, zi_ref, g_ref, w_ref, o_ref):
    fw = w_ref.shape[1]
    for b in range(bt):
        z = jnp.concatenate([zr_ref[b], zi_ref[b]], axis=0)
        x = jnp.dot(g_ref[b], z, preferred_element_type=F32)
        o_ref[:, b * fw:(b + 1) * fw] = jnp.dot(
            x.astype(BF16), w_ref[0], preferred_element_type=F32).astype(o_ref.dtype)


def _dft2_call(z3, g, fourier_w, layer):
    n1x2, n2, fw = z3.shape
    n1 = n1x2 // 2
    bt = min(8, n1)
    return pl.pallas_call(
        functools.partial(_dft2_kernel, bt),
        grid=(n1 // bt,),
        in_specs=[pl.BlockSpec((bt, n2, fw), lambda b: (b, 0, 0)),
                  pl.BlockSpec((bt, n2, fw), lambda b: (b + n1 // bt, 0, 0)),
                  pl.BlockSpec((bt, n2, 2 * n2), lambda b: (b, 0, 0)),
                  _layer_block(fourier_w, layer)],
        out_specs=pl.BlockSpec((n2, bt * fw), lambda b: (0, b)),
        out_shape=jax.ShapeDtypeStruct((n2, n1 * fw), BF16),
        compiler_params=_cparams(("arbitrary",)),
        name="dft_stage2",
    )(z3, z3, g, fourier_w)


def _fourier_apply(ab, tables, fourier_w, layer, n1):
    t1, g = tables
    _, l, fw = ab.shape
    n2 = l // n1
    if n1 == 1:
        z3 = ab
    else:
        z3 = _dft1_call(t1, ab.reshape(2 * n1, n2 * fw)).reshape(2 * n1, n2, fw)
    return _dft2_call(z3, g, fourier_w, layer).reshape(l, fw)


def _outproj_kernel(attn_ref, yf_ref, yb_ref, four_ref, x_ref, wo_ref, gw_ref, gb_ref, npost_ref,
                    gate_ref, npre_ref, sh_ref, sc_ref, xo_ref, h_ref):
    tm = x_ref.shape[0]
    n_split = 2 if tm % 32 == 0 else 1
    for k in range(n_split):
        rows = slice(k * (tm // n_split), (k + 1) * (tm // n_split))
        ys = yf_ref[rows, :] + yb_ref[rows, :]
        gl = 0.5 * ys * (1.0 + jnp.tanh(math.sqrt(2.0 / math.pi)
                                        * (ys + 0.044715 * (ys * ys * ys))))
        z = jnp.dot(gl.astype(BF16), gw_ref[0], preferred_element_type=F32) + gb_ref[...]
        ssm = (gl * jax.nn.sigmoid(z)).astype(BF16)
        cat = jnp.concatenate([attn_ref[rows, :], ssm, four_ref[rows, :]], axis=-1)
        y = jnp.dot(cat, wo_ref[0], preferred_element_type=F32)
        xn = x_ref[rows, :] + gate_ref[...] * _rms(y, npost_ref[...])
        xo_ref[rows, :] = xn
        h_ref[rows, :] = (_rms(xn, npre_ref[...]) * (1.0 + sc_ref[...])
                          + sh_ref[...]).astype(BF16)


def _outproj_call(attn, yf, yb, four, x, w_out, glu_w, layer, glu_b, npost, gate, npre, shift,
                  scale):
    l, d = x.shape
    tm = min(512, l)
    row = lambda i: (i, 0)
    fix = lambda i: (0, 0)
    vec = pl.BlockSpec((1, d), fix)
    return pl.pallas_call(
        _outproj_kernel,
        grid=(l // tm,),
        in_specs=[pl.BlockSpec((tm, attn.shape[1]), row),
                  pl.BlockSpec((tm, yf.shape[1]), row),
                  pl.BlockSpec((tm, yb.shape[1]), row),
                  pl.BlockSpec((tm, four.shape[1]), row),
                  pl.BlockSpec((tm, d), row),
                  _layer_block(w_out, layer),
                  _layer_block(glu_w, layer),
                  pl.BlockSpec((1, glu_w.shape[2]), fix),
                  vec, vec, vec, vec, vec],
        out_specs=[pl.BlockSpec((tm, d), row), pl.BlockSpec((tm, d), row)],
        out_shape=[jax.ShapeDtypeStruct((l, d), F32), jax.ShapeDtypeStruct((l, d), BF16)],
        compiler_params=_cparams(("arbitrary",)),
        name="out_proj",
    )(attn, yf, yb, four, x, w_out, glu_w, glu_b, npost, gate, npre, shift, scale)


def _ffn_kernel(nj1, nj2, th, tn, h_ref, x_ref, wg_ref, wu_ref, wd_ref, npost_ref, gate_ref, o_ref,
                act_ref, tmp_ref):
    j = pl.program_id(1)

    @pl.when(j < nj1)
    def _up():
        h = h_ref[...]
        a = jnp.dot(h, wg_ref[0], preferred_element_type=F32)
        u = jnp.dot(h, wu_ref[0], preferred_element_type=F32)
        tmp_ref[...] = (a * jax.nn.sigmoid(a) * u).astype(BF16)

    for c in range(nj1):
        @pl.when(j == c)
        def _place(c=c):
            act_ref[:, c * th:(c + 1) * th] = tmp_ref[...]

    for c in range(nj2):
        @pl.when(j == nj1 + c)
        def _down(c=c):
            o_ref[:, c * tn:(c + 1) * tn] = jnp.dot(act_ref[...], wd_ref[0],
                                                    preferred_element_type=F32)

    @pl.when(j == nj1 + nj2 - 1)
    def _finish():
        o_ref[...] = x_ref[...] + gate_ref[...] * _rms(o_ref[...], npost_ref[...])


def _ffn_call(h, x, w_gate, w_up, w_down, layer, npost, gate):
    l, d = x.shape
    fh = w_gate.shape[2]
    th, tn = FFN_TH, FFN_TN
    nj1 = fh // th
    nj2 = d // tn
    tm = min(FFN_TM, l)
    vec = pl.BlockSpec((1, d), lambda i, j: (0, 0))
    up = pl.BlockSpec((1, d, th), lambda i, j: (layer, 0, jnp.minimum(j, nj1 - 1)))
    once = pl.Buffered(1)
    return pl.pallas_call(
        functools.partial(_ffn_kernel, nj1, nj2, th, tn),
        grid=(l // tm, nj1 + nj2),
        in_specs=[pl.BlockSpec((tm, d), lambda i, j: (i, 0)),
                  pl.BlockSpec((tm, d), lambda i, j: (i, 0), pipeline_mode=once),
                  up, up,
                  pl.BlockSpec((1, fh, tn), lambda i, j: (layer, 0, jnp.maximum(j - nj1, 0))),
                  vec, vec],
        out_specs=pl.BlockSpec((tm, d), lambda i, j: (i, 0), pipeline_mode=once),
        out_shape=jax.ShapeDtypeStruct((l, d), F32),
        scratch_shapes=[pltpu.VMEM((tm, fh), BF16), pltpu.VMEM((tm, th), BF16)],
        compiler_params=_cparams(("arbitrary", "arbitrary")),
        name="ffn",
    )(h, x, w_gate, w_up, w_down, npost, gate)


def _rope_tables(l):
    t = np.arange(l)
    row = (t // GRID_W).astype(np.float32)
    col = (t % GRID_W).astype(np.float32)
    freqs = np.float32(ROPE_THETA) ** (-np.arange(ROPE_PAIRS, dtype=np.float32) / np.float32(ROPE_PAIRS))
    ang_r = (row[:, None] * freqs).astype(np.float32)
    ang_c = (col[:, None] * freqs).astype(np.float32)
    cos = np.concatenate([np.cos(ang_r)] * 2 + [np.cos(ang_c)] * 2, axis=-1)
    sin = np.concatenate([-np.sin(ang_r), np.sin(ang_r), -np.sin(ang_c), np.sin(ang_c)], axis=-1)
    return jnp.asarray(cos, dtype=F32), jnp.asarray(sin, dtype=F32)


def _channel_dft_table(fw):
    hw = FOURIER_HEAD_DIM
    ang = 2.0 * np.pi * ((np.arange(hw)[:, None] * np.arange(hw)[None, :]) % hw) / hw
    eye = np.eye(fw // hw)
    return jnp.asarray(np.concatenate([np.kron(eye, np.cos(ang)), np.kron(eye, np.sin(ang))], axis=1),
                       dtype=BF16)


def _dft_split(l):
    n1 = 1
    while n1 * n1 < l:
        n1 *= 2
    return n1 if (l >= 1024 and n1 * n1 == l) else 1


def kernel(x, c, ctx, c_ctx, ada_w, ada_b, norm_mix_pre, norm_mix_post, norm_ffn_pre, norm_ffn_post, w_in, q_norm, k_norm, ssm_lam_re, ssm_lam_im, ssm_log_dt, ssm_b_re, ssm_b_im, ssm_c_re, ssm_c_im, ssm_d, ssm_glu_w, ssm_glu_b, fourier_w, w_out, ffn_w_gate, ffn_w_up, ffn_w_down):
    depth = ada_w.shape[0]
    _, l, d = x.shape
    n_ctx = ctx.shape[1]
    fw = fourier_w.shape[1]
    sw = ssm_d.shape[1]

    cond = jnp.zeros((8, d), F32).at[0].set(c[0]).at[1].set(c_ctx)
    mod = _ada_call(cond, ada_w, ada_b)

    cos, sin = _rope_tables(l)
    zero_tab = jnp.zeros((n_ctx, HEAD_DIM), F32)
    cs = _channel_dft_table(fw)
    n1 = _dft_split(l)
    n1c = _dft_split(n_ctx)
    dft_x = _dft_tables(l, n1, FOURIER_HEAD_DIM)
    dft_c = _dft_tables(n_ctx, n1c, FOURIER_HEAD_DIM)
    h0 = jnp.zeros((sw // SSM_LANES, 1, 4 * (SSM_LANES // SSM_GROUP) * SSM_STATE), F32)

    w_in_b = w_in.astype(BF16)
    w_out_b = w_out.astype(BF16)
    glu_w_b = ssm_glu_w.astype(BF16)
    four_w_b = fourier_w.astype(BF16)
    wg_b = ffn_w_gate.astype(BF16)
    wu_b = ffn_w_up.astype(BF16)
    wd_b = ffn_w_down.astype(BF16)

    xs = x[0]
    xc = ctx[0]
    for layer in range(depth):
        need_ctx = layer < depth - 1
        vecs = lambda r: [mod[layer, r:r + 1, i * d:(i + 1) * d] for i in range(6)]
        sh_m, sc_m, g_m, sh_f, sc_f, g_f = vecs(0)
        shc_m, scc_m, gc_m, shc_f, scc_f, gc_f = vecs(1)
        row = lambda a: a[layer].reshape(1, -1)
        tables = _ssm_tables(ssm_lam_re[layer], ssm_lam_im[layer], ssm_log_dt[layer],
                             ssm_b_re[layer], ssm_b_im[layer], ssm_c_re[layer], ssm_c_im[layer],
                             ssm_d[layer])

        qc, kc, vtc, usc, abc = _inproj_call(
            xc, row(norm_mix_pre), shc_m, scc_m, w_in_b, layer, row(q_norm), row(k_norm),
            zero_tab, zero_tab, cs, False)
        yfc, ybc, hend_c = _ssm_apply(usc, tables, h0)

        q, k, vt, us, ab = _inproj_call(xs, row(norm_mix_pre), sh_m, sc_m, w_in_b, layer,
                                        row(q_norm), row(k_norm), cos, sin, cs, True)
        attn = _attn_dispatch(q, k, vt, kc, vtc, q_norm[layer], k_norm[layer])
        yf, yb, _ = _ssm_apply(us, tables, hend_c)
        four = _fourier_apply(ab, dft_x, four_w_b, layer, n1)
        xs, hs = _outproj_call(attn, yf, yb, four, xs, w_out_b, glu_w_b, layer, row(ssm_glu_b),
                               row(norm_mix_post), g_m, row(norm_ffn_pre), sh_f, sc_f)
        xs = _ffn_call(hs, xs, wg_b, wu_b, wd_b, layer, row(norm_ffn_post), g_f)

        if need_ctx:
            attn_c = _attn_call(qc, kc, vtc)
            four_c = _fourier_apply(abc, dft_c, four_w_b, layer, n1c)
            xc, hc = _outproj_call(attn_c, yfc, ybc, four_c, xc, w_out_b, glu_w_b, layer,
                                   row(ssm_glu_b), row(norm_mix_post), gc_m, row(norm_ffn_pre),
                                   shc_f, scc_f)
            xc = _ffn_call(hc, xc, wg_b, wu_b, wd_b, layer, row(norm_ffn_post), gc_f)
    return xs[None]
```

```python
import functools
import math

import numpy as np
import jax
import jax.numpy as jnp
from jax import lax
from jax.experimental import pallas as pl
from jax.experimental.pallas import tpu as pltpu

F32 = jnp.float32
BF16 = jnp.bfloat16

HEAD_DIM = 128
N_KV_HEADS = 2
Q_PER_KV = 4
GRID_W = 64
ROPE_THETA = 10000.0
ROPE_PAIRS = HEAD_DIM // 4
SSM_GROUP = 16
SSM_STATE = 64
FOURIER_HEAD_DIM = 128
NORM_EPS = 1e-6
SSM_CHUNK = 8
SSM_LANES = 128
LOG2E = 1.4426950408889634
ATTN_TQ = 512
ATTN_TK = 1280
Q_SCALE = HEAD_DIM ** -0.5 * LOG2E
NORM_SLACK = 1.01
ATTN_MAX_GAP = 100.0
ATTN_SUB = 256
FFN_TM = 1024
FFN_TH = 512
FFN_TN = 256
VT_PAD = 16
VT_ROWS = HEAD_DIM + VT_PAD

VMEM_LIMIT_MB = 56


def _cparams(semantics, vmem_mb=VMEM_LIMIT_MB):
    return pltpu.CompilerParams(dimension_semantics=semantics,
                                vmem_limit_bytes=vmem_mb * 2 ** 20)


def _rms(x, gain):
    return x * lax.rsqrt(jnp.mean(x * x, axis=-1, keepdims=True) + NORM_EPS) * gain


def _ada_kernel(s_ref, w_ref, b_ref, o_ref):
    s = s_ref[...]
    act = (s * jax.nn.sigmoid(s)).astype(BF16)
    o_ref[0] = jnp.dot(act, w_ref[0].astype(BF16), preferred_element_type=F32) + b_ref[0]


def _ada_call(cond, ada_w, ada_b):
    depth, d, n = ada_w.shape
    tn = 1024
    return pl.pallas_call(
        _ada_kernel,
        grid=(depth, n // tn),
        in_specs=[pl.BlockSpec((8, d), lambda l, j: (0, 0)),
                  pl.BlockSpec((1, d, tn), lambda l, j: (l, 0, j)),
                  pl.BlockSpec((1, 1, tn), lambda l, j: (l, 0, j))],
        out_specs=pl.BlockSpec((1, 8, tn), lambda l, j: (l, 0, j)),
        out_shape=jax.ShapeDtypeStruct((depth, 8, n), F32),
        compiler_params=_cparams(("arbitrary", "arbitrary")),
        name="ada_ln",
    )(cond, ada_w, ada_b.reshape(depth, 1, n))


def _inproj_kernel(rope, x_ref, g_ref, sh_ref, sc_ref, w_ref, qn_ref, kn_ref, cos_ref, sin_ref,
                   cs_ref, q_ref, k_ref, vt_ref, us_ref, ab_ref):
    x = x_ref[...]
    h = (_rms(x, g_ref[...]) * (1.0 + sc_ref[...]) + sh_ref[...]).astype(BF16)
    n_q = q_ref.shape[1] // HEAD_DIM
    n_k = k_ref.shape[1] // HEAD_DIM
    q_end = n_q * HEAD_DIM
    k_end = q_end + n_k * HEAD_DIM
    v_end = k_end + n_k * HEAD_DIM
    s_end = v_end + us_ref.shape[1]
    f_end = s_end + ab_ref.shape[2]

    if rope:
        cos = cos_ref[...]
        sin = sin_ref[...]
        lane = lax.broadcasted_iota(jnp.int32, cos.shape, 1)
        low = (lane % 64) < 32

    def head(t, gain, scale):
        t = _rms(t, gain)
        if rope:
            sw = jnp.where(low, pltpu.roll(t, HEAD_DIM - 32, 1), pltpu.roll(t, 32, 1))
            t = t * cos + sw * sin
        if scale != 1.0:
            t = t * scale
        return t.astype(BF16)

    qk = jnp.dot(h, w_ref[0, :, 0:k_end], preferred_element_type=F32)
    for i in range(n_q):
        q_ref[:, i * HEAD_DIM:(i + 1) * HEAD_DIM] = head(
            qk[:, i * HEAD_DIM:(i + 1) * HEAD_DIM], qn_ref[...], Q_SCALE)
    for i in range(n_k):
        k_ref[:, i * HEAD_DIM:(i + 1) * HEAD_DIM] = head(
            qk[:, q_end + i * HEAD_DIM:q_end + (i + 1) * HEAD_DIM], kn_ref[...], 1.0)
    rest = jnp.dot(h, w_ref[0, :, k_end:f_end], preferred_element_type=F32)
    ones_rows = (lax.broadcasted_iota(jnp.int32, (VT_PAD, ATTN_SUB), 0) == 0).astype(BF16)
    for i in range(n_k):
        vt = rest[:, i * HEAD_DIM:(i + 1) * HEAD_DIM].T.astype(BF16)
        for b in range(vt_ref.shape[1]):
            vt_ref[i, b, 0:HEAD_DIM, :] = vt[:, b * ATTN_SUB:(b + 1) * ATTN_SUB]
            vt_ref[i, b, HEAD_DIM:VT_ROWS, :] = ones_rows
    us_ref[...] = rest[:, v_end - k_end:s_end - k_end]
    uf = rest[:, s_end - k_end:f_end - k_end].astype(BF16)
    ab = jnp.dot(uf, cs_ref[...], preferred_element_type=F32)
    fw = ab_ref.shape[2]
    ab_ref[0] = ab[:, 0:fw].astype(BF16)
    ab_ref[1] = ab[:, fw:2 * fw].astype(BF16)


def _layer_block(w, layer):
    return pl.BlockSpec((1,) + w.shape[1:], lambda *_: (layer, 0, 0))


def _inproj_call(x, gain, shift, scale, w_in, layer, qn, kn, cos, sin, cs, rope):
    l, d = x.shape
    tm = min(512, l)
    qw = Q_PER_KV * N_KV_HEADS * HEAD_DIM
    kw = N_KV_HEADS * HEAD_DIM
    fw = cs.shape[0]
    sw = w_in.shape[2] - qw - 2 * kw - fw
    row = lambda i: (i, 0)
    fix = lambda i: (0, 0)
    return pl.pallas_call(
        functools.partial(_inproj_kernel, rope),
        grid=(l // tm,),
        in_specs=[pl.BlockSpec((tm, d), row),
                  pl.BlockSpec((1, d), fix), pl.BlockSpec((1, d), fix), pl.BlockSpec((1, d), fix),
                  _layer_block(w_in, layer),
                  pl.BlockSpec((1, HEAD_DIM), fix), pl.BlockSpec((1, HEAD_DIM), fix),
                  pl.BlockSpec((tm, HEAD_DIM), row), pl.BlockSpec((tm, HEAD_DIM), row),
                  pl.BlockSpec(cs.shape, fix)],
        out_specs=[pl.BlockSpec((tm, qw), row),
                   pl.BlockSpec((tm, kw), row),
                   pl.BlockSpec((N_KV_HEADS, tm // ATTN_SUB, VT_ROWS, ATTN_SUB),
                                lambda i: (0, i, 0, 0)),
                   pl.BlockSpec((tm, sw), row),
                   pl.BlockSpec((2, tm, fw), lambda i: (0, i, 0))],
        out_shape=[jax.ShapeDtypeStruct((l, qw), BF16),
                   jax.ShapeDtypeStruct((l, kw), BF16),
                   jax.ShapeDtypeStruct((N_KV_HEADS, l // ATTN_SUB, VT_ROWS, ATTN_SUB), BF16),
                   jax.ShapeDtypeStruct((l, sw), F32),
                   jax.ShapeDtypeStruct((2, l, fw), BF16)],
        compiler_params=_cparams(("arbitrary",)),
        name="in_proj",
    )(x, gain, shift, scale, w_in, qn, kn, cos, sin, cs)


def _attn_kernel(nkb, tq, q_ref, k_ref, vt_ref, o_ref, qs_ref, s0_ref, s1_ref, bm0_ref, bm1_ref,
                 m_ref, acc_ref):
    j = pl.program_id(2)
    s_refs = (s0_ref, s1_ref)
    bm_refs = (bm0_ref, bm1_ref)

    tk = k_ref.shape[0]
    sub = vt_ref.shape[3]

    def step(score_slot, absorb_slot):
        if absorb_slot is not None:
            m_prev = m_ref[...]
            m_new = jnp.maximum(m_prev, bm_refs[absorb_slot][...])
            alpha = jnp.exp2(m_prev - m_new)
            pv = jnp.zeros(acc_ref.shape, F32)
        bmax = None
        for c in range(tk // sub):
            rows = slice(c * sub, (c + 1) * sub)
            if score_slot is not None:
                s = lax.dot_general(k_ref[rows, :], qs_ref[...], (((1,), (1,)), ((), ())),
                                    preferred_element_type=F32)
                s_refs[score_slot][rows, :] = s
                cmax = jnp.max(s, axis=0, keepdims=True)
                bmax = cmax if bmax is None else jnp.maximum(bmax, cmax)
            if absorb_slot is not None:
                p = jnp.exp2((s_refs[absorb_slot][rows, :] - m_new).astype(BF16))
                pv = pv + jnp.dot(vt_ref[0, c], p, preferred_element_type=F32)
        if score_slot is not None:
            bm_refs[score_slot][...] = bmax
        if absorb_slot is not None:
            acc_ref[...] = alpha * acc_ref[...] + pv
            m_ref[...] = m_new

    @pl.when(j == 0)
    def _first():
        for h in range(Q_PER_KV):
            qs_ref[h * tq:(h + 1) * tq, :] = q_ref[:, h * HEAD_DIM:(h + 1) * HEAD_DIM]
        m_ref[...] = jnp.full(m_ref.shape, -jnp.inf, F32)
        acc_ref[...] = jnp.zeros(acc_ref.shape, F32)
        step(0, None)

    for parity in (0, 1):
        @pl.when((j > 0) & (j < nkb) & (j % 2 == parity))
        def _mid(parity=parity):
            step(parity, 1 - parity)

    @pl.when(j == nkb)
    def _last():
        step(None, (nkb - 1) % 2)
        o = acc_ref[0:HEAD_DIM, :] / acc_ref[HEAD_DIM:HEAD_DIM + 1, :]
        for h in range(Q_PER_KV):
            o_ref[:, h * HEAD_DIM:(h + 1) * HEAD_DIM] = o[:, h * tq:(h + 1) * tq].T.astype(BF16)


def _pick_block(total, unit, cap):
    best = unit
    for b in range(unit, cap + 1, unit):
        if total % b == 0:
            best = b
    return best


def _attn_bounded_kernel(unroll, tq, kn_ref, q_ref, k_ref, vt_ref, kc_ref, vtc_ref, o_ref, qs_ref,
                         acc_ref):
    sub = vt_ref.shape[3]
    for h in range(Q_PER_KV):
        qs_ref[h * tq:(h + 1) * tq, :] = q_ref[:, h * HEAD_DIM:(h + 1) * HEAD_DIM]
    qf = qs_ref[...].astype(F32)
    norm2 = lax.dot_general(jnp.ones((8, HEAD_DIM), F32), qf * qf, (((1,), (1,)), ((), ())),
                            preferred_element_type=F32)[0:1, :]
    shift = jnp.sqrt(norm2) * kn_ref[pl.program_id(0)]

    def weighted_values(keys, vt):
        s = lax.dot_general(keys, qs_ref[...], (((1,), (1,)), ((), ())),
                            preferred_element_type=F32)
        return jnp.dot(vt, jnp.exp2(s - shift).astype(BF16), preferred_element_type=F32)

    acc = None
    for c in range(vtc_ref.shape[1]):
        d = weighted_values(kc_ref[c * sub:(c + 1) * sub, :], vtc_ref[0, c])
        acc = d if acc is None else acc + d
    acc_ref[...] = acc

    def body(t, carry):
        pv = None
        for u in range(unroll):
            c = t * unroll + u
            start = pl.multiple_of(c * sub, sub)
            d = weighted_values(k_ref[pl.ds(start, sub), :], vt_ref[0, c])
            pv = d if pv is None else pv + d
        acc_ref[...] += pv
        return carry

    lax.fori_loop(0, vt_ref.shape[1] // unroll, body, 0)
    o = acc_ref[0:HEAD_DIM, :] / acc_ref[HEAD_DIM:HEAD_DIM + 1, :]
    for h in range(Q_PER_KV):
        o_ref[:, h * HEAD_DIM:(h + 1) * HEAD_DIM] = o[:, h * tq:(h + 1) * tq].T.astype(BF16)


def _attn_bounded_call(q, k, vt, kc, vtc, k_norm_max):
    l = q.shape[0]
    nblk = vt.shape[1]
    tq = min(ATTN_TQ, l)
    gw = Q_PER_KV * HEAD_DIM
    n = Q_PER_KV * tq
    unroll = max(u for u in (8, 5, 4, 3, 2, 1) if nblk % u == 0)
    head_rows = lambda a: pl.BlockSpec((a.shape[0], HEAD_DIM), lambda g, i: (0, g))
    head_blocks = lambda a: pl.BlockSpec((1,) + a.shape[1:], lambda g, i: (g, 0, 0, 0))
    return pl.pallas_call(
        functools.partial(_attn_bounded_kernel, unroll, tq),
        grid=(N_KV_HEADS, l // tq),
        in_specs=[pl.BlockSpec(memory_space=pltpu.SMEM),
                  pl.BlockSpec((tq, gw), lambda g, i: (i, g)),
                  head_rows(k), head_blocks(vt), head_rows(kc), head_blocks(vtc)],
        out_specs=pl.BlockSpec((tq, gw), lambda g, i: (i, g)),
        out_shape=jax.ShapeDtypeStruct(q.shape, BF16),
        scratch_shapes=[pltpu.VMEM((n, HEAD_DIM), BF16), pltpu.VMEM((VT_ROWS, n), F32)],
        compiler_params=_cparams(("arbitrary", "arbitrary")),
        name="attention_bounded",
    )(k_norm_max, q, k, vt, kc, vtc)


def _attn_dispatch(q, k, vt, kc, vtc, q_gain, k_gain):
    bound = lambda gain: NORM_SLACK * math.sqrt(HEAD_DIM) * jnp.max(jnp.abs(gain.astype(F32)))
    qn = Q_SCALE * bound(q_gain)
    kn = jnp.full((N_KV_HEADS,), bound(k_gain), F32)
    safe = 2.0 * qn * jnp.max(kn) <= ATTN_MAX_GAP
    return lax.cond(safe,
                    lambda: _attn_bounded_call(q, k, vt, kc, vtc, kn),
                    lambda: _attn_call(q, jnp.concatenate([k, kc], axis=0),
                                       jnp.concatenate([vt, vtc], axis=1)))


def _attn_call(q, k_all, vt_all):
    l = q.shape[0]
    s_len = k_all.shape[0]
    sub = vt_all.shape[3]
    tq = min(ATTN_TQ, l)
    tk = _pick_block(s_len, sub, ATTN_TK)
    nkb = s_len // tk
    gw = Q_PER_KV * HEAD_DIM
    n = Q_PER_KV * tq
    return pl.pallas_call(
        functools.partial(_attn_kernel, nkb, tq),
        grid=(N_KV_HEADS, l // tq, nkb + 1),
        in_specs=[pl.BlockSpec((tq, gw), lambda g, i, j: (i, g)),
                  pl.BlockSpec((tk, HEAD_DIM), lambda g, i, j: (jnp.minimum(j, nkb - 1), g)),
                  pl.BlockSpec((1, tk // sub, VT_ROWS, sub),
                               lambda g, i, j: (g, jnp.maximum(j - 1, 0), 0, 0))],
        out_specs=pl.BlockSpec((tq, gw), lambda g, i, j: (i, g)),
        out_shape=jax.ShapeDtypeStruct(q.shape, BF16),
        scratch_shapes=[pltpu.VMEM((n, HEAD_DIM), BF16),
                        pltpu.VMEM((tk, n), F32),
                        pltpu.VMEM((tk, n), F32),
                        pltpu.VMEM((1, n), F32),
                        pltpu.VMEM((1, n), F32),
                        pltpu.VMEM((1, n), F32),
                        pltpu.VMEM((VT_ROWS, n), F32)],
        compiler_params=_cparams(("arbitrary", "arbitrary", "arbitrary")),
        name="attention",
    )(q, k_all, vt_all)


def _ssm_tables(lam_re, lam_im, log_dt, b_re, b_im, c_re, c_im, d):
    t = SSM_CHUNK
    g, p = lam_re.shape[1:]
    hdim = b_re.shape[-1]
    lr = lam_re.astype(F32)
    li = lam_im.astype(F32)
    dt = jnp.exp(log_dt.astype(F32))[..., None]
    mag = jnp.exp(lr * dt)
    ar = mag * jnp.cos(li * dt)
    ai = mag * jnp.sin(li * dt)
    den = lr * lr + li * li
    cr = ((ar - 1.0) * lr + ai * li) / den
    ci = (ai * lr - (ar - 1.0) * li) / den
    br = b_re.astype(F32)
    bi = b_im.astype(F32)
    bbr = cr[..., None] * br - ci[..., None] * bi
    bbi = cr[..., None] * bi + ci[..., None] * br
    n = jnp.arange(t + 1, dtype=F32)[:, None, None, None]
    pmag = jnp.exp(n * (lr * dt)[None])
    pr = pmag * jnp.cos(n * (li * dt)[None])
    pi = pmag * jnp.sin(n * (li * dt)[None])
    cre = c_re.astype(F32)
    cim = c_im.astype(F32)

    gs = SSM_LANES // hdim
    n_sg = g // gs

    def selector(r, inner):
        e = np.zeros((gs, r, gs, inner, r, inner), np.float32)
        for grp in range(gs):
            e[grp, :, grp] = np.eye(r * inner, dtype=np.float32).reshape(r, inner, r, inner)
        return jnp.asarray(e.reshape(gs, r * gs * inner, r * inner), dtype=BF16)

    def expand(small):
        _, r1, i1, r2, i2 = small.shape
        blocks = small.reshape(n_sg, gs, r1 * i1, r2 * i2).astype(BF16)
        wide = jnp.einsum('sgkl,gcl->sgkc', blocks, selector(r2, i2),
                          preferred_element_type=F32).astype(BF16)
        wide = wide.reshape(n_sg, gs, r1, i1, r2 * gs * i2)
        return jnp.transpose(wide, (0, 2, 1, 3, 4)).reshape(n_sg, r1 * gs * i1, r2 * gs * i2)

    def summ(direction, powers):
        er = pr[powers, direction][..., None] * bbr[direction][None] \
            - pi[powers, direction][..., None] * bbi[direction][None]
        ei = pr[powers, direction][..., None] * bbi[direction][None] \
            + pi[powers, direction][..., None] * bbr[direction][None]
        tr = lambda e: jnp.transpose(e, (1, 0, 3, 2))
        return expand(jnp.stack([tr(er), tr(ei)], axis=3))
    steps = jnp.arange(t)
    p_f = summ(0, t - 1 - steps)
    p_b = summ(1, steps)

    row = lambda v: v.reshape(n_sg, gs * p)
    a_row = jnp.concatenate([row(pr[t, 0]), row(pr[t, 1]), row(pi[t, 0]), row(pi[t, 1])],
                            axis=-1)[:, None, :]

    def kern(direction):
        er = pr[:t, direction][..., None] * bbr[direction][None] \
            - pi[:t, direction][..., None] * bbi[direction][None]
        ei = pr[:t, direction][..., None] * bbi[direction][None] \
            + pi[:t, direction][..., None] * bbr[direction][None]
        return (jnp.einsum('gop,ngpi->ngoi', cre[direction], er)
                - jnp.einsum('gop,ngpi->ngoi', cim[direction], ei))
    kf = kern(0)
    kb = kern(1)
    s_idx = steps[:, None]
    t_idx = steps[None, :]
    lag_f = jnp.clip(t_idx - s_idx, 0, t - 1)
    lag_b = jnp.clip(s_idx - t_idx, 0, t - 1)
    mf = jnp.where((t_idx >= s_idx)[:, :, None, None, None], kf[lag_f], 0.0)
    mb = jnp.where((s_idx >= t_idx)[:, :, None, None, None], kb[lag_b], 0.0)
    dd = d.astype(F32).reshape(g, hdim)
    eye_t = jnp.eye(t, dtype=F32)[:, :, None, None, None]
    eye_c = jnp.eye(hdim, dtype=F32)[None, None, None]
    m5 = mf + mb + eye_t * eye_c * dd[None, None, :, :, None]
    m_mat = expand(jnp.transpose(m5, (2, 0, 4, 1, 3)))

    def state_out(direction, powers):
        prn = pr[powers, direction]
        pin = pi[powers, direction]
        wr = cre[direction][None] * prn[:, :, None, :] - cim[direction][None] * pin[:, :, None, :]
        wi = -cre[direction][None] * pin[:, :, None, :] - cim[direction][None] * prn[:, :, None, :]
        tr = lambda w: jnp.transpose(w, (1, 3, 0, 2))
        return expand(jnp.stack([tr(wr), tr(wi)], axis=1))
    q_f = state_out(0, steps + 1)
    q_b = state_out(1, t - steps)
    return p_f, p_b, m_mat, q_f, q_b, a_row


def _ssm_kernel(tbk, usf_ref, usb_ref, pf_ref, pb_ref, m_ref, qf_ref, qb_ref, a_ref, h0_ref,
                yf_ref, yb_ref, hend_ref, xf_ref, xb_ref, sf_ref, sb_ref, hf_ref, hb_ref, h_ref):
    t = SSM_CHUNK
    lanes = SSM_LANES

    @pl.when(pl.program_id(1) == 0)
    def _init():
        h_ref[...] = h0_ref[0]

    for tt in range(t):
        cols = slice(tt * lanes, (tt + 1) * lanes)
        xf_ref[:, cols] = usf_ref[pl.ds(tt, tbk, stride=t), :].astype(BF16)
        xb_ref[:, cols] = usb_ref[pl.ds(tt, tbk, stride=t), :].astype(BF16)
    sf_ref[...] = jnp.dot(xf_ref[...], pf_ref[0, 0], preferred_element_type=F32)
    sb_ref[...] = jnp.dot(xb_ref[...], pb_ref[0, 0], preferred_element_type=F32)

    w = a_ref.shape[3] // 4
    a = a_ref[0, 0]
    arf, arb, aif, aib = (a[:, i * w:(i + 1) * w] for i in range(4))
    h = h_ref[...]

    def body(k, carry):
        hrf, hrb, hif, hib = carry
        kb = tbk - 1 - k
        hf_ref[pl.ds(k, 1), 0:w] = hrf
        hf_ref[pl.ds(k, 1), w:2 * w] = hif
        hb_ref[pl.ds(kb, 1), 0:w] = hrb
        hb_ref[pl.ds(kb, 1), w:2 * w] = hib
        sf = sf_ref[pl.ds(k, 1), :]
        sb = sb_ref[pl.ds(kb, 1), :]
        return (arf * hrf - aif * hif + sf[:, 0:w], arb * hrb - aib * hib + sb[:, 0:w],
                arf * hif + aif * hrf + sf[:, w:2 * w], arb * hib + aib * hrb + sb[:, w:2 * w])

    carry = lax.fori_loop(0, tbk, body, tuple(h[:, i * w:(i + 1) * w] for i in range(4)),
                          unroll=8)
    h = jnp.concatenate(carry, axis=1)
    h_ref[...] = h
    hend_ref[0] = h

    yf = (jnp.dot(xf_ref[...], m_ref[0, 0], preferred_element_type=F32)
          + jnp.dot(hf_ref[...].astype(BF16), qf_ref[0, 0], preferred_element_type=F32))
    yb = jnp.dot(hb_ref[...].astype(BF16), qb_ref[0, 0], preferred_element_type=F32)
    for tt in range(t):
        cols = slice(tt * lanes, (tt + 1) * lanes)
        yf_ref[pl.ds(tt, tbk, stride=t), :] = yf[:, cols]
        yb_ref[pl.ds(tt, tbk, stride=t), :] = yb[:, cols]


def _ssm_apply(us, tables, layer, h0):
    p_f, p_b, m_mat, q_f, q_b, a_row = tables
    l, w = us.shape
    n_sg = w // SSM_LANES
    n = l // SSM_CHUNK
    tbk = min(256, n)
    nb = n // tbk
    rows = tbk * SSM_CHUNK
    kx = SSM_CHUNK * SSM_LANES
    sw = p_f.shape[3]
    hw = a_row.shape[3]
    per_sg = lambda a: pl.BlockSpec((1,) + a.shape[1:], lambda s, i: (s, 0, 0))
    table = lambda a: pl.BlockSpec((1, 1) + a.shape[2:], lambda s, i: (layer, s, 0, 0))
    fwd = pl.BlockSpec((rows, SSM_LANES), lambda s, i: (i, s))
    bwd = pl.BlockSpec((rows, SSM_LANES), lambda s, i: (nb - 1 - i, s))
    return pl.pallas_call(
        functools.partial(_ssm_kernel, tbk),
        grid=(n_sg, nb),
        in_specs=[fwd, bwd, table(p_f), table(p_b), table(m_mat), table(q_f), table(q_b),
                  table(a_row), per_sg(h0)],
        out_specs=[fwd, bwd, per_sg(h0)],
        out_shape=[jax.ShapeDtypeStruct((l, w), F32), jax.ShapeDtypeStruct((l, w), F32),
                   jax.ShapeDtypeStruct(h0.shape, F32)],
        scratch_shapes=[pltpu.VMEM((tbk, kx), BF16), pltpu.VMEM((tbk, kx), BF16),
                        pltpu.VMEM((tbk, sw), F32), pltpu.VMEM((tbk, sw), F32),
                        pltpu.VMEM((tbk, sw), F32), pltpu.VMEM((tbk, sw), F32),
                        pltpu.VMEM((1, hw), F32)],
        compiler_params=_cparams(("arbitrary", "arbitrary")),
        name="ssm_scan",
    )(us, us, p_f, p_b, m_mat, q_f, q_b, a_row, h0)


def _dft_tables(l, n1, hw):
    n2 = l // n1
    scale = 1.0 / math.sqrt(l * hw)
    if n1 == 1:
        t1 = None
    else:
        ang = 2.0 * np.pi * ((np.arange(n1)[:, None] * np.arange(n1)[None, :]) % n1) / n1
        c, s = np.cos(ang), np.sin(ang)
        t1 = jnp.asarray(np.block([[c, -s], [-s, -c]]), dtype=BF16)
    k1 = np.arange(n1)[:, None, None]
    k2 = np.arange(n2)[None, :, None]
    m = np.arange(n2)[None, None, :]
    ang = 2.0 * np.pi * ((m * (n1 * k2 + k1)) % l) / l
    sign = -1.0 if n1 == 1 else 1.0
    g = np.concatenate([np.cos(ang), sign * np.sin(ang)], axis=-1) * scale
    return t1, jnp.asarray(g, dtype=BF16)


def _dft1_kernel(t_ref, x_ref, z_ref):
    z_ref[...] = jnp.dot(t_ref[...], x_ref[...], preferred_element_type=F32).astype(z_ref.dtype)


def _dft1_call(t1, ab2d):
    r, n = ab2d.shape
    tn = min(4096, n)
    return pl.pallas_call(
        _dft1_kernel,
        grid=(n // tn,),
        in_specs=[pl.BlockSpec((r, r), lambda j: (0, 0)),
                  pl.BlockSpec((r, tn), lambda j: (0, j))],
        out_specs=pl.BlockSpec((r, tn), lambda j: (0, j)),
        out_shape=jax.ShapeDtypeStruct((r, n), BF16),
        compiler_params=_cparams(("arbitrary",)),
        name="dft_stage1",
    )(t1, ab2d)


def _dft2_kernel(bt, zr_ref, zi_ref, g_ref, w_ref, o_ref):
    fw = w_ref.shape[1]
    for b in range(bt):
        z = jnp.concatenate([zr_ref[b], zi_ref[b]], axis=0)
        x = jnp.dot(g_ref[b], z, preferred_element_type=F32)
        o_ref[:, b * fw:(b + 1) * fw] = jnp.dot(
            x.astype(BF16), w_ref[0], preferred_element_type=F32).astype(o_ref.dtype)


def _dft2_call(z3, g, fourier_w, layer):
    n1x2, n2, fw = z3.shape
    n1 = n1x2 // 2
    bt = min(8, n1)
    return pl.pallas_call(
        functools.partial(_dft2_kernel, bt),
        grid=(n1 // bt,),
        in_specs=[pl.BlockSpec((bt, n2, fw), lambda b: (b, 0, 0)),
                  pl.BlockSpec((bt, n2, fw), lambda b: (b + n1 // bt, 0, 0)),
                  pl.BlockSpec((bt, n2, 2 * n2), lambda b: (b, 0, 0)),
                  _layer_block(fourier_w, layer)],
        out_specs=pl.BlockSpec((n2, bt * fw), lambda b: (0, b)),
        out_shape=jax.ShapeDtypeStruct((n2, n1 * fw), BF16),
        compiler_params=_cparams(("arbitrary",)),
        name="dft_stage2",
    )(z3, z3, g, fourier_w)


def _fourier_apply(ab, tables, fourier_w, layer, n1):
    t1, g = tables
    _, l, fw = ab.shape
    n2 = l // n1
    if n1 == 1:
        z3 = ab
    else:
        z3 = _dft1_call(t1, ab.reshape(2 * n1, n2 * fw)).reshape(2 * n1, n2, fw)
    return _dft2_call(z3, g, fourier_w, layer).reshape(l, fw)


def _outproj_kernel(attn_ref, yf_ref, yb_ref, four_ref, x_ref, wo_ref, gw_ref, gb_ref, npost_ref,
                    gate_ref, npre_ref, sh_ref, sc_ref, xo_ref, h_ref):
    tm = x_ref.shape[0]
    n_split = 2 if tm % 32 == 0 else 1
    for k in range(n_split):
        rows = slice(k * (tm // n_split), (k + 1) * (tm // n_split))
        ys = yf_ref[rows, :] + yb_ref[rows, :]
        gl = 0.5 * ys * (1.0 + jnp.tanh(math.sqrt(2.0 / math.pi)
                                        * (ys + 0.044715 * (ys * ys * ys))))
        z = jnp.dot(gl.astype(BF16), gw_ref[0], preferred_element_type=F32) + gb_ref[...]
        ssm = (gl * jax.nn.sigmoid(z)).astype(BF16)
        cat = jnp.concatenate([attn_ref[rows, :], ssm, four_ref[rows, :]], axis=-1)
        y = jnp.dot(cat, wo_ref[0], preferred_element_type=F32)
        xn = x_ref[rows, :] + gate_ref[...] * _rms(y, npost_ref[...])
        xo_ref[rows, :] = xn
        h_ref[rows, :] = (_rms(xn, npre_ref[...]) * (1.0 + sc_ref[...])
                          + sh_ref[...]).astype(BF16)


def _outproj_call(attn, yf, yb, four, x, w_out, glu_w, layer, glu_b, npost, gate, npre, shift,
                  scale):
    l, d = x.shape
    tm = min(512, l)
    row = lambda i: (i, 0)
    fix = lambda i: (0, 0)
    vec = pl.BlockSpec((1, d), fix)
    return pl.pallas_call(
        _outproj_kernel,
        grid=(l // tm,),
        in_specs=[pl.BlockSpec((tm, attn.shape[1]), row),
                  pl.BlockSpec((tm, yf.shape[1]), row),
                  pl.BlockSpec((tm, yb.shape[1]), row),
                  pl.BlockSpec((tm, four.shape[1]), row),
                  pl.BlockSpec((tm, d), row),
                  _layer_block(w_out, layer),
                  _layer_block(glu_w, layer),
                  pl.BlockSpec((1, glu_w.shape[2]), fix),
                  vec, vec, vec, vec, vec],
        out_specs=[pl.BlockSpec((tm, d), row), pl.BlockSpec((tm, d), row)],
        out_shape=[jax.ShapeDtypeStruct((l, d), F32), jax.ShapeDtypeStruct((l, d), BF16)],
        compiler_params=_cparams(("arbitrary",)),
        name="out_proj",
    )(attn, yf, yb, four, x, w_out, glu_w, glu_b, npost, gate, npre, shift, scale)


def _ffn_kernel(nj1, nj2, th, tn, h_ref, x_ref, wg_ref, wu_ref, wd_ref, npost_ref, gate_ref, o_ref,
                act_ref, tmp_ref):
    j = pl.program_id(1)

    @pl.when(j < nj1)
    def _up():
        h = h_ref[...]
        a = jnp.dot(h, wg_ref[0], preferred_element_type=F32)
        u = jnp.dot(h, wu_ref[0], preferred_element_type=F32)
        tmp_ref[...] = (a * jax.nn.sigmoid(a) * u).astype(BF16)

    for c in range(nj1):
        @pl.when(j == c)
        def _place(c=c):
            act_ref[:, c * th:(c + 1) * th] = tmp_ref[...]

    for c in range(nj2):
        @pl.when(j == nj1 + c)
        def _down(c=c):
            o_ref[:, c * tn:(c + 1) * tn] = jnp.dot(act_ref[...], wd_ref[0],
                                                    preferred_element_type=F32)

    @pl.when(j == nj1 + nj2 - 1)
    def _finish():
        o_ref[...] = x_ref[...] + gate_ref[...] * _rms(o_ref[...], npost_ref[...])


def _ffn_call(h, x, w_gate, w_up, w_down, layer, npost, gate):
    l, d = x.shape
    fh = w_gate.shape[2]
    th, tn = FFN_TH, FFN_TN
    nj1 = fh // th
    nj2 = d // tn
    tm = min(FFN_TM, l)
    vec = pl.BlockSpec((1, d), lambda i, j: (0, 0))
    up = pl.BlockSpec((1, d, th), lambda i, j: (layer, 0, jnp.minimum(j, nj1 - 1)))
    once = pl.Buffered(1)
    return pl.pallas_call(
        functools.partial(_ffn_kernel, nj1, nj2, th, tn),
        grid=(l // tm, nj1 + nj2),
        in_specs=[pl.BlockSpec((tm, d), lambda i, j: (i, 0)),
                  pl.BlockSpec((tm, d), lambda i, j: (i, 0), pipeline_mode=once),
                  up, up,
                  pl.BlockSpec((1, fh, tn), lambda i, j: (layer, 0, jnp.maximum(j - nj1, 0))),
                  vec, vec],
        out_specs=pl.BlockSpec((tm, d), lambda i, j: (i, 0), pipeline_mode=once),
        out_shape=jax.ShapeDtypeStruct((l, d), F32),
        scratch_shapes=[pltpu.VMEM((tm, fh), BF16), pltpu.VMEM((tm, th), BF16)],
        compiler_params=_cparams(("arbitrary", "arbitrary")),
        name="ffn",
    )(h, x, w_gate, w_up, w_down, npost, gate)


def _rope_tables(l):
    t = np.arange(l)
    row = (t // GRID_W).astype(np.float32)
    col = (t % GRID_W).astype(np.float32)
    freqs = np.float32(ROPE_THETA) ** (-np.arange(ROPE_PAIRS, dtype=np.float32) / np.float32(ROPE_PAIRS))
    ang_r = (row[:, None] * freqs).astype(np.float32)
    ang_c = (col[:, None] * freqs).astype(np.float32)
    cos = np.concatenate([np.cos(ang_r)] * 2 + [np.cos(ang_c)] * 2, axis=-1)
    sin = np.concatenate([-np.sin(ang_r), np.sin(ang_r), -np.sin(ang_c), np.sin(ang_c)], axis=-1)
    return jnp.asarray(cos, dtype=F32), jnp.asarray(sin, dtype=F32)


def _channel_dft_table(fw):
    hw = FOURIER_HEAD_DIM
    ang = 2.0 * np.pi * ((np.arange(hw)[:, None] * np.arange(hw)[None, :]) % hw) / hw
    eye = np.eye(fw // hw)
    return jnp.asarray(np.concatenate([np.kron(eye, np.cos(ang)), np.kron(eye, np.sin(ang))], axis=1),
                       dtype=BF16)


def _dft_split(l):
    n1 = 1
    while n1 * n1 < l:
        n1 *= 2
    return n1 if (l >= 1024 and n1 * n1 == l) else 1


def kernel(x, c, ctx, c_ctx, ada_w, ada_b, norm_mix_pre, norm_mix_post, norm_ffn_pre, norm_ffn_post, w_in, q_norm, k_norm, ssm_lam_re, ssm_lam_im, ssm_log_dt, ssm_b_re, ssm_b_im, ssm_c_re, ssm_c_im, ssm_d, ssm_glu_w, ssm_glu_b, fourier_w, w_out, ffn_w_gate, ffn_w_up, ffn_w_down):
    depth = ada_w.shape[0]
    _, l, d = x.shape
    n_ctx = ctx.shape[1]
    fw = fourier_w.shape[1]
    sw = ssm_d.shape[1]

    cond = jnp.zeros((8, d), F32).at[0].set(c[0]).at[1].set(c_ctx)
    mod = _ada_call(cond, ada_w, ada_b)

    cos, sin = _rope_tables(l)
    zero_tab = jnp.zeros((n_ctx, HEAD_DIM), F32)
    cs = _channel_dft_table(fw)
    n1 = _dft_split(l)
    n1c = _dft_split(n_ctx)
    dft_x = _dft_tables(l, n1, FOURIER_HEAD_DIM)
    dft_c = _dft_tables(n_ctx, n1c, FOURIER_HEAD_DIM)
    h0 = jnp.zeros((sw // SSM_LANES, 1, 4 * (SSM_LANES // SSM_GROUP) * SSM_STATE), F32)

    w_in_b = w_in.astype(BF16)
    w_out_b = w_out.astype(BF16)
    glu_w_b = ssm_glu_w.astype(BF16)
    four_w_b = fourier_w.astype(BF16)
    wg_b = ffn_w_gate.astype(BF16)
    wu_b = ffn_w_up.astype(BF16)
    wd_b = ffn_w_down.astype(BF16)
    tables = jax.vmap(_ssm_tables)(ssm_lam_re, ssm_lam_im, ssm_log_dt, ssm_b_re, ssm_b_im,
                                   ssm_c_re, ssm_c_im, ssm_d)

    xs = x[0]
    xc = ctx[0]
    for layer in range(depth):
        need_ctx = layer < depth - 1
        vecs = lambda r: [mod[layer, r:r + 1, i * d:(i + 1) * d] for i in range(6)]
        sh_m, sc_m, g_m, sh_f, sc_f, g_f = vecs(0)
        shc_m, scc_m, gc_m, shc_f, scc_f, gc_f = vecs(1)
        row = lambda a: a[layer].reshape(1, -1)

        qc, kc, vtc, usc, abc = _inproj_call(
            xc, row(norm_mix_pre), shc_m, scc_m, w_in_b, layer, row(q_norm), row(k_norm),
            zero_tab, zero_tab, cs, False)
        yfc, ybc, hend_c = _ssm_apply(usc, tables, layer, h0)

        q, k, vt, us, ab = _inproj_call(xs, row(norm_mix_pre), sh_m, sc_m, w_in_b, layer,
                                        row(q_norm), row(k_norm), cos, sin, cs, True)
        attn = _attn_dispatch(q, k, vt, kc, vtc, q_norm[layer], k_norm[layer])
        yf, yb, _ = _ssm_apply(us, tables, layer, hend_c)
        four = _fourier_apply(ab, dft_x, four_w_b, layer, n1)
        xs, hs = _outproj_call(attn, yf, yb, four, xs, w_out_b, glu_w_b, layer, row(ssm_glu_b),
                               row(norm_mix_post), g_m, row(norm_ffn_pre), sh_f, sc_f)
        xs = _ffn_call(hs, xs, wg_b, wu_b, wd_b, layer, row(norm_ffn_post), g_f)

        if need_ctx:
            attn_c = _attn_call(qc, kc, vtc)
            four_c = _fourier_apply(abc, dft_c, four_w_b, layer, n1c)
            xc, hc = _outproj_call(attn_c, yfc, ybc, four_c, xc, w_out_b, glu_w_b, layer,
                                   row(ssm_glu_b), row(norm_mix_post), gc_m, row(norm_ffn_pre),
                                   shc_f, scc_f)
            xc = _ffn_call(hc, xc, wg_b, wu_b, wd_b, layer, row(norm_ffn_post), gc_f)
    return xs[None]
```

```python
import functools
import math

import numpy as np
import jax
import jax.numpy as jnp
from jax import lax
from jax.experimental import pallas as pl
from jax.experimental.pallas import tpu as pltpu

F32 = jnp.float32
BF16 = jnp.bfloat16

HEAD_DIM = 128
N_KV_HEADS = 2
Q_PER_KV = 4
GRID_W = 64
ROPE_THETA = 10000.0
ROPE_PAIRS = HEAD_DIM // 4
SSM_GROUP = 16
SSM_STATE = 64
FOURIER_HEAD_DIM = 128
NORM_EPS = 1e-6
SSM_CHUNK = 8
SSM_LANES = 128
LOG2E = 1.4426950408889634
ATTN_TQ = 1024
ATTN_TK = 1280
Q_SCALE = HEAD_DIM ** -0.5 * LOG2E
NORM_SLACK = 1.01
ATTN_MAX_GAP = 100.0
ATTN_SUB = 256
FFN_TM = 1024
FFN_TH = 512
FFN_TN = 256
VT_PAD = 16
VT_ROWS = HEAD_DIM + VT_PAD

VMEM_LIMIT_MB = 56


def _cparams(semantics, vmem_mb=VMEM_LIMIT_MB):
    return pltpu.CompilerParams(dimension_semantics=semantics,
                                vmem_limit_bytes=vmem_mb * 2 ** 20)


def _rms(x, gain):
    return x * lax.rsqrt(jnp.mean(x * x, axis=-1, keepdims=True) + NORM_EPS) * gain


def _ada_kernel(s_ref, w_ref, b_ref, o_ref):
    s = s_ref[...]
    act = (s * jax.nn.sigmoid(s)).astype(BF16)
    o_ref[0] = jnp.dot(act, w_ref[0].astype(BF16), preferred_element_type=F32) + b_ref[0]


def _ada_call(cond, ada_w, ada_b):
    depth, d, n = ada_w.shape
    tn = 1024
    return pl.pallas_call(
        _ada_kernel,
        grid=(depth, n // tn),
        in_specs=[pl.BlockSpec((8, d), lambda l, j: (0, 0)),
                  pl.BlockSpec((1, d, tn), lambda l, j: (l, 0, j)),
                  pl.BlockSpec((1, 1, tn), lambda l, j: (l, 0, j))],
        out_specs=pl.BlockSpec((1, 8, tn), lambda l, j: (l, 0, j)),
        out_shape=jax.ShapeDtypeStruct((depth, 8, n), F32),
        compiler_params=_cparams(("arbitrary", "arbitrary")),
        name="ada_ln",
    )(cond, ada_w, ada_b.reshape(depth, 1, n))


def _inproj_kernel(rope, x_ref, g_ref, sh_ref, sc_ref, w_ref, qn_ref, kn_ref, cos_ref, sin_ref,
                   cs_ref, q_ref, k_ref, vt_ref, us_ref, ab_ref):
    x = x_ref[...]
    h = (_rms(x, g_ref[...]) * (1.0 + sc_ref[...]) + sh_ref[...]).astype(BF16)
    n_q = q_ref.shape[1] // HEAD_DIM
    n_k = k_ref.shape[1] // HEAD_DIM
    q_end = n_q * HEAD_DIM
    k_end = q_end + n_k * HEAD_DIM
    v_end = k_end + n_k * HEAD_DIM
    s_end = v_end + us_ref.shape[1]
    f_end = s_end + ab_ref.shape[2]

    if rope:
        cos = cos_ref[...]
        sin = sin_ref[...]
        lane = lax.broadcasted_iota(jnp.int32, cos.shape, 1)
        low = (lane % 64) < 32

    def head(t, gain, scale):
        t = _rms(t, gain)
        if rope:
            sw = jnp.where(low, pltpu.roll(t, HEAD_DIM - 32, 1), pltpu.roll(t, 32, 1))
            t = t * cos + sw * sin
        if scale != 1.0:
            t = t * scale
        return t.astype(BF16)

    qk = jnp.dot(h, w_ref[0, :, 0:k_end], preferred_element_type=F32)
    for i in range(n_q):
        q_ref[:, i * HEAD_DIM:(i + 1) * HEAD_DIM] = head(
            qk[:, i * HEAD_DIM:(i + 1) * HEAD_DIM], qn_ref[...], Q_SCALE)
    for i in range(n_k):
        k_ref[:, i * HEAD_DIM:(i + 1) * HEAD_DIM] = head(
            qk[:, q_end + i * HEAD_DIM:q_end + (i + 1) * HEAD_DIM], kn_ref[...], 1.0)
    rest = jnp.dot(h, w_ref[0, :, k_end:f_end], preferred_element_type=F32)
    ones_rows = (lax.broadcasted_iota(jnp.int32, (VT_PAD, ATTN_SUB), 0) == 0).astype(BF16)
    for i in range(n_k):
        vt = rest[:, i * HEAD_DIM:(i + 1) * HEAD_DIM].T.astype(BF16)
        for b in range(vt_ref.shape[1]):
            vt_ref[i, b, 0:HEAD_DIM, :] = vt[:, b * ATTN_SUB:(b + 1) * ATTN_SUB]
            vt_ref[i, b, HEAD_DIM:VT_ROWS, :] = ones_rows
    us_ref[...] = rest[:, v_end - k_end:s_end - k_end]
    uf = rest[:, s_end - k_end:f_end - k_end].astype(BF16)
    ab = jnp.dot(uf, cs_ref[...], preferred_element_type=F32)
    fw = ab_ref.shape[2]
    ab_ref[0] = ab[:, 0:fw].astype(BF16)
    ab_ref[1] = ab[:, fw:2 * fw].astype(BF16)


def _layer_block(w, layer):
    return pl.BlockSpec((1,) + w.shape[1:], lambda *_: (layer, 0, 0))


def _inproj_call(x, gain, shift, scale, w_in, layer, qn, kn, cos, sin, cs, rope):
    l, d = x.shape
    tm = min(512, l)
    qw = Q_PER_KV * N_KV_HEADS * HEAD_DIM
    kw = N_KV_HEADS * HEAD_DIM
    fw = cs.shape[0]
    sw = w_in.shape[2] - qw - 2 * kw - fw
    row = lambda i: (i, 0)
    fix = lambda i: (0, 0)
    return pl.pallas_call(
        functools.partial(_inproj_kernel, rope),
        grid=(l // tm,),
        in_specs=[pl.BlockSpec((tm, d), row),
                  pl.BlockSpec((1, d), fix), pl.BlockSpec((1, d), fix), pl.BlockSpec((1, d), fix),
                  _layer_block(w_in, layer),
                  pl.BlockSpec((1, HEAD_DIM), fix), pl.BlockSpec((1, HEAD_DIM), fix),
                  pl.BlockSpec((tm, HEAD_DIM), row), pl.BlockSpec((tm, HEAD_DIM), row),
                  pl.BlockSpec(cs.shape, fix)],
        out_specs=[pl.BlockSpec((tm, qw), row),
                   pl.BlockSpec((tm, kw), row),
                   pl.BlockSpec((N_KV_HEADS, tm // ATTN_SUB, VT_ROWS, ATTN_SUB),
                                lambda i: (0, i, 0, 0)),
                   pl.BlockSpec((tm, sw), row),
                   pl.BlockSpec((2, tm, fw), lambda i: (0, i, 0))],
        out_shape=[jax.ShapeDtypeStruct((l, qw), BF16),
                   jax.ShapeDtypeStruct((l, kw), BF16),
                   jax.ShapeDtypeStruct((N_KV_HEADS, l // ATTN_SUB, VT_ROWS, ATTN_SUB), BF16),
                   jax.ShapeDtypeStruct((l, sw), F32),
                   jax.ShapeDtypeStruct((2, l, fw), BF16)],
        compiler_params=_cparams(("arbitrary",)),
        name="in_proj",
    )(x, gain, shift, scale, w_in, qn, kn, cos, sin, cs)


def _attn_kernel(nkb, tq, q_ref, k_ref, vt_ref, o_ref, qs_ref, s0_ref, s1_ref, bm0_ref, bm1_ref,
                 m_ref, acc_ref):
    j = pl.program_id(2)
    s_refs = (s0_ref, s1_ref)
    bm_refs = (bm0_ref, bm1_ref)

    tk = k_ref.shape[0]
    sub = vt_ref.shape[3]

    def step(score_slot, absorb_slot):
        if absorb_slot is not None:
            m_prev = m_ref[...]
            m_new = jnp.maximum(m_prev, bm_refs[absorb_slot][...])
            alpha = jnp.exp2(m_prev - m_new)
            pv = jnp.zeros(acc_ref.shape, F32)
        bmax = None
        for c in range(tk // sub):
            rows = slice(c * sub, (c + 1) * sub)
            if score_slot is not None:
                s = lax.dot_general(k_ref[rows, :], qs_ref[...], (((1,), (1,)), ((), ())),
                                    preferred_element_type=F32)
                s_refs[score_slot][rows, :] = s
                cmax = jnp.max(s, axis=0, keepdims=True)
                bmax = cmax if bmax is None else jnp.maximum(bmax, cmax)
            if absorb_slot is not None:
                p = jnp.exp2((s_refs[absorb_slot][rows, :] - m_new).astype(BF16))
                pv = pv + jnp.dot(vt_ref[0, c], p, preferred_element_type=F32)
        if score_slot is not None:
            bm_refs[score_slot][...] = bmax
        if absorb_slot is not None:
            acc_ref[...] = alpha * acc_ref[...] + pv
            m_ref[...] = m_new

    @pl.when(j == 0)
    def _first():
        for h in range(Q_PER_KV):
            qs_ref[h * tq:(h + 1) * tq, :] = q_ref[:, h * HEAD_DIM:(h + 1) * HEAD_DIM]
        m_ref[...] = jnp.full(m_ref.shape, -jnp.inf, F32)
        acc_ref[...] = jnp.zeros(acc_ref.shape, F32)
        step(0, None)

    for parity in (0, 1):
        @pl.when((j > 0) & (j < nkb) & (j % 2 == parity))
        def _mid(parity=parity):
            step(parity, 1 - parity)

    @pl.when(j == nkb)
    def _last():
        step(None, (nkb - 1) % 2)
        o = acc_ref[0:HEAD_DIM, :] / acc_ref[HEAD_DIM:HEAD_DIM + 1, :]
        for h in range(Q_PER_KV):
            o_ref[:, h * HEAD_DIM:(h + 1) * HEAD_DIM] = o[:, h * tq:(h + 1) * tq].T.astype(BF16)


def _pick_block(total, unit, cap):
    best = unit
    for b in range(unit, cap + 1, unit):
        if total % b == 0:
            best = b
    return best


def _attn_bounded_kernel(unroll, tq, kn_ref, q_ref, k_ref, vt_ref, kc_ref, vtc_ref, o_ref, qs_ref,
                         acc_ref):
    sub = vt_ref.shape[3]
    for h in range(Q_PER_KV):
        qs_ref[h * tq:(h + 1) * tq, :] = q_ref[:, h * HEAD_DIM:(h + 1) * HEAD_DIM]
    qf = qs_ref[...].astype(F32)
    norm2 = lax.dot_general(jnp.ones((8, HEAD_DIM), F32), qf * qf, (((1,), (1,)), ((), ())),
                            preferred_element_type=F32)[0:1, :]
    shift = jnp.sqrt(norm2) * kn_ref[pl.program_id(0)]

    def weighted_values(keys, vt):
        s = lax.dot_general(keys, qs_ref[...], (((1,), (1,)), ((), ())),
                            preferred_element_type=F32)
        return jnp.dot(vt, jnp.exp2(s - shift).astype(BF16), preferred_element_type=F32)

    acc = None
    for c in range(vtc_ref.shape[1]):
        d = weighted_values(kc_ref[c * sub:(c + 1) * sub, :], vtc_ref[0, c])
        acc = d if acc is None else acc + d
    acc_ref[...] = acc

    def body(t, carry):
        pv = None
        for u in range(unroll):
            c = t * unroll + u
            start = pl.multiple_of(c * sub, sub)
            d = weighted_values(k_ref[pl.ds(start, sub), :], vt_ref[0, c])
            pv = d if pv is None else pv + d
        acc_ref[...] += pv
        return carry

    lax.fori_loop(0, vt_ref.shape[1] // unroll, body, 0)
    o = acc_ref[0:HEAD_DIM, :] / acc_ref[HEAD_DIM:HEAD_DIM + 1, :]
    for h in range(Q_PER_KV):
        o_ref[:, h * HEAD_DIM:(h + 1) * HEAD_DIM] = o[:, h * tq:(h + 1) * tq].T.astype(BF16)


def _attn_bounded_call(q, k, vt, kc, vtc, k_norm_max):
    l = q.shape[0]
    nblk = vt.shape[1]
    tq = min(ATTN_TQ, l)
    gw = Q_PER_KV * HEAD_DIM
    n = Q_PER_KV * tq
    unroll = max(u for u in (8, 5, 4, 3, 2, 1) if nblk % u == 0)
    head_rows = lambda a: pl.BlockSpec((a.shape[0], HEAD_DIM), lambda g, i: (0, g))
    head_blocks = lambda a: pl.BlockSpec((1,) + a.shape[1:], lambda g, i: (g, 0, 0, 0))
    return pl.pallas_call(
        functools.partial(_attn_bounded_kernel, unroll, tq),
        grid=(N_KV_HEADS, l // tq),
        in_specs=[pl.BlockSpec(memory_space=pltpu.SMEM),
                  pl.BlockSpec((tq, gw), lambda g, i: (i, g)),
                  head_rows(k), head_blocks(vt), head_rows(kc), head_blocks(vtc)],
        out_specs=pl.BlockSpec((tq, gw), lambda g, i: (i, g)),
        out_shape=jax.ShapeDtypeStruct(q.shape, BF16),
        scratch_shapes=[pltpu.VMEM((n, HEAD_DIM), BF16), pltpu.VMEM((VT_ROWS, n), F32)],
        compiler_params=_cparams(("arbitrary", "arbitrary")),
        name="attention_bounded",
    )(k_norm_max, q, k, vt, kc, vtc)


def _attn_dispatch(q, k, vt, kc, vtc, q_gain, k_gain):
    bound = lambda gain: NORM_SLACK * math.sqrt(HEAD_DIM) * jnp.max(jnp.abs(gain.astype(F32)))
    qn = Q_SCALE * bound(q_gain)
    kn = jnp.full((N_KV_HEADS,), bound(k_gain), F32)
    safe = 2.0 * qn * jnp.max(kn) <= ATTN_MAX_GAP
    return lax.cond(safe,
                    lambda: _attn_bounded_call(q, k, vt, kc, vtc, kn),
                    lambda: _attn_call(q, jnp.concatenate([k, kc], axis=0),
                                       jnp.concatenate([vt, vtc], axis=1)))


def _attn_call(q, k_all, vt_all):
    l = q.shape[0]
    s_len = k_all.shape[0]
    sub = vt_all.shape[3]
    tq = min(ATTN_TQ, l)
    tk = _pick_block(s_len, sub, ATTN_TK)
    nkb = s_len // tk
    gw = Q_PER_KV * HEAD_DIM
    n = Q_PER_KV * tq
    return pl.pallas_call(
        functools.partial(_attn_kernel, nkb, tq),
        grid=(N_KV_HEADS, l // tq, nkb + 1),
        in_specs=[pl.BlockSpec((tq, gw), lambda g, i, j: (i, g)),
                  pl.BlockSpec((tk, HEAD_DIM), lambda g, i, j: (jnp.minimum(j, nkb - 1), g)),
                  pl.BlockSpec((1, tk // sub, VT_ROWS, sub),
                               lambda g, i, j: (g, jnp.maximum(j - 1, 0), 0, 0))],
        out_specs=pl.BlockSpec((tq, gw), lambda g, i, j: (i, g)),
        out_shape=jax.ShapeDtypeStruct(q.shape, BF16),
        scratch_shapes=[pltpu.VMEM((n, HEAD_DIM), BF16),
                        pltpu.VMEM((tk, n), F32),
                        pltpu.VMEM((tk, n), F32),
                        pltpu.VMEM((1, n), F32),
                        pltpu.VMEM((1, n), F32),
                        pltpu.VMEM((1, n), F32),
                        pltpu.VMEM((VT_ROWS, n), F32)],
        compiler_params=_cparams(("arbitrary", "arbitrary", "arbitrary")),
        name="attention",
    )(q, k_all, vt_all)


def _ssm_tables(lam_re, lam_im, log_dt, b_re, b_im, c_re, c_im, d):
    t = SSM_CHUNK
    g, p = lam_re.shape[1:]
    hdim = b_re.shape[-1]
    lr = lam_re.astype(F32)
    li = lam_im.astype(F32)
    dt = jnp.exp(log_dt.astype(F32))[..., None]
    mag = jnp.exp(lr * dt)
    ar = mag * jnp.cos(li * dt)
    ai = mag * jnp.sin(li * dt)
    den = lr * lr + li * li
    cr = ((ar - 1.0) * lr + ai * li) / den
    ci = (ai * lr - (ar - 1.0) * li) / den
    br = b_re.astype(F32)
    bi = b_im.astype(F32)
    bbr = cr[..., None] * br - ci[..., None] * bi
    bbi = cr[..., None] * bi + ci[..., None] * br
    n = jnp.arange(t + 1, dtype=F32)[:, None, None, None]
    pmag = jnp.exp(n * (lr * dt)[None])
    pr = pmag * jnp.cos(n * (li * dt)[None])
    pi = pmag * jnp.sin(n * (li * dt)[None])
    cre = c_re.astype(F32)
    cim = c_im.astype(F32)

    gs = SSM_LANES // hdim
    n_sg = g // gs

    def selector(r, inner):
        e = np.zeros((gs, r, gs, inner, r, inner), np.float32)
        for grp in range(gs):
            e[grp, :, grp] = np.eye(r * inner, dtype=np.float32).reshape(r, inner, r, inner)
        return jnp.asarray(e.reshape(gs, r * gs * inner, r * inner), dtype=BF16)

    def expand(small):
        _, r1, i1, r2, i2 = small.shape
        blocks = small.reshape(n_sg, gs, r1 * i1, r2 * i2).astype(BF16)
        wide = jnp.einsum('sgkl,gcl->sgkc', blocks, selector(r2, i2),
                          preferred_element_type=F32).astype(BF16)
        wide = wide.reshape(n_sg, gs, r1, i1, r2 * gs * i2)
        return jnp.transpose(wide, (0, 2, 1, 3, 4)).reshape(n_sg, r1 * gs * i1, r2 * gs * i2)

    def summ(direction, powers):
        er = pr[powers, direction][..., None] * bbr[direction][None] \
            - pi[powers, direction][..., None] * bbi[direction][None]
        ei = pr[powers, direction][..., None] * bbi[direction][None] \
            + pi[powers, direction][..., None] * bbr[direction][None]
        tr = lambda e: jnp.transpose(e, (1, 0, 3, 2))
        return expand(jnp.stack([tr(er), tr(ei)], axis=3))
    steps = jnp.arange(t)
    p_f = summ(0, t - 1 - steps)
    p_b = summ(1, steps)

    row = lambda v: v.reshape(n_sg, gs * p)
    a_row = jnp.concatenate([row(pr[t, 0]), row(pr[t, 1]), row(pi[t, 0]), row(pi[t, 1])],
                            axis=-1)[:, None, :]

    def kern(direction):
        er = pr[:t, direction][..., None] * bbr[direction][None] \
            - pi[:t, direction][..., None] * bbi[direction][None]
        ei = pr[:t, direction][..., None] * bbi[direction][None] \
            + pi[:t, direction][..., None] * bbr[direction][None]
        return (jnp.einsum('gop,ngpi->ngoi', cre[direction], er)
                - jnp.einsum('gop,ngpi->ngoi', cim[direction], ei))
    kf = kern(0)
    kb = kern(1)
    s_idx = steps[:, None]
    t_idx = steps[None, :]
    lag_f = jnp.clip(t_idx - s_idx, 0, t - 1)
    lag_b = jnp.clip(s_idx - t_idx, 0, t - 1)
    mf = jnp.where((t_idx >= s_idx)[:, :, None, None, None], kf[lag_f], 0.0)
    mb = jnp.where((s_idx >= t_idx)[:, :, None, None, None], kb[lag_b], 0.0)
    dd = d.astype(F32).reshape(g, hdim)
    eye_t = jnp.eye(t, dtype=F32)[:, :, None, None, None]
    eye_c = jnp.eye(hdim, dtype=F32)[None, None, None]
    m5 = mf + mb + eye_t * eye_c * dd[None, None, :, :, None]
    m_mat = expand(jnp.transpose(m5, (2, 0, 4, 1, 3)))

    def state_out(direction, powers):
        prn = pr[powers, direction]
        pin = pi[powers, direction]
        wr = cre[direction][None] * prn[:, :, None, :] - cim[direction][None] * pin[:, :, None, :]
        wi = -cre[direction][None] * pin[:, :, None, :] - cim[direction][None] * prn[:, :, None, :]
        tr = lambda w: jnp.transpose(w, (1, 3, 0, 2))
        return expand(jnp.stack([tr(wr), tr(wi)], axis=1))
    q_f = state_out(0, steps + 1)
    q_b = state_out(1, t - steps)
    return p_f, p_b, m_mat, q_f, q_b, a_row


def _ssm_kernel(tbk, usf_ref, usb_ref, pf_ref, pb_ref, m_ref, qf_ref, qb_ref, a_ref, h0_ref,
                yf_ref, yb_ref, hend_ref, xf_ref, xb_ref, sf_ref, sb_ref, hf_ref, hb_ref, h_ref):
    t = SSM_CHUNK
    lanes = SSM_LANES

    @pl.when(pl.program_id(1) == 0)
    def _init():
        h_ref[...] = h0_ref[0]

    for tt in range(t):
        cols = slice(tt * lanes, (tt + 1) * lanes)
        xf_ref[:, cols] = usf_ref[pl.ds(tt, tbk, stride=t), :].astype(BF16)
        xb_ref[:, cols] = usb_ref[pl.ds(tt, tbk, stride=t), :].astype(BF16)
    sf_ref[...] = jnp.dot(xf_ref[...], pf_ref[0, 0], preferred_element_type=F32)
    sb_ref[...] = jnp.dot(xb_ref[...], pb_ref[0, 0], preferred_element_type=F32)

    w = a_ref.shape[3] // 4
    a = a_ref[0, 0]
    arf, arb, aif, aib = (a[:, i * w:(i + 1) * w] for i in range(4))
    h = h_ref[...]

    def body(k, carry):
        hrf, hrb, hif, hib = carry
        kb = tbk - 1 - k
        hf_ref[pl.ds(k, 1), 0:w] = hrf
        hf_ref[pl.ds(k, 1), w:2 * w] = hif
        hb_ref[pl.ds(kb, 1), 0:w] = hrb
        hb_ref[pl.ds(kb, 1), w:2 * w] = hib
        sf = sf_ref[pl.ds(k, 1), :]
        sb = sb_ref[pl.ds(kb, 1), :]
        return (arf * hrf - aif * hif + sf[:, 0:w], arb * hrb - aib * hib + sb[:, 0:w],
                arf * hif + aif * hrf + sf[:, w:2 * w], arb * hib + aib * hrb + sb[:, w:2 * w])

    carry = lax.fori_loop(0, tbk, body, tuple(h[:, i * w:(i + 1) * w] for i in range(4)),
                          unroll=8)
    h = jnp.concatenate(carry, axis=1)
    h_ref[...] = h
    hend_ref[0] = h

    yf = (jnp.dot(xf_ref[...], m_ref[0, 0], preferred_element_type=F32)
          + jnp.dot(hf_ref[...].astype(BF16), qf_ref[0, 0], preferred_element_type=F32))
    yb = jnp.dot(hb_ref[...].astype(BF16), qb_ref[0, 0], preferred_element_type=F32)
    for tt in range(t):
        cols = slice(tt * lanes, (tt + 1) * lanes)
        yf_ref[pl.ds(tt, tbk, stride=t), :] = yf[:, cols]
        yb_ref[pl.ds(tt, tbk, stride=t), :] = yb[:, cols]


def _ssm_apply(us, tables, layer, h0):
    p_f, p_b, m_mat, q_f, q_b, a_row = tables
    l, w = us.shape
    n_sg = w // SSM_LANES
    n = l // SSM_CHUNK
    tbk = min(256, n)
    nb = n // tbk
    rows = tbk * SSM_CHUNK
    kx = SSM_CHUNK * SSM_LANES
    sw = p_f.shape[3]
    hw = a_row.shape[3]
    per_sg = lambda a: pl.BlockSpec((1,) + a.shape[1:], lambda s, i: (s, 0, 0))
    table = lambda a: pl.BlockSpec((1, 1) + a.shape[2:], lambda s, i: (layer, s, 0, 0))
    fwd = pl.BlockSpec((rows, SSM_LANES), lambda s, i: (i, s))
    bwd = pl.BlockSpec((rows, SSM_LANES), lambda s, i: (nb - 1 - i, s))
    return pl.pallas_call(
        functools.partial(_ssm_kernel, tbk),
        grid=(n_sg, nb),
        in_specs=[fwd, bwd, table(p_f), table(p_b), table(m_mat), table(q_f), table(q_b),
                  table(a_row), per_sg(h0)],
        out_specs=[fwd, bwd, per_sg(h0)],
        out_shape=[jax.ShapeDtypeStruct((l, w), F32), jax.ShapeDtypeStruct((l, w), F32),
                   jax.ShapeDtypeStruct(h0.shape, F32)],
        scratch_shapes=[pltpu.VMEM((tbk, kx), BF16), pltpu.VMEM((tbk, kx), BF16),
                        pltpu.VMEM((tbk, sw), F32), pltpu.VMEM((tbk, sw), F32),
                        pltpu.VMEM((tbk, sw), F32), pltpu.VMEM((tbk, sw), F32),
                        pltpu.VMEM((1, hw), F32)],
        compiler_params=_cparams(("arbitrary", "arbitrary")),
        name="ssm_scan",
    )(us, us, p_f, p_b, m_mat, q_f, q_b, a_row, h0)


def _dft_tables(l, n1, hw):
    n2 = l // n1
    scale = 1.0 / math.sqrt(l * hw)
    if n1 == 1:
        t1 = None
    else:
        ang = 2.0 * np.pi * ((np.arange(n1)[:, None] * np.arange(n1)[None, :]) % n1) / n1
        c, s = np.cos(ang), np.sin(ang)
        t1 = jnp.asarray(np.block([[c, -s], [-s, -c]]), dtype=BF16)
    k1 = np.arange(n1)[:, None, None]
    k2 = np.arange(n2)[None, :, None]
    m = np.arange(n2)[None, None, :]
    ang = 2.0 * np.pi * ((m * (n1 * k2 + k1)) % l) / l
    sign = -1.0 if n1 == 1 else 1.0
    g = np.concatenate([np.cos(ang), sign * np.sin(ang)], axis=-1) * scale
    return t1, jnp.asarray(g, dtype=BF16)


def _dft1_kernel(t_ref, x_ref, z_ref):
    z_ref[...] = jnp.dot(t_ref[...], x_ref[...], preferred_element_type=F32).astype(z_ref.dtype)


def _dft1_call(t1, ab2d):
    r, n = ab2d.shape
    tn = min(4096, n)
    return pl.pallas_call(
        _dft1_kernel,
        grid=(n // tn,),
        in_specs=[pl.BlockSpec((r, r), lambda j: (0, 0)),
                  pl.BlockSpec((r, tn), lambda j: (0, j))],
        out_specs=pl.BlockSpec((r, tn), lambda j: (0, j)),
        out_shape=jax.ShapeDtypeStruct((r, n), BF16),
        compiler_params=_cparams(("arbitrary",)),
        name="dft_stage1",
    )(t1, ab2d)


def _dft2_kernel(bt, zr_ref, zi_ref, g_ref, w_ref, o_ref):
    fw = w_ref.shape[1]
    for b in range(bt):
        z = jnp.concatenate([zr_ref[b], zi_ref[b]], axis=0)
        x = jnp.dot(g_ref[b], z, preferred_element_type=F32)
        o_ref[:, b * fw:(b + 1) * fw] = jnp.dot(
            x.astype(BF16), w_ref[0], preferred_element_type=F32).astype(o_ref.dtype)


def _dft2_call(z3, g, fourier_w, layer):
    n1x2, n2, fw = z3.shape
    n1 = n1x2 // 2
    bt = min(8, n1)
    return pl.pallas_call(
        functools.partial(_dft2_kernel, bt),
        grid=(n1 // bt,),
        in_specs=[pl.BlockSpec((bt, n2, fw), lambda b: (b, 0, 0)),
                  pl.BlockSpec((bt, n2, fw), lambda b: (b + n1 // bt, 0, 0)),
                  pl.BlockSpec((bt, n2, 2 * n2), lambda b: (b, 0, 0)),
                  _layer_block(fourier_w, layer)],
        out_specs=pl.BlockSpec((n2, bt * fw), lambda b: (0, b)),
        out_shape=jax.ShapeDtypeStruct((n2, n1 * fw), BF16),
        compiler_params=_cparams(("arbitrary",)),
        name="dft_stage2",
    )(z3, z3, g, fourier_w)


def _fourier_apply(ab, tables, fourier_w, layer, n1):
    t1, g = tables
    _, l, fw = ab.shape
    n2 = l // n1
    if n1 == 1:
        z3 = ab
    else:
        z3 = _dft1_call(t1, ab.reshape(2 * n1, n2 * fw)).reshape(2 * n1, n2, fw)
    return _dft2_call(z3, g, fourier_w, layer).reshape(l, fw)


def _outproj_kernel(attn_ref, yf_ref, yb_ref, four_ref, x_ref, wo_ref, gw_ref, gb_ref, npost_ref,
                    gate_ref, npre_ref, sh_ref, sc_ref, xo_ref, h_ref):
    tm = x_ref.shape[0]
    n_split = 2 if tm % 32 == 0 else 1
    for k in range(n_split):
        rows = slice(k * (tm // n_split), (k + 1) * (tm // n_split))
        ys = yf_ref[rows, :] + yb_ref[rows, :]
        gl = 0.5 * ys * (1.0 + jnp.tanh(math.sqrt(2.0 / math.pi)
                                        * (ys + 0.044715 * (ys * ys * ys))))
        z = jnp.dot(gl.astype(BF16), gw_ref[0], preferred_element_type=F32) + gb_ref[...]
        ssm = (gl * jax.nn.sigmoid(z)).astype(BF16)
        cat = jnp.concatenate([attn_ref[rows, :], ssm, four_ref[rows, :]], axis=-1)
        y = jnp.dot(cat, wo_ref[0], preferred_element_type=F32)
        xn = x_ref[rows, :] + gate_ref[...] * _rms(y, npost_ref[...])
        xo_ref[rows, :] = xn
        h_ref[rows, :] = (_rms(xn, npre_ref[...]) * (1.0 + sc_ref[...])
                          + sh_ref[...]).astype(BF16)


def _outproj_call(attn, yf, yb, four, x, w_out, glu_w, layer, glu_b, npost, gate, npre, shift,
                  scale):
    l, d = x.shape
    tm = min(512, l)
    row = lambda i: (i, 0)
    fix = lambda i: (0, 0)
    vec = pl.BlockSpec((1, d), fix)
    return pl.pallas_call(
        _outproj_kernel,
        grid=(l // tm,),
        in_specs=[pl.BlockSpec((tm, attn.shape[1]), row),
                  pl.BlockSpec((tm, yf.shape[1]), row),
                  pl.BlockSpec((tm, yb.shape[1]), row),
                  pl.BlockSpec((tm, four.shape[1]), row),
                  pl.BlockSpec((tm, d), row),
                  _layer_block(w_out, layer),
                  _layer_block(glu_w, layer),
                  pl.BlockSpec((1, glu_w.shape[2]), fix),
                  vec, vec, vec, vec, vec],
        out_specs=[pl.BlockSpec((tm, d), row), pl.BlockSpec((tm, d), row)],
        out_shape=[jax.ShapeDtypeStruct((l, d), F32), jax.ShapeDtypeStruct((l, d), BF16)],
        compiler_params=_cparams(("arbitrary",)),
        name="out_proj",
    )(attn, yf, yb, four, x, w_out, glu_w, glu_b, npost, gate, npre, shift, scale)


def _ffn_kernel(nj1, nj2, th, tn, h_ref, x_ref, wg_ref, wu_ref, wd_ref, npost_ref, gate_ref, o_ref,
                act_ref, tmp_ref):
    j = pl.program_id(1)

    @pl.when(j < nj1)
    def _up():
        h = h_ref[...]
        a = jnp.dot(h, wg_ref[0], preferred_element_type=F32)
        u = jnp.dot(h, wu_ref[0], preferred_element_type=F32)
        tmp_ref[...] = (a * jax.nn.sigmoid(a) * u).astype(BF16)

    for c in range(nj1):
        @pl.when(j == c)
        def _place(c=c):
            act_ref[:, c * th:(c + 1) * th] = tmp_ref[...]

    for c in range(nj2):
        @pl.when(j == nj1 + c)
        def _down(c=c):
            o_ref[:, c * tn:(c + 1) * tn] = jnp.dot(act_ref[...], wd_ref[0],
                                                    preferred_element_type=F32)

    @pl.when(j == nj1 + nj2 - 1)
    def _finish():
        o_ref[...] = x_ref[...] + gate_ref[...] * _rms(o_ref[...], npost_ref[...])


def _ffn_call(h, x, w_gate, w_up, w_down, layer, npost, gate):
    l, d = x.shape
    fh = w_gate.shape[2]
    th, tn = FFN_TH, FFN_TN
    nj1 = fh // th
    nj2 = d // tn
    tm = min(FFN_TM, l)
    vec = pl.BlockSpec((1, d), lambda i, j: (0, 0))
    up = pl.BlockSpec((1, d, th), lambda i, j: (layer, 0, jnp.minimum(j, nj1 - 1)))
    once = pl.Buffered(1)
    return pl.pallas_call(
        functools.partial(_ffn_kernel, nj1, nj2, th, tn),
        grid=(l // tm, nj1 + nj2),
        in_specs=[pl.BlockSpec((tm, d), lambda i, j: (i, 0)),
                  pl.BlockSpec((tm, d), lambda i, j: (i, 0), pipeline_mode=once),
                  up, up,
                  pl.BlockSpec((1, fh, tn), lambda i, j: (layer, 0, jnp.maximum(j - nj1, 0))),
                  vec, vec],
        out_specs=pl.BlockSpec((tm, d), lambda i, j: (i, 0), pipeline_mode=once),
        out_shape=jax.ShapeDtypeStruct((l, d), F32),
        scratch_shapes=[pltpu.VMEM((tm, fh), BF16), pltpu.VMEM((tm, th), BF16)],
        compiler_params=_cparams(("arbitrary", "arbitrary")),
        name="ffn",
    )(h, x, w_gate, w_up, w_down, npost, gate)


def _rope_tables(l):
    t = np.arange(l)
    row = (t // GRID_W).astype(np.float32)
    col = (t % GRID_W).astype(np.float32)
    freqs = np.float32(ROPE_THETA) ** (-np.arange(ROPE_PAIRS, dtype=np.float32) / np.float32(ROPE_PAIRS))
    ang_r = (row[:, None] * freqs).astype(np.float32)
    ang_c = (col[:, None] * freqs).astype(np.float32)
    cos = np.concatenate([np.cos(ang_r)] * 2 + [np.cos(ang_c)] * 2, axis=-1)
    sin = np.concatenate([-np.sin(ang_r), np.sin(ang_r), -np.sin(ang_c), np.sin(ang_c)], axis=-1)
    return jnp.asarray(cos, dtype=F32), jnp.asarray(sin, dtype=F32)


def _channel_dft_table(fw):
    hw = FOURIER_HEAD_DIM
    ang = 2.0 * np.pi * ((np.arange(hw)[:, None] * np.arange(hw)[None, :]) % hw) / hw
    eye = np.eye(fw // hw)
    return jnp.asarray(np.concatenate([np.kron(eye, np.cos(ang)), np.kron(eye, np.sin(ang))], axis=1),
                       dtype=BF16)


def _dft_split(l):
    n1 = 1
    while n1 * n1 < l:
        n1 *= 2
    return n1 if (l >= 1024 and n1 * n1 == l) else 1


def kernel(x, c, ctx, c_ctx, ada_w, ada_b, norm_mix_pre, norm_mix_post, norm_ffn_pre, norm_ffn_post, w_in, q_norm, k_norm, ssm_lam_re, ssm_lam_im, ssm_log_dt, ssm_b_re, ssm_b_im, ssm_c_re, ssm_c_im, ssm_d, ssm_glu_w, ssm_glu_b, fourier_w, w_out, ffn_w_gate, ffn_w_up, ffn_w_down):
    depth = ada_w.shape[0]
    _, l, d = x.shape
    n_ctx = ctx.shape[1]
    fw = fourier_w.shape[1]
    sw = ssm_d.shape[1]

    cond = jnp.zeros((8, d), F32).at[0].set(c[0]).at[1].set(c_ctx)
    mod = _ada_call(cond, ada_w, ada_b)

    cos, sin = _rope_tables(l)
    zero_tab = jnp.zeros((n_ctx, HEAD_DIM), F32)
    cs = _channel_dft_table(fw)
    n1 = _dft_split(l)
    n1c = _dft_split(n_ctx)
    dft_x = _dft_tables(l, n1, FOURIER_HEAD_DIM)
    dft_c = _dft_tables(n_ctx, n1c, FOURIER_HEAD_DIM)
    h0 = jnp.zeros((sw // SSM_LANES, 1, 4 * (SSM_LANES // SSM_GROUP) * SSM_STATE), F32)

    w_in_b = w_in.astype(BF16)
    w_out_b = w_out.astype(BF16)
    glu_w_b = ssm_glu_w.astype(BF16)
    four_w_b = fourier_w.astype(BF16)
    wg_b = ffn_w_gate.astype(BF16)
    wu_b = ffn_w_up.astype(BF16)
    wd_b = ffn_w_down.astype(BF16)
    tables = jax.vmap(_ssm_tables)(ssm_lam_re, ssm_lam_im, ssm_log_dt, ssm_b_re, ssm_b_im,
                                   ssm_c_re, ssm_c_im, ssm_d)

    xs = x[0]
    xc = ctx[0]
    for layer in range(depth):
        need_ctx = layer < depth - 1
        vecs = lambda r: [mod[layer, r:r + 1, i * d:(i + 1) * d] for i in range(6)]
        sh_m, sc_m, g_m, sh_f, sc_f, g_f = vecs(0)
        shc_m, scc_m, gc_m, shc_f, scc_f, gc_f = vecs(1)
        row = lambda a: a[layer].reshape(1, -1)

        qc, kc, vtc, usc, abc = _inproj_call(
            xc, row(norm_mix_pre), shc_m, scc_m, w_in_b, layer, row(q_norm), row(k_norm),
            zero_tab, zero_tab, cs, False)
        yfc, ybc, hend_c = _ssm_apply(usc, tables, layer, h0)

        q, k, vt, us, ab = _inproj_call(xs, row(norm_mix_pre), sh_m, sc_m, w_in_b, layer,
                                        row(q_norm), row(k_norm), cos, sin, cs, True)
        attn = _attn_dispatch(q, k, vt, kc, vtc, q_norm[layer], k_norm[layer])
        yf, yb, _ = _ssm_apply(us, tables, layer, hend_c)
        four = _fourier_apply(ab, dft_x, four_w_b, layer, n1)
        xs, hs = _outproj_call(attn, yf, yb, four, xs, w_out_b, glu_w_b, layer, row(ssm_glu_b),
                               row(norm_mix_post), g_m, row(norm_ffn_pre), sh_f, sc_f)
        xs = _ffn_call(hs, xs, wg_b, wu_b, wd_b, layer, row(norm_ffn_post), g_f)

        if need_ctx:
            attn_c = _attn_call(qc, kc, vtc)
            four_c = _fourier_apply(abc, dft_c, four_w_b, layer, n1c)
            xc, hc = _outproj_call(attn_c, yfc, ybc, four_c, xc, w_out_b, glu_w_b, layer,
                                   row(ssm_glu_b), row(norm_mix_post), gc_m, row(norm_ffn_pre),
                                   shc_f, scc_f)
            xc = _ffn_call(hc, xc, wg_b, wu_b, wd_b, layer, row(norm_ffn_post), gc_f)
    return xs[None]
```

```python
import functools
import math

import numpy as np
import jax
import jax.numpy as jnp
from jax import lax
from jax.experimental import pallas as pl
from jax.experimental.pallas import tpu as pltpu

F32 = jnp.float32
BF16 = jnp.bfloat16

HEAD_DIM = 128
N_KV_HEADS = 2
Q_PER_KV = 4
GRID_W = 64
ROPE_THETA = 10000.0
ROPE_PAIRS = HEAD_DIM // 4
SSM_GROUP = 16
SSM_STATE = 64
FOURIER_HEAD_DIM = 128
NORM_EPS = 1e-6
SSM_CHUNK = 8
SSM_LANES = 128
LOG2E = 1.4426950408889634
ATTN_TQ = 1024
ATTN_TK = 1280
Q_SCALE = HEAD_DIM ** -0.5 * LOG2E
NORM_SLACK = 1.01
ATTN_MAX_GAP = 100.0
ATTN_SUB = 256
FFN_TM = 1024
FFN_TH = 512
FFN_TN = 512
FFN_VMEM_MB = 62
VT_PAD = 16
VT_ROWS = HEAD_DIM + VT_PAD

VMEM_LIMIT_MB = 56


def _cparams(semantics, vmem_mb=VMEM_LIMIT_MB):
    return pltpu.CompilerParams(dimension_semantics=semantics,
                                vmem_limit_bytes=vmem_mb * 2 ** 20)


def _rms(x, gain):
    return x * lax.rsqrt(jnp.mean(x * x, axis=-1, keepdims=True) + NORM_EPS) * gain


def _ada_kernel(s_ref, w_ref, b_ref, o_ref):
    s = s_ref[...]
    act = (s * jax.nn.sigmoid(s)).astype(BF16)
    o_ref[0] = jnp.dot(act, w_ref[0].astype(BF16), preferred_element_type=F32) + b_ref[0]


def _ada_call(cond, ada_w, ada_b):
    depth, d, n = ada_w.shape
    tn = 1024
    return pl.pallas_call(
        _ada_kernel,
        grid=(depth, n // tn),
        in_specs=[pl.BlockSpec((8, d), lambda l, j: (0, 0)),
                  pl.BlockSpec((1, d, tn), lambda l, j: (l, 0, j)),
                  pl.BlockSpec((1, 1, tn), lambda l, j: (l, 0, j))],
        out_specs=pl.BlockSpec((1, 8, tn), lambda l, j: (l, 0, j)),
        out_shape=jax.ShapeDtypeStruct((depth, 8, n), F32),
        compiler_params=_cparams(("arbitrary", "arbitrary")),
        name="ada_ln",
    )(cond, ada_w, ada_b.reshape(depth, 1, n))


def _inproj_kernel(rope, x_ref, g_ref, sh_ref, sc_ref, w_ref, qn_ref, kn_ref, cos_ref, sin_ref,
                   cs_ref, q_ref, k_ref, vt_ref, us_ref, ab_ref):
    x = x_ref[...]
    h = (_rms(x, g_ref[...]) * (1.0 + sc_ref[...]) + sh_ref[...]).astype(BF16)
    n_q = q_ref.shape[1] // HEAD_DIM
    n_k = k_ref.shape[1] // HEAD_DIM
    q_end = n_q * HEAD_DIM
    k_end = q_end + n_k * HEAD_DIM
    v_end = k_end + n_k * HEAD_DIM
    s_end = v_end + us_ref.shape[1]
    f_end = s_end + ab_ref.shape[2]

    if rope:
        cos = cos_ref[...]
        sin = sin_ref[...]
        lane = lax.broadcasted_iota(jnp.int32, cos.shape, 1)
        low = (lane % 64) < 32

    def head(t, gain, scale):
        t = _rms(t, gain)
        if rope:
            sw = jnp.where(low, pltpu.roll(t, HEAD_DIM - 32, 1), pltpu.roll(t, 32, 1))
            t = t * cos + sw * sin
        if scale != 1.0:
            t = t * scale
        return t.astype(BF16)

    qk = jnp.dot(h, w_ref[0, :, 0:k_end], preferred_element_type=F32)
    for i in range(n_q):
        q_ref[:, i * HEAD_DIM:(i + 1) * HEAD_DIM] = head(
            qk[:, i * HEAD_DIM:(i + 1) * HEAD_DIM], qn_ref[...], Q_SCALE)
    for i in range(n_k):
        k_ref[:, i * HEAD_DIM:(i + 1) * HEAD_DIM] = head(
            qk[:, q_end + i * HEAD_DIM:q_end + (i + 1) * HEAD_DIM], kn_ref[...], 1.0)
    rest = jnp.dot(h, w_ref[0, :, k_end:f_end], preferred_element_type=F32)
    ones_rows = (lax.broadcasted_iota(jnp.int32, (VT_PAD, ATTN_SUB), 0) == 0).astype(BF16)
    for i in range(n_k):
        vt = rest[:, i * HEAD_DIM:(i + 1) * HEAD_DIM].T.astype(BF16)
        for b in range(vt_ref.shape[1]):
            vt_ref[i, b, 0:HEAD_DIM, :] = vt[:, b * ATTN_SUB:(b + 1) * ATTN_SUB]
            vt_ref[i, b, HEAD_DIM:VT_ROWS, :] = ones_rows
    us_ref[...] = rest[:, v_end - k_end:s_end - k_end]
    uf = rest[:, s_end - k_end:f_end - k_end].astype(BF16)
    ab = jnp.dot(uf, cs_ref[...].astype(BF16), preferred_element_type=F32)
    fw = ab_ref.shape[2]
    ab_ref[0] = ab[:, 0:fw].astype(BF16)
    ab_ref[1] = ab[:, fw:2 * fw].astype(BF16)


def _layer_block(w, layer):
    return pl.BlockSpec((1,) + w.shape[1:], lambda *_: (layer, 0, 0))


def _inproj_call(x, gain, shift, scale, w_in, layer, qn, kn, cos, sin, cs, rope):
    l, d = x.shape
    tm = min(512, l)
    qw = Q_PER_KV * N_KV_HEADS * HEAD_DIM
    kw = N_KV_HEADS * HEAD_DIM
    fw = cs.shape[0]
    sw = w_in.shape[2] - qw - 2 * kw - fw
    row = lambda i: (i, 0)
    fix = lambda i: (0, 0)
    return pl.pallas_call(
        functools.partial(_inproj_kernel, rope),
        grid=(l // tm,),
        in_specs=[pl.BlockSpec((tm, d), row),
                  pl.BlockSpec((1, d), fix), pl.BlockSpec((1, d), fix), pl.BlockSpec((1, d), fix),
                  _layer_block(w_in, layer),
                  pl.BlockSpec((1, HEAD_DIM), fix), pl.BlockSpec((1, HEAD_DIM), fix),
                  pl.BlockSpec((tm, HEAD_DIM), row), pl.BlockSpec((tm, HEAD_DIM), row),
                  pl.BlockSpec(cs.shape, fix)],
        out_specs=[pl.BlockSpec((tm, qw), row),
                   pl.BlockSpec((tm, kw), row),
                   pl.BlockSpec((N_KV_HEADS, tm // ATTN_SUB, VT_ROWS, ATTN_SUB),
                                lambda i: (0, i, 0, 0)),
                   pl.BlockSpec((tm, sw), row),
                   pl.BlockSpec((2, tm, fw), lambda i: (0, i, 0))],
        out_shape=[jax.ShapeDtypeStruct((l, qw), BF16),
                   jax.ShapeDtypeStruct((l, kw), BF16),
                   jax.ShapeDtypeStruct((N_KV_HEADS, l // ATTN_SUB, VT_ROWS, ATTN_SUB), BF16),
                   jax.ShapeDtypeStruct((l, sw), F32),
                   jax.ShapeDtypeStruct((2, l, fw), BF16)],
        compiler_params=_cparams(("arbitrary",)),
        name="in_proj",
    )(x, gain, shift, scale, w_in, qn, kn, cos, sin, cs)


def _attn_kernel(nkb, tq, q_ref, k_ref, vt_ref, o_ref, qs_ref, s0_ref, s1_ref, bm0_ref, bm1_ref,
                 m_ref, acc_ref):
    j = pl.program_id(2)
    s_refs = (s0_ref, s1_ref)
    bm_refs = (bm0_ref, bm1_ref)

    tk = k_ref.shape[0]
    sub = vt_ref.shape[3]

    def step(score_slot, absorb_slot):
        if absorb_slot is not None:
            m_prev = m_ref[...]
            m_new = jnp.maximum(m_prev, bm_refs[absorb_slot][...])
            alpha = jnp.exp2(m_prev - m_new)
            pv = jnp.zeros(acc_ref.shape, F32)
        bmax = None
        for c in range(tk // sub):
            rows = slice(c * sub, (c + 1) * sub)
            if score_slot is not None:
                s = lax.dot_general(k_ref[rows, :], qs_ref[...], (((1,), (1,)), ((), ())),
                                    preferred_element_type=F32)
                s_refs[score_slot][rows, :] = s
                cmax = jnp.max(s, axis=0, keepdims=True)
                bmax = cmax if bmax is None else jnp.maximum(bmax, cmax)
            if absorb_slot is not None:
                p = jnp.exp2((s_refs[absorb_slot][rows, :] - m_new).astype(BF16))
                pv = pv + jnp.dot(vt_ref[0, c], p, preferred_element_type=F32)
        if score_slot is not None:
            bm_refs[score_slot][...] = bmax
        if absorb_slot is not None:
            acc_ref[...] = alpha * acc_ref[...] + pv
            m_ref[...] = m_new

    @pl.when(j == 0)
    def _first():
        for h in range(Q_PER_KV):
            qs_ref[h * tq:(h + 1) * tq, :] = q_ref[:, h * HEAD_DIM:(h + 1) * HEAD_DIM]
        m_ref[...] = jnp.full(m_ref.shape, -jnp.inf, F32)
        acc_ref[...] = jnp.zeros(acc_ref.shape, F32)
        step(0, None)

    for parity in (0, 1):
        @pl.when((j > 0) & (j < nkb) & (j % 2 == parity))
        def _mid(parity=parity):
            step(parity, 1 - parity)

    @pl.when(j == nkb)
    def _last():
        step(None, (nkb - 1) % 2)
        o = acc_ref[0:HEAD_DIM, :] / acc_ref[HEAD_DIM:HEAD_DIM + 1, :]
        for h in range(Q_PER_KV):
            o_ref[:, h * HEAD_DIM:(h + 1) * HEAD_DIM] = o[:, h * tq:(h + 1) * tq].T.astype(BF16)


def _pick_block(total, unit, cap):
    best = unit
    for b in range(unit, cap + 1, unit):
        if total % b == 0:
            best = b
    return best


def _attn_bounded_kernel(unroll, tq, kn_ref, q_ref, k_ref, vt_ref, kc_ref, vtc_ref, o_ref, qs_ref,
                         acc_ref):
    sub = vt_ref.shape[3]
    for h in range(Q_PER_KV):
        qs_ref[h * tq:(h + 1) * tq, :] = q_ref[:, h * HEAD_DIM:(h + 1) * HEAD_DIM]
    qf = qs_ref[...].astype(F32)
    norm2 = lax.dot_general(jnp.ones((8, HEAD_DIM), F32), qf * qf, (((1,), (1,)), ((), ())),
                            preferred_element_type=F32)[0:1, :]
    shift = jnp.sqrt(norm2) * kn_ref[pl.program_id(0)]

    def weighted_values(keys, vt):
        s = lax.dot_general(keys, qs_ref[...], (((1,), (1,)), ((), ())),
                            preferred_element_type=F32)
        return jnp.dot(vt, jnp.exp2(s - shift).astype(BF16), preferred_element_type=F32)

    acc = None
    for c in range(vtc_ref.shape[1]):
        d = weighted_values(kc_ref[c * sub:(c + 1) * sub, :], vtc_ref[0, c])
        acc = d if acc is None else acc + d
    acc_ref[...] = acc

    def body(t, carry):
        pv = None
        for u in range(unroll):
            c = t * unroll + u
            start = pl.multiple_of(c * sub, sub)
            d = weighted_values(k_ref[pl.ds(start, sub), :], vt_ref[0, c])
            pv = d if pv is None else pv + d
        acc_ref[...] += pv
        return carry

    lax.fori_loop(0, vt_ref.shape[1] // unroll, body, 0)
    o = acc_ref[0:HEAD_DIM, :] / acc_ref[HEAD_DIM:HEAD_DIM + 1, :]
    for h in range(Q_PER_KV):
        o_ref[:, h * HEAD_DIM:(h + 1) * HEAD_DIM] = o[:, h * tq:(h + 1) * tq].T.astype(BF16)


def _attn_bounded_call(q, k, vt, kc, vtc, k_norm_max):
    l = q.shape[0]
    nblk = vt.shape[1]
    tq = min(ATTN_TQ, l)
    gw = Q_PER_KV * HEAD_DIM
    n = Q_PER_KV * tq
    unroll = max(u for u in (8, 5, 4, 3, 2, 1) if nblk % u == 0)
    head_rows = lambda a: pl.BlockSpec((a.shape[0], HEAD_DIM), lambda g, i: (0, g))
    head_blocks = lambda a: pl.BlockSpec((1,) + a.shape[1:], lambda g, i: (g, 0, 0, 0))
    return pl.pallas_call(
        functools.partial(_attn_bounded_kernel, unroll, tq),
        grid=(N_KV_HEADS, l // tq),
        in_specs=[pl.BlockSpec(memory_space=pltpu.SMEM),
                  pl.BlockSpec((tq, gw), lambda g, i: (i, g)),
                  head_rows(k), head_blocks(vt), head_rows(kc), head_blocks(vtc)],
        out_specs=pl.BlockSpec((tq, gw), lambda g, i: (i, g)),
        out_shape=jax.ShapeDtypeStruct(q.shape, BF16),
        scratch_shapes=[pltpu.VMEM((n, HEAD_DIM), BF16), pltpu.VMEM((VT_ROWS, n), F32)],
        compiler_params=_cparams(("arbitrary", "arbitrary")),
        name="attention_bounded",
    )(k_norm_max, q, k, vt, kc, vtc)


def _attn_dispatch(q, k, vt, kc, vtc, q_gain, k_gain):
    bound = lambda gain: NORM_SLACK * math.sqrt(HEAD_DIM) * jnp.max(jnp.abs(gain.astype(F32)))
    qn = Q_SCALE * bound(q_gain)
    kn = jnp.full((N_KV_HEADS,), bound(k_gain), F32)
    safe = 2.0 * qn * jnp.max(kn) <= ATTN_MAX_GAP
    return lax.cond(safe,
                    lambda: _attn_bounded_call(q, k, vt, kc, vtc, kn),
                    lambda: _attn_call(q, jnp.concatenate([k, kc], axis=0),
                                       jnp.concatenate([vt, vtc], axis=1)))


def _attn_call(q, k_all, vt_all):
    l = q.shape[0]
    s_len = k_all.shape[0]
    sub = vt_all.shape[3]
    tq = min(ATTN_TQ, l)
    tk = _pick_block(s_len, sub, ATTN_TK)
    nkb = s_len // tk
    gw = Q_PER_KV * HEAD_DIM
    n = Q_PER_KV * tq
    return pl.pallas_call(
        functools.partial(_attn_kernel, nkb, tq),
        grid=(N_KV_HEADS, l // tq, nkb + 1),
        in_specs=[pl.BlockSpec((tq, gw), lambda g, i, j: (i, g)),
                  pl.BlockSpec((tk, HEAD_DIM), lambda g, i, j: (jnp.minimum(j, nkb - 1), g)),
                  pl.BlockSpec((1, tk // sub, VT_ROWS, sub),
                               lambda g, i, j: (g, jnp.maximum(j - 1, 0), 0, 0))],
        out_specs=pl.BlockSpec((tq, gw), lambda g, i, j: (i, g)),
        out_shape=jax.ShapeDtypeStruct(q.shape, BF16),
        scratch_shapes=[pltpu.VMEM((n, HEAD_DIM), BF16),
                        pltpu.VMEM((tk, n), F32),
                        pltpu.VMEM((tk, n), F32),
                        pltpu.VMEM((1, n), F32),
                        pltpu.VMEM((1, n), F32),
                        pltpu.VMEM((1, n), F32),
                        pltpu.VMEM((VT_ROWS, n), F32)],
        compiler_params=_cparams(("arbitrary", "arbitrary", "arbitrary")),
        name="attention",
    )(q, k_all, vt_all)


def _ssm_tables(lam_re, lam_im, log_dt, b_re, b_im, c_re, c_im, d):
    t = SSM_CHUNK
    g, p = lam_re.shape[1:]
    hdim = b_re.shape[-1]
    lr = lam_re.astype(F32)
    li = lam_im.astype(F32)
    dt = jnp.exp(log_dt.astype(F32))[..., None]
    mag = jnp.exp(lr * dt)
    ar = mag * jnp.cos(li * dt)
    ai = mag * jnp.sin(li * dt)
    den = lr * lr + li * li
    cr = ((ar - 1.0) * lr + ai * li) / den
    ci = (ai * lr - (ar - 1.0) * li) / den
    br = b_re.astype(F32)
    bi = b_im.astype(F32)
    bbr = cr[..., None] * br - ci[..., None] * bi
    bbi = cr[..., None] * bi + ci[..., None] * br
    n = jnp.arange(t + 1, dtype=F32)[:, None, None, None]
    pmag = jnp.exp(n * (lr * dt)[None])
    pr = pmag * jnp.cos(n * (li * dt)[None])
    pi = pmag * jnp.sin(n * (li * dt)[None])
    cre = c_re.astype(F32)
    cim = c_im.astype(F32)

    gs = SSM_LANES // hdim
    n_sg = g // gs

    def selector(r, inner):
        e = np.zeros((gs, r, gs, inner, r, inner), np.float32)
        for grp in range(gs):
            e[grp, :, grp] = np.eye(r * inner, dtype=np.float32).reshape(r, inner, r, inner)
        return jnp.asarray(e.reshape(gs, r * gs * inner, r * inner), dtype=BF16)

    def expand(small):
        _, r1, i1, r2, i2 = small.shape
        blocks = small.reshape(n_sg, gs, r1 * i1, r2 * i2).astype(BF16)
        wide = jnp.einsum('sgkl,gcl->sgkc', blocks, selector(r2, i2),
                          preferred_element_type=F32).astype(BF16)
        wide = wide.reshape(n_sg, gs, r1, i1, r2 * gs * i2)
        return jnp.transpose(wide, (0, 2, 1, 3, 4)).reshape(n_sg, r1 * gs * i1, r2 * gs * i2)

    def summ(direction, powers):
        er = pr[powers, direction][..., None] * bbr[direction][None] \
            - pi[powers, direction][..., None] * bbi[direction][None]
        ei = pr[powers, direction][..., None] * bbi[direction][None] \
            + pi[powers, direction][..., None] * bbr[direction][None]
        tr = lambda e: jnp.transpose(e, (1, 0, 3, 2))
        return expand(jnp.stack([tr(er), tr(ei)], axis=3))
    steps = jnp.arange(t)
    p_f = summ(0, t - 1 - steps)
    p_b = summ(1, steps)

    row = lambda v: v.reshape(n_sg, gs * p)
    a_row = jnp.concatenate([row(pr[t, 0]), row(pr[t, 1]), row(pi[t, 0]), row(pi[t, 1])],
                            axis=-1)[:, None, :]

    def kern(direction):
        er = pr[:t, direction][..., None] * bbr[direction][None] \
            - pi[:t, direction][..., None] * bbi[direction][None]
        ei = pr[:t, direction][..., None] * bbi[direction][None] \
            + pi[:t, direction][..., None] * bbr[direction][None]
        return (jnp.einsum('gop,ngpi->ngoi', cre[direction], er)
                - jnp.einsum('gop,ngpi->ngoi', cim[direction], ei))
    kf = kern(0)
    kb = kern(1)
    s_idx = steps[:, None]
    t_idx = steps[None, :]
    lag_f = jnp.clip(t_idx - s_idx, 0, t - 1)
    lag_b = jnp.clip(s_idx - t_idx, 0, t - 1)
    mf = jnp.where((t_idx >= s_idx)[:, :, None, None, None], kf[lag_f], 0.0)
    mb = jnp.where((s_idx >= t_idx)[:, :, None, None, None], kb[lag_b], 0.0)
    dd = d.astype(F32).reshape(g, hdim)
    eye_t = jnp.eye(t, dtype=F32)[:, :, None, None, None]
    eye_c = jnp.eye(hdim, dtype=F32)[None, None, None]
    m5 = mf + mb + eye_t * eye_c * dd[None, None, :, :, None]
    m_mat = expand(jnp.transpose(m5, (2, 0, 4, 1, 3)))

    def state_out(direction, powers):
        prn = pr[powers, direction]
        pin = pi[powers, direction]
        wr = cre[direction][None] * prn[:, :, None, :] - cim[direction][None] * pin[:, :, None, :]
        wi = -cre[direction][None] * pin[:, :, None, :] - cim[direction][None] * prn[:, :, None, :]
        tr = lambda w: jnp.transpose(w, (1, 3, 0, 2))
        return expand(jnp.stack([tr(wr), tr(wi)], axis=1))
    q_f = state_out(0, steps + 1)
    q_b = state_out(1, t - steps)
    return p_f, p_b, m_mat, q_f, q_b, a_row


def _ssm_kernel(tbk, usf_ref, usb_ref, pf_ref, pb_ref, m_ref, qf_ref, qb_ref, a_ref, h0_ref,
                yf_ref, yb_ref, hend_ref, xf_ref, xb_ref, sf_ref, sb_ref, hf_ref, hb_ref, h_ref):
    t = SSM_CHUNK
    lanes = SSM_LANES

    @pl.when(pl.program_id(1) == 0)
    def _init():
        h_ref[...] = h0_ref[0]

    for tt in range(t):
        cols = slice(tt * lanes, (tt + 1) * lanes)
        xf_ref[:, cols] = usf_ref[pl.ds(tt, tbk, stride=t), :].astype(BF16)
        xb_ref[:, cols] = usb_ref[pl.ds(tt, tbk, stride=t), :].astype(BF16)
    sf_ref[...] = jnp.dot(xf_ref[...], pf_ref[0, 0], preferred_element_type=F32)
    sb_ref[...] = jnp.dot(xb_ref[...], pb_ref[0, 0], preferred_element_type=F32)

    w = a_ref.shape[3] // 4
    a = a_ref[0, 0]
    arf, arb, aif, aib = (a[:, i * w:(i + 1) * w] for i in range(4))
    h = h_ref[...]

    def body(k, carry):
        hrf, hrb, hif, hib = carry
        kb = tbk - 1 - k
        hf_ref[pl.ds(k, 1), 0:w] = hrf
        hf_ref[pl.ds(k, 1), w:2 * w] = hif
        hb_ref[pl.ds(kb, 1), 0:w] = hrb
        hb_ref[pl.ds(kb, 1), w:2 * w] = hib
        sf = sf_ref[pl.ds(k, 1), :]
        sb = sb_ref[pl.ds(kb, 1), :]
        return (arf * hrf - aif * hif + sf[:, 0:w], arb * hrb - aib * hib + sb[:, 0:w],
                arf * hif + aif * hrf + sf[:, w:2 * w], arb * hib + aib * hrb + sb[:, w:2 * w])

    carry = lax.fori_loop(0, tbk, body, tuple(h[:, i * w:(i + 1) * w] for i in range(4)),
                          unroll=8)
    h = jnp.concatenate(carry, axis=1)
    h_ref[...] = h
    hend_ref[0] = h

    yf = (jnp.dot(xf_ref[...], m_ref[0, 0], preferred_element_type=F32)
          + jnp.dot(hf_ref[...].astype(BF16), qf_ref[0, 0], preferred_element_type=F32))
    yb = jnp.dot(hb_ref[...].astype(BF16), qb_ref[0, 0], preferred_element_type=F32)
    for tt in range(t):
        cols = slice(tt * lanes, (tt + 1) * lanes)
        yf_ref[pl.ds(tt, tbk, stride=t), :] = yf[:, cols]
        yb_ref[pl.ds(tt, tbk, stride=t), :] = yb[:, cols]


def _ssm_apply(us, tables, layer, h0):
    p_f, p_b, m_mat, q_f, q_b, a_row = tables
    l, w = us.shape
    n_sg = w // SSM_LANES
    n = l // SSM_CHUNK
    tbk = min(256, n)
    nb = n // tbk
    rows = tbk * SSM_CHUNK
    kx = SSM_CHUNK * SSM_LANES
    sw = p_f.shape[3]
    hw = a_row.shape[3]
    per_sg = lambda a: pl.BlockSpec((1,) + a.shape[1:], lambda s, i: (s, 0, 0))
    table = lambda a: pl.BlockSpec((1, 1) + a.shape[2:], lambda s, i: (layer, s, 0, 0))
    fwd = pl.BlockSpec((rows, SSM_LANES), lambda s, i: (i, s))
    bwd = pl.BlockSpec((rows, SSM_LANES), lambda s, i: (nb - 1 - i, s))
    return pl.pallas_call(
        functools.partial(_ssm_kernel, tbk),
        grid=(n_sg, nb),
        in_specs=[fwd, bwd, table(p_f), table(p_b), table(m_mat), table(q_f), table(q_b),
                  table(a_row), per_sg(h0)],
        out_specs=[fwd, bwd, per_sg(h0)],
        out_shape=[jax.ShapeDtypeStruct((l, w), F32), jax.ShapeDtypeStruct((l, w), F32),
                   jax.ShapeDtypeStruct(h0.shape, F32)],
        scratch_shapes=[pltpu.VMEM((tbk, kx), BF16), pltpu.VMEM((tbk, kx), BF16),
                        pltpu.VMEM((tbk, sw), F32), pltpu.VMEM((tbk, sw), F32),
                        pltpu.VMEM((tbk, sw), F32), pltpu.VMEM((tbk, sw), F32),
                        pltpu.VMEM((1, hw), F32)],
        compiler_params=_cparams(("arbitrary", "arbitrary")),
        name="ssm_scan",
    )(us, us, p_f, p_b, m_mat, q_f, q_b, a_row, h0)


def _dft_tables(l, n1, hw):
    n2 = l // n1
    scale = 1.0 / math.sqrt(l * hw)
    if n1 == 1:
        t1 = None
    else:
        ang = 2.0 * np.pi * ((np.arange(n1)[:, None] * np.arange(n1)[None, :]) % n1) / n1
        c, s = np.cos(ang), np.sin(ang)
        t1 = jnp.asarray(np.block([[c, -s], [-s, -c]]), dtype=F32)
    k1 = np.arange(n1)[:, None, None]
    k2 = np.arange(n2)[None, :, None]
    m = np.arange(n2)[None, None, :]
    ang = 2.0 * np.pi * ((m * (n1 * k2 + k1)) % l) / l
    sign = -1.0 if n1 == 1 else 1.0
    g = np.concatenate([np.cos(ang), sign * np.sin(ang)], axis=-1) * scale
    return t1, jnp.asarray(g, dtype=F32)


def _dft1_kernel(t_ref, x_ref, z_ref):
    z_ref[...] = jnp.dot(t_ref[...].astype(BF16), x_ref[...],
                         preferred_element_type=F32).astype(z_ref.dtype)


def _dft1_call(t1, ab2d):
    r, n = ab2d.shape
    tn = min(4096, n)
    return pl.pallas_call(
        _dft1_kernel,
        grid=(n // tn,),
        in_specs=[pl.BlockSpec((r, r), lambda j: (0, 0)),
                  pl.BlockSpec((r, tn), lambda j: (0, j))],
        out_specs=pl.BlockSpec((r, tn), lambda j: (0, j)),
        out_shape=jax.ShapeDtypeStruct((r, n), BF16),
        compiler_params=_cparams(("arbitrary",)),
        name="dft_stage1",
    )(t1, ab2d)


def _dft2_kernel(bt, zr_ref, zi_ref, g_ref, w_ref, o_ref):
    fw = w_ref.shape[1]
    for b in range(bt):
        z = jnp.concatenate([zr_ref[b], zi_ref[b]], axis=0)
        x = jnp.dot(g_ref[b].astype(BF16), z, preferred_element_type=F32)
        o_ref[:, b * fw:(b + 1) * fw] = jnp.dot(
            x.astype(BF16), w_ref[0], preferred_element_type=F32).astype(o_ref.dtype)


def _dft2_call(z3, g, fourier_w, layer):
    n1x2, n2, fw = z3.shape
    n1 = n1x2 // 2
    bt = min(8, n1)
    return pl.pallas_call(
        functools.partial(_dft2_kernel, bt),
        grid=(n1 // bt,),
        in_specs=[pl.BlockSpec((bt, n2, fw), lambda b: (b, 0, 0)),
                  pl.BlockSpec((bt, n2, fw), lambda b: (b + n1 // bt, 0, 0)),
                  pl.BlockSpec((bt, n2, 2 * n2), lambda b: (b, 0, 0)),
                  _layer_block(fourier_w, layer)],
        out_specs=pl.BlockSpec((n2, bt * fw), lambda b: (0, b)),
        out_shape=jax.ShapeDtypeStruct((n2, n1 * fw), BF16),
        compiler_params=_cparams(("arbitrary",)),
        name="dft_stage2",
    )(z3, z3, g, fourier_w)


def _fourier_apply(ab, tables, fourier_w, layer, n1):
    t1, g = tables
    _, l, fw = ab.shape
    n2 = l // n1
    if n1 == 1:
        z3 = ab
    else:
        z3 = _dft1_call(t1, ab.reshape(2 * n1, n2 * fw)).reshape(2 * n1, n2, fw)
    return _dft2_call(z3, g, fourier_w, layer).reshape(l, fw)


def _outproj_kernel(attn_ref, yf_ref, yb_ref, four_ref, x_ref, wo_ref, gw_ref, gb_ref, npost_ref,
                    gate_ref, npre_ref, sh_ref, sc_ref, xo_ref, h_ref):
    tm = x_ref.shape[0]
    n_split = 2 if tm % 32 == 0 else 1
    for k in range(n_split):
        rows = slice(k * (tm // n_split), (k + 1) * (tm // n_split))
        ys = yf_ref[rows, :] + yb_ref[rows, :]
        gl = 0.5 * ys * (1.0 + jnp.tanh(math.sqrt(2.0 / math.pi)
                                        * (ys + 0.044715 * (ys * ys * ys))))
        z = jnp.dot(gl.astype(BF16), gw_ref[0], preferred_element_type=F32) + gb_ref[...]
        ssm = (gl * jax.nn.sigmoid(z)).astype(BF16)
        cat = jnp.concatenate([attn_ref[rows, :], ssm, four_ref[rows, :]], axis=-1)
        y = jnp.dot(cat, wo_ref[0], preferred_element_type=F32)
        xn = x_ref[rows, :] + gate_ref[...] * _rms(y, npost_ref[...])
        xo_ref[rows, :] = xn
        h_ref[rows, :] = (_rms(xn, npre_ref[...]) * (1.0 + sc_ref[...])
                          + sh_ref[...]).astype(BF16)


def _outproj_call(attn, yf, yb, four, x, w_out, glu_w, layer, glu_b, npost, gate, npre, shift,
                  scale):
    l, d = x.shape
    tm = min(512, l)
    row = lambda i: (i, 0)
    fix = lambda i: (0, 0)
    vec = pl.BlockSpec((1, d), fix)
    return pl.pallas_call(
        _outproj_kernel,
        grid=(l // tm,),
        in_specs=[pl.BlockSpec((tm, attn.shape[1]), row),
                  pl.BlockSpec((tm, yf.shape[1]), row),
                  pl.BlockSpec((tm, yb.shape[1]), row),
                  pl.BlockSpec((tm, four.shape[1]), row),
                  pl.BlockSpec((tm, d), row),
                  _layer_block(w_out, layer),
                  _layer_block(glu_w, layer),
                  pl.BlockSpec((1, glu_w.shape[2]), fix),
                  vec, vec, vec, vec, vec],
        out_specs=[pl.BlockSpec((tm, d), row), pl.BlockSpec((tm, d), row)],
        out_shape=[jax.ShapeDtypeStruct((l, d), F32), jax.ShapeDtypeStruct((l, d), BF16)],
        compiler_params=_cparams(("arbitrary",)),
        name="out_proj",
    )(attn, yf, yb, four, x, w_out, glu_w, glu_b, npost, gate, npre, shift, scale)


def _ffn_kernel(nj1, nj2, th, tn, h_ref, x_ref, wg_ref, wu_ref, wd_ref, npost_ref, gate_ref, o_ref,
                act_ref, tmp_ref):
    j = pl.program_id(1)

    @pl.when(j < nj1)
    def _up():
        h = h_ref[...]
        a = jnp.dot(h, wg_ref[0], preferred_element_type=F32)
        u = jnp.dot(h, wu_ref[0], preferred_element_type=F32)
        tmp_ref[...] = (a * jax.nn.sigmoid(a) * u).astype(BF16)

    for c in range(nj1):
        @pl.when(j == c)
        def _place(c=c):
            act_ref[:, c * th:(c + 1) * th] = tmp_ref[...]

    for c in range(nj2):
        @pl.when(j == nj1 + c)
        def _down(c=c):
            o_ref[:, c * tn:(c + 1) * tn] = jnp.dot(act_ref[...], wd_ref[0],
                                                    preferred_element_type=F32)

    @pl.when(j == nj1 + nj2 - 1)
    def _finish():
        o_ref[...] = x_ref[...] + gate_ref[...] * _rms(o_ref[...], npost_ref[...])


def _ffn_call(h, x, w_gate, w_up, w_down, layer, npost, gate):
    l, d = x.shape
    fh = w_gate.shape[2]
    th, tn = FFN_TH, FFN_TN
    nj1 = fh // th
    nj2 = d // tn
    tm = min(FFN_TM, l)
    vec = pl.BlockSpec((1, d), lambda i, j: (0, 0))
    up = pl.BlockSpec((1, d, th), lambda i, j: (layer, 0, jnp.minimum(j, nj1 - 1)))
    once = pl.Buffered(1)
    return pl.pallas_call(
        functools.partial(_ffn_kernel, nj1, nj2, th, tn),
        grid=(l // tm, nj1 + nj2),
        in_specs=[pl.BlockSpec((tm, d), lambda i, j: (i, 0)),
                  pl.BlockSpec((tm, d), lambda i, j: (i, 0), pipeline_mode=once),
                  up, up,
                  pl.BlockSpec((1, fh, tn), lambda i, j: (layer, 0, jnp.maximum(j - nj1, 0))),
                  vec, vec],
        out_specs=pl.BlockSpec((tm, d), lambda i, j: (i, 0), pipeline_mode=once),
        out_shape=jax.ShapeDtypeStruct((l, d), F32),
        scratch_shapes=[pltpu.VMEM((tm, fh), BF16), pltpu.VMEM((tm, th), BF16)],
        compiler_params=_cparams(("arbitrary", "arbitrary"), FFN_VMEM_MB),
        name="ffn",
    )(h, x, w_gate, w_up, w_down, npost, gate)


def _rope_tables(l):
    t = np.arange(l)
    row = (t // GRID_W).astype(np.float64)
    col = (t % GRID_W).astype(np.float64)
    freqs = ROPE_THETA ** (-np.arange(ROPE_PAIRS, dtype=np.float64) / ROPE_PAIRS)
    ang_r = row[:, None] * freqs
    ang_c = col[:, None] * freqs
    cos = np.concatenate([np.cos(ang_r)] * 2 + [np.cos(ang_c)] * 2, axis=-1)
    sin = np.concatenate([-np.sin(ang_r), np.sin(ang_r), -np.sin(ang_c), np.sin(ang_c)], axis=-1)
    return jnp.asarray(cos, dtype=F32), jnp.asarray(sin, dtype=F32)


def _channel_dft_table(fw):
    hw = FOURIER_HEAD_DIM
    ang = 2.0 * np.pi * ((np.arange(hw)[:, None] * np.arange(hw)[None, :]) % hw) / hw
    eye = np.eye(fw // hw)
    return jnp.asarray(np.concatenate([np.kron(eye, np.cos(ang)), np.kron(eye, np.sin(ang))], axis=1),
                       dtype=F32)


def _dft_split(l):
    n1 = 1
    while n1 * n1 < l:
        n1 *= 2
    return n1 if (l >= 1024 and n1 * n1 == l) else 1


def kernel(x, c, ctx, c_ctx, ada_w, ada_b, norm_mix_pre, norm_mix_post, norm_ffn_pre, norm_ffn_post, w_in, q_norm, k_norm, ssm_lam_re, ssm_lam_im, ssm_log_dt, ssm_b_re, ssm_b_im, ssm_c_re, ssm_c_im, ssm_d, ssm_glu_w, ssm_glu_b, fourier_w, w_out, ffn_w_gate, ffn_w_up, ffn_w_down):
    depth = ada_w.shape[0]
    _, l, d = x.shape
    n_ctx = ctx.shape[1]
    fw = fourier_w.shape[1]
    sw = ssm_d.shape[1]

    cond = jnp.zeros((8, d), F32).at[0].set(c[0]).at[1].set(c_ctx)
    mod = _ada_call(cond, ada_w, ada_b)

    cos, sin = _rope_tables(l)
    zero_tab = jnp.zeros((n_ctx, HEAD_DIM), F32)
    cs = _channel_dft_table(fw)
    n1 = _dft_split(l)
    n1c = _dft_split(n_ctx)
    dft_x = _dft_tables(l, n1, FOURIER_HEAD_DIM)
    dft_c = _dft_tables(n_ctx, n1c, FOURIER_HEAD_DIM)
    h0 = jnp.zeros((sw // SSM_LANES, 1, 4 * (SSM_LANES // SSM_GROUP) * SSM_STATE), F32)

    w_in_b = w_in.astype(BF16)
    w_out_b = w_out.astype(BF16)
    glu_w_b = ssm_glu_w.astype(BF16)
    four_w_b = fourier_w.astype(BF16)
    wg_b = ffn_w_gate.astype(BF16)
    wu_b = ffn_w_up.astype(BF16)
    wd_b = ffn_w_down.astype(BF16)
    tables = jax.vmap(_ssm_tables)(ssm_lam_re, ssm_lam_im, ssm_log_dt, ssm_b_re, ssm_b_im,
                                   ssm_c_re, ssm_c_im, ssm_d)

    xs = x[0]
    xc = ctx[0]
    for layer in range(depth):
        need_ctx = layer < depth - 1
        vecs = lambda r: [mod[layer, r:r + 1, i * d:(i + 1) * d] for i in range(6)]
        sh_m, sc_m, g_m, sh_f, sc_f, g_f = vecs(0)
        shc_m, scc_m, gc_m, shc_f, scc_f, gc_f = vecs(1)
        row = lambda a: a[layer].reshape(1, -1)

        qc, kc, vtc, usc, abc = _inproj_call(
            xc, row(norm_mix_pre), shc_m, scc_m, w_in_b, layer, row(q_norm), row(k_norm),
            zero_tab, zero_tab, cs, False)
        yfc, ybc, hend_c = _ssm_apply(usc, tables, layer, h0)

        q, k, vt, us, ab = _inproj_call(xs, row(norm_mix_pre), sh_m, sc_m, w_in_b, layer,
                                        row(q_norm), row(k_norm), cos, sin, cs, True)
        attn = _attn_dispatch(q, k, vt, kc, vtc, q_norm[layer], k_norm[layer])
        yf, yb, _ = _ssm_apply(us, tables, layer, hend_c)
        four = _fourier_apply(ab, dft_x, four_w_b, layer, n1)
        xs, hs = _outproj_call(attn, yf, yb, four, xs, w_out_b, glu_w_b, layer, row(ssm_glu_b),
                               row(norm_mix_post), g_m, row(norm_ffn_pre), sh_f, sc_f)
        xs = _ffn_call(hs, xs, wg_b, wu_b, wd_b, layer, row(norm_ffn_post), g_f)

        if need_ctx:
            attn_c = _attn_call(qc, kc, vtc)
            four_c = _fourier_apply(abc, dft_c, four_w_b, layer, n1c)
            xc, hc = _outproj_call(attn_c, yfc, ybc, four_c, xc, w_out_b, glu_w_b, layer,
                                   row(ssm_glu_b), row(norm_mix_post), gc_m, row(norm_ffn_pre),
                                   shc_f, scc_f)
            xc = _ffn_call(hc, xc, wg_b, wu_b, wd_b, layer, row(norm_ffn_post), gc_f)
    return xs[None]
```
